```python
import math
import jax
import jax.numpy as jnp
from jax import lax

D_MODEL = 1024
BATCH = 32
SEQ = 2048
DEPTH = 4

CTX_LEN = 256
GRID_W = 64
D_MIX = D_MODEL
DIFF_HEADS = 4
DIFF_HEAD_DIM = 32
DIFF_V_DIM = 2 * DIFF_HEAD_DIM
DIFF_WIDTH = DIFF_HEADS * DIFF_V_DIM
HGRN_HEADS = 4
HGRN_K = 128
HGRN_V = 128
HGRN_WIDTH = HGRN_HEADS * HGRN_V
HGRN_CHUNK = 32
SWA_Q_HEADS = 4
SWA_KV_HEADS = 2
SWA_GROUP = SWA_Q_HEADS // SWA_KV_HEADS
SWA_HEAD_DIM = 64
SWA_WIDTH = SWA_Q_HEADS * SWA_HEAD_DIM
WINDOW = 128
Q_BLOCK = 128
D_FF = 2816
N_EXPERTS = 8
TOP_K = 2
ROPE_BASE = 10000.0
NORM_EPS = 1e-6
IN_SIZES = (DIFF_WIDTH,) * 3 + (HGRN_WIDTH,) * 5 + (SWA_WIDTH, SWA_KV_HEADS * SWA_HEAD_DIM, SWA_KV_HEADS * SWA_HEAD_DIM)
D_IN = sum(IN_SIZES)
N_DENSE = (DEPTH + 1) // 2
N_MOE = DEPTH // 2
F32 = jnp.float32

kernel_name = 'hymba_style_diffusion_hybrid_trunk'


def rmsnorm(x, g):
    xf = x.astype(F32)
    y = xf * lax.rsqrt(jnp.mean(xf * xf, axis=-1, keepdims=True) + NORM_EPS)
    return (y * g.astype(F32)).astype(x.dtype)


def grid_positions(n_tokens):
    rows = n_tokens // GRID_W
    row = jnp.repeat(jnp.arange(rows, dtype=jnp.int32), GRID_W)
    col = jnp.tile(jnp.arange(GRID_W, dtype=jnp.int32), rows)
    return row, col


def _rotate(xp, pos):
    nf = xp.shape[-1] // 2
    inv = ROPE_BASE ** (-jnp.arange(nf, dtype=F32) / nf)
    ang = pos.astype(F32)[:, None] * inv[None, :]
    shape = (1, pos.shape[0]) + (1,) * (xp.ndim - 3) + (nf,)
    cos = jnp.cos(ang).reshape(shape).astype(xp.dtype)
    sin = jnp.sin(ang).reshape(shape).astype(xp.dtype)
    x1, x2 = xp[..., :nf], xp[..., nf:]
    return jnp.concatenate([x1 * cos - x2 * sin, x1 * sin + x2 * cos], axis=-1)


def axial_rope(x, row, col):
    half = x.shape[-1] // 2
    return jnp.concatenate([_rotate(x[..., :half], row), _rotate(x[..., half:], col)], axis=-1)


def map_query_blocks(fn, q):
    b, n = q.shape[:2]
    nb = n // Q_BLOCK
    qb = jnp.moveaxis(q.reshape((b, nb, Q_BLOCK) + q.shape[2:]), 1, 0)
    out = lax.map(lambda a: fn(a[0], a[1]), (qb, jnp.arange(nb, dtype=jnp.int32)))
    out = jnp.moveaxis(out, 0, 1)
    return out.reshape((b, n) + out.shape[3:])


def split_projection(h, w):
    p = jnp.einsum('bld,de->ble', h, w)
    points, acc = [], 0
    for s in IN_SIZES[:-1]:
        acc += s
        points.append(acc)
    return jnp.split(p, points, axis=-1)


def _diff_core(q, k, v, lam):
    s = jnp.einsum('bqhmd,bkhmd->bhmqk', q, k, preferred_element_type=F32) * (DIFF_HEAD_DIM ** -0.5)
    p = jax.nn.softmax(s, axis=-1)
    a = p[:, :, 0] - lam * p[:, :, 1]
    return jnp.einsum('bhqk,bkhe->bqhe', a.astype(v.dtype), v)


def diff_attention_mixer(lat, ctx, row, col, qk_g, lam_vec, lam_init, sub_g, need_ctx):
    def prep(q, k, v, rotary):
        b, n, _ = q.shape
        q = rmsnorm(q.reshape(b, n, DIFF_HEADS, 2, DIFF_HEAD_DIM), qk_g[0])
        k = rmsnorm(k.reshape(b, n, DIFF_HEADS, 2, DIFF_HEAD_DIM), qk_g[1])
        if rotary:
            q, k = axial_rope(q, row, col), axial_rope(k, row, col)
        return q, k, v.reshape(b, n, DIFF_HEADS, DIFF_V_DIM)

    q_l, k_l, v_l = prep(*lat, True)
    q_c, k_c, v_c = prep(*ctx, False)
    lv = lam_vec.astype(F32)
    lam = jnp.exp(jnp.sum(lv[0] * lv[1])) - jnp.exp(jnp.sum(lv[2] * lv[3])) + lam_init
    k_all = jnp.concatenate([k_c, k_l], axis=1)
    v_all = jnp.concatenate([v_c, v_l], axis=1)

    def post(o):
        b, n = o.shape[:2]
        return (rmsnorm(o, sub_g) * (1.0 - lam_init)).reshape(b, n, DIFF_WIDTH)

    o_l = post(map_query_blocks(lambda qb, i: _diff_core(qb, k_all, v_all, lam), q_l))
    o_c = post(_diff_core(q_c, k_c, v_c, lam)) if need_ctx else None
    return o_l, o_c


def _hgrn_gates(f_logit, lb):
    lb = lb.reshape(HGRN_HEADS, HGRN_K).astype(F32)
    log_f = jnp.logaddexp(jnp.log(lb), jnp.log1p(-lb) + jax.nn.log_sigmoid(f_logit))
    return -jnp.expm1(log_f), log_f


def _hgrn2_chunkwise(q, k, v, log_f, s0):
    b, n, h, kd = q.shape
    nc = n // HGRN_CHUNK

    def to_chunks(t):
        return jnp.moveaxis(t.reshape(b, nc, HGRN_CHUNK, h, t.shape[-1]), 1, 0)

    causal = jnp.tril(jnp.ones((HGRN_CHUNK, HGRN_CHUNK), dtype=bool))[None, :, :, None, None]

    def step(s, inp):
        qc, kc, vc, lfc = inp
        cum = jnp.cumsum(lfc, axis=1)
        dec = jnp.exp(jnp.where(causal, cum[:, :, None] - cum[:, None], -jnp.inf))
        scores = jnp.sum(qc[:, :, None] * kc[:, None] * dec, axis=-1)
        o = jnp.einsum('btsh,bshv->bthv', scores, vc)
        o = o + jnp.einsum('bthk,bhkv->bthv', qc * jnp.exp(cum), s)
        last = cum[:, -1]
        k_dec = kc * jnp.exp(last[:, None] - cum)
        s_new = jnp.exp(last)[..., None] * s + jnp.einsum('bshk,bshv->bhkv', k_dec, vc)
        return s_new, o

    s_fin, o = lax.scan(step, s0, (to_chunks(q), to_chunks(k), to_chunks(v), to_chunks(log_f)))
    o = jnp.moveaxis(o, 0, 1).reshape(b, n, h, v.shape[-1])
    return o, s_fin


def hgrn2_mixer(lat, ctx, lb_fwd, lb_bwd, norm_g, need_ctx):
    def prep(q, i, f_fwd, f_bwd, g):
        b, n, _ = q.shape
        heads = lambda t: t.reshape(b, n, HGRN_HEADS, -1).astype(F32)
        return (jax.nn.silu(heads(q)), heads(i), _hgrn_gates(heads(f_fwd), lb_fwd),
                _hgrn_gates(heads(f_bwd), lb_bwd), g)

    q_l, i_l, (kf_l, lf_l), (kb_l, lb_l), g_l = prep(*lat)
    q_c, i_c, (kf_c, lf_c), (kb_c, lb_c), g_c = prep(*ctx)
    s0 = jnp.zeros((q_l.shape[0], HGRN_HEADS, HGRN_K, HGRN_V), F32)
    rev = lambda t: jnp.flip(t, axis=1)
    o_cf, s_cf = _hgrn2_chunkwise(q_c, kf_c, i_c, lf_c, s0)
    o_lf, _ = _hgrn2_chunkwise(q_l, kf_l, i_l, lf_l, s_cf)
    o_cb, s_cb = _hgrn2_chunkwise(rev(q_c), rev(kb_c), rev(i_c), rev(lb_c), s0)
    o_lb, _ = _hgrn2_chunkwise(rev(q_l), rev(kb_l), rev(i_l), rev(lb_l), s_cb)

    def post(o, g):
        b, n = o.shape[:2]
        o = rmsnorm(o, norm_g).reshape(b, n, HGRN_WIDTH)
        return (o * jax.nn.silu(g.astype(F32))).astype(g.dtype)

    out_l = post(o_lf + rev(o_lb), g_l)
    out_c = post(o_cf + rev(o_cb), g_c) if need_ctx else None
    return out_l, out_c


def _sink_softmax(logits, sink):
    m = jnp.maximum(jnp.max(logits, axis=-1, keepdims=True), sink)
    e = jnp.exp(logits - m)
    return e / (jnp.sum(e, axis=-1, keepdims=True) + jnp.exp(sink - m))


def swa_mixer(lat, ctx, row, col, qk_g, sink, need_ctx):
    def prep(q, k, v, rotary):
        b, n, _ = q.shape
        q = rmsnorm(q.reshape(b, n, SWA_Q_HEADS, SWA_HEAD_DIM), qk_g[0])
        k = rmsnorm(k.reshape(b, n, SWA_KV_HEADS, SWA_HEAD_DIM), qk_g[1])
        if rotary:
            q, k = axial_rope(q, row, col), axial_rope(k, row, col)
        return q.reshape(b, n, SWA_KV_HEADS, SWA_GROUP, SWA_HEAD_DIM), k, v.reshape(b, n, SWA_KV_HEADS, SWA_HEAD_DIM)

    q_l, k_l, v_l = prep(*lat, True)
    q_c, k_c, v_c = prep(*ctx, False)
    scale = SWA_HEAD_DIM ** -0.5
    sink_b = sink.astype(F32).reshape(1, SWA_KV_HEADS, SWA_GROUP, 1, 1)
    b, n_lat = q_l.shape[:2]
    n_ctx = k_c.shape[1]
    span = Q_BLOCK + 2 * WINDOW
    pad = ((0, 0), (WINDOW, WINDOW), (0, 0), (0, 0))
    kp, vp = jnp.pad(k_l, pad), jnp.pad(v_l, pad)

    def block(qb, i):
        start = i * Q_BLOCK
        kw = lax.dynamic_slice_in_dim(kp, start, span, axis=1)
        vw = lax.dynamic_slice_in_dim(vp, start, span, axis=1)
        s_w = jnp.einsum('bqhgd,bkhd->bhgqk', qb, kw, preferred_element_type=F32) * scale
        qpos = start + jnp.arange(Q_BLOCK)
        kpos = start - WINDOW + jnp.arange(span)
        valid = (jnp.abs(qpos[:, None] - kpos[None, :]) <= WINDOW) & (kpos[None, :] >= 0) & (kpos[None, :] < n_lat)
        s_w = jnp.where(valid, s_w, -jnp.inf)
        s_c = jnp.einsum('bqhgd,bkhd->bhgqk', qb, k_c, preferred_element_type=F32) * scale
        p = _sink_softmax(jnp.concatenate([s_c, s_w], axis=-1), sink_b).astype(v_l.dtype)
        return (jnp.einsum('bhgqk,bkhd->bqhgd', p[..., :n_ctx], v_c)
                + jnp.einsum('bhgqk,bkhd->bqhgd', p[..., n_ctx:], vw))

    o_l = map_query_blocks(block, q_l).reshape(b, n_lat, SWA_WIDTH)
    o_c = None
    if need_ctx:
        s = jnp.einsum('bqhgd,bkhd->bhgqk', q_c, k_c, preferred_element_type=F32) * scale
        p = _sink_softmax(s, sink_b).astype(v_c.dtype)
        o_c = jnp.einsum('bhgqk,bkhd->bqhgd', p, v_c).reshape(b, n_ctx, SWA_WIDTH)
    return o_l, o_c


def swiglu(h, w1, w3, w2):
    return jnp.matmul(jax.nn.silu(jnp.matmul(h, w1)) * jnp.matmul(h, w3), w2)


def moe_swiglu(h, router, w1, w3, w2):
    logits = jnp.einsum('bld,de->ble', h, router, preferred_element_type=F32)
    top_v, top_i = lax.top_k(logits, TOP_K)
    wts = jax.nn.softmax(top_v, axis=-1)
    gates = jnp.sum(jax.nn.one_hot(top_i, N_EXPERTS, dtype=F32) * wts[..., None], axis=-2)
    y = jnp.zeros_like(h)
    for e in range(N_EXPERTS):
        y = y + gates[..., e:e + 1].astype(h.dtype) * swiglu(h, w1[e], w3[e], w2[e])
    return y


def channel_mixer(h, layer, ffn_w1, ffn_w3, ffn_w2, moe_router, moe_w1, moe_w3, moe_w2):
    j = layer // 2
    if layer % 2 == 0:
        return swiglu(h, ffn_w1[j], ffn_w3[j], ffn_w2[j])
    return moe_swiglu(h, moe_router[j], moe_w1[j], moe_w3[j], moe_w2[j])


def setup_inputs(seed: int = 0) -> dict:
    key = jax.random.key(seed)
    ks = jax.random.split(key, 24)
    nrm = lambda k, shape, s: jax.random.normal(k, shape, F32) * s
    gain = lambda k, shape: 1.0 + 0.02 * jax.random.normal(k, shape, F32)
    return {
        'x': nrm(ks[0], (BATCH, SEQ, D_MODEL), 1.0),
        'c': nrm(ks[1], (BATCH, D_MODEL), 1.0),
        'ctx': nrm(ks[2], (BATCH, CTX_LEN, D_MODEL), 1.0),
        'c_ctx': nrm(ks[3], (D_MODEL,), 1.0),
        'ada_w': nrm(ks[4], (DEPTH, D_MODEL, 6 * D_MODEL), 0.5 * D_MODEL ** -0.5),
        'ada_b': nrm(ks[5], (DEPTH, 6 * D_MODEL), 0.02),
        'norm_mix_g': gain(ks[6], (DEPTH, D_MODEL)),
        'norm_ffn_g': gain(ks[7], (DEPTH, D_MODEL)),
        'w_in': nrm(ks[8], (DEPTH, D_MODEL, D_IN), D_MODEL ** -0.5),
        'w_out': nrm(ks[9], (DEPTH, D_MIX, D_MODEL), D_MIX ** -0.5),
        'diff_qk_norm_g': gain(ks[10], (DEPTH, 2, DIFF_HEAD_DIM)),
        'diff_lambda': nrm(ks[11], (DEPTH, 4, DIFF_HEAD_DIM), 0.1),
        'diff_subln_g': gain(ks[12], (DEPTH, DIFF_V_DIM)),
        'hgrn_lb_logits': nrm(ks[13], (2, DEPTH, HGRN_WIDTH), 0.5),
        'hgrn_norm_g': gain(ks[14], (DEPTH, HGRN_V)),
        'swa_qk_norm_g': gain(ks[15], (DEPTH, 2, SWA_HEAD_DIM)),
        'swa_sink': nrm(ks[16], (DEPTH, SWA_Q_HEADS), 0.5),
        'ffn_w1': nrm(ks[17], (N_DENSE, D_MODEL, D_FF), D_MODEL ** -0.5),
        'ffn_w3': nrm(ks[18], (N_DENSE, D_MODEL, D_FF), D_MODEL ** -0.5),
        'ffn_w2': nrm(ks[19], (N_DENSE, D_FF, D_MODEL), D_FF ** -0.5),
        'moe_router': nrm(ks[20], (N_MOE, D_MODEL, N_EXPERTS), D_MODEL ** -0.5),
        'moe_w1': nrm(ks[21], (N_MOE, N_EXPERTS, D_MODEL, D_FF), D_MODEL ** -0.5),
        'moe_w3': nrm(ks[22], (N_MOE, N_EXPERTS, D_MODEL, D_FF), D_MODEL ** -0.5),
        'moe_w2': nrm(ks[23], (N_MOE, N_EXPERTS, D_FF, D_MODEL), D_FF ** -0.5),
    }


def reference(x, c, ctx, c_ctx, ada_w, ada_b, norm_mix_g, norm_ffn_g, w_in, w_out,
              diff_qk_norm_g, diff_lambda, diff_subln_g, hgrn_lb_logits, hgrn_norm_g,
              swa_qk_norm_g, swa_sink, ffn_w1, ffn_w3, ffn_w2, moe_router, moe_w1, moe_w3, moe_w2):
    row, col = grid_positions(x.shape[1])
    lb = jnp.cumsum(jax.nn.softmax(hgrn_lb_logits.astype(F32), axis=1), axis=1)
    lb = lb - lb[:, :1]
    silu_c = jax.nn.silu(c)
    silu_cc = jax.nn.silu(c_ctx)
    for layer in range(DEPTH):
        need_ctx = layer < DEPTH - 1
        m_l = [m[:, None, :] for m in jnp.split(silu_c @ ada_w[layer] + ada_b[layer], 6, axis=-1)]
        m_c = jnp.split(silu_cc @ ada_w[layer] + ada_b[layer], 6, axis=-1)
        h_l = rmsnorm(x, norm_mix_g[layer]) * (1 + m_l[1]) + m_l[0]
        h_c = rmsnorm(ctx, norm_mix_g[layer]) * (1 + m_c[1]) + m_c[0]
        p_l = split_projection(h_l, w_in[layer])
        p_c = split_projection(h_c, w_in[layer])
        lam_init = 0.8 - 0.6 * math.exp(-0.3 * layer)
        a_l, a_c = diff_attention_mixer(tuple(p_l[0:3]), tuple(p_c[0:3]), row, col, diff_qk_norm_g[layer],
                                        diff_lambda[layer], lam_init, diff_subln_g[layer], need_ctx)
        b_l, b_c = hgrn2_mixer(tuple(p_l[3:8]), tuple(p_c[3:8]), lb[0, layer], lb[1, layer],
                               hgrn_norm_g[layer], need_ctx)
        c_l, c_c = swa_mixer(tuple(p_l[8:11]), tuple(p_c[8:11]), row, col, swa_qk_norm_g[layer],
                             swa_sink[layer], need_ctx)
        x = x + m_l[2] * (jnp.concatenate([a_l, b_l, c_l], axis=-1) @ w_out[layer])
        if need_ctx:
            ctx = ctx + m_c[2] * (jnp.concatenate([a_c, b_c, c_c], axis=-1) @ w_out[layer])
        h2 = rmsnorm(x, norm_ffn_g[layer]) * (1 + m_l[4]) + m_l[3]
        x = x + m_l[5] * channel_mixer(h2, layer, ffn_w1, ffn_w3, ffn_w2, moe_router, moe_w1, moe_w3, moe_w2)
        if need_ctx:
            h2c = rmsnorm(ctx, norm_ffn_g[layer]) * (1 + m_c[4]) + m_c[3]
            ctx = ctx + m_c[5] * channel_mixer(h2c, layer, ffn_w1, ffn_w3, ffn_w2, moe_router, moe_w1, moe_w3, moe_w2)
    return x
```

```python
import functools
import math

import jax
import jax.numpy as jnp
from jax import lax
from jax.experimental import pallas as pl
from jax.experimental.pallas import tpu as pltpu

D_MODEL = 1024
GRID_W = 64
DIFF_HEADS = 4
DIFF_HEAD_DIM = 32
DIFF_V_DIM = 64
DIFF_WIDTH = 256
HGRN_HEADS = 4
HGRN_K = 128
HGRN_WIDTH = 512
SWA_Q_HEADS = 4
SWA_KV_HEADS = 2
SWA_GROUP = 2
SWA_HEAD_DIM = 64
SWA_WIDTH = 256
SWA_KV_WIDTH = 128
WINDOW = 128
D_FF = 2816
N_EXPERTS = 8
ROPE_BASE = 10000.0
NORM_EPS = 1e-6
D_IN = 3840

F32 = jnp.float32
BF16 = jnp.bfloat16

ROW_TILE = 256
HGRN_CHUNK = 64
NEG_BIG = -1e30
VMEM_LIMIT = 56 * 1024 * 1024


def _cparams(sem):
    return pltpu.CompilerParams(dimension_semantics=sem, vmem_limit_bytes=VMEM_LIMIT)


def _split_bf16(v):
    hi = v.astype(BF16)
    lo = (v - hi.astype(F32)).astype(BF16)
    return hi, lo


def _dot(a, b):
    return jnp.dot(a, b, preferred_element_type=F32)


def _dot_nt(a, b):
    return lax.dot_general(a, b, (((1,), (1,)), ((), ())), preferred_element_type=F32)


def _dot_tn(a, b):
    return lax.dot_general(a, b, (((0,), (0,)), ((), ())), preferred_element_type=F32)


def _group_mean_sq(y, gmat):
    hi, lo = _split_bf16(y * y)
    return _dot(hi, gmat) + _dot(lo, gmat)


def _silu(v):
    return v * (1.0 / (1.0 + jnp.exp(-v)))


def _ada_kernel(s_ref, w_ref, b_ref, o_ref):
    s = s_ref[...]
    s = _silu(s)
    s_hi, s_lo = _split_bf16(s)
    w_hi, w_lo = _split_bf16(w_ref[...])
    o_ref[...] = _dot(s_hi, w_hi) + _dot(s_lo, w_hi) + _dot(s_hi, w_lo) + b_ref[...]


def _ada_modulation(cond, ada_w, ada_b):
    depth = ada_w.shape[0]
    r = cond.shape[0]
    nblk = 6 * D_MODEL // 1024
    return pl.pallas_call(
        _ada_kernel,
        grid=(depth, nblk),
        in_specs=[
            pl.BlockSpec((r, D_MODEL), lambda l, n: (0, 0)),
            pl.BlockSpec((None, D_MODEL, 1024), lambda l, n: (l, 0, n)),
            pl.BlockSpec((None, 1, 1024), lambda l, n: (l, 0, n)),
        ],
        out_specs=pl.BlockSpec((None, r, 1024), lambda l, n: (l, 0, n)),
        out_shape=jax.ShapeDtypeStruct((depth, r, 6 * D_MODEL), F32),
        compiler_params=_cparams(("parallel", "parallel")),
        name="ada_modulation",
    )(cond, ada_w, ada_b.reshape(depth, 1, 6 * D_MODEL))


def _rope(y, cos, sa, sb, half):
    w = y.shape[-1]
    fwd = pltpu.roll(y, w - half, 1)
    bwd = pltpu.roll(y, half, 1)
    return y * cos + fwd * sa + bwd * sb


def _inproj_kernel(x_ref, mod_ref, g_ref, w_ref, rope_ref, gains_ref, lbt_ref, ga_ref, gc_ref,
                   qa_ref, ka_ref, va_ref, hq_ref, hv_ref, kf_ref, kb_ref, lff_ref, lfb_ref, hg_ref,
                   qc_ref, kc_ref, vc_ref):
    x = x_ref[...]
    shift = mod_ref[0:1, :]
    scale = mod_ref[1:2, :]
    ms = jnp.mean(x * x, axis=-1, keepdims=True)
    h = (x * lax.rsqrt(ms + NORM_EPS) * g_ref[...]) * (1.0 + scale) + shift
    hb = h.astype(BF16)

    def proj(a, b):
        return _dot(hb, w_ref[:, a:b])

    def qk_prep(y, gmat, gain, cos, sa, sb, half):
        msq = _group_mean_sq(y, gmat)
        y = y * lax.rsqrt(msq + NORM_EPS) * gain
        return _rope(y, cos, sa, sb, half)

    ga = ga_ref[...]
    gc = gc_ref[...]
    ra = [rope_ref[:, 256 * i:256 * (i + 1)] for i in range(6)]
    qa = qk_prep(proj(0, 256), ga, gains_ref[0:1, :], ra[0], ra[1], ra[2], 8)
    qa_ref[...] = qa.astype(BF16)
    ka = qk_prep(proj(256, 512), ga, gains_ref[1:2, :], ra[0], ra[1], ra[2], 8)
    ka_ref[...] = ka.astype(BF16)
    va_ref[...] = proj(512, 768).astype(BF16)
    hq_ref[...] = _silu(proj(768, 1280)).astype(BF16)
    hv_ref[...] = proj(1280, 1792).astype(BF16)
    for d, (k_ref, lf_ref) in enumerate(((kf_ref, lff_ref), (kb_ref, lfb_ref))):
        z = proj(1792 + 512 * d, 2304 + 512 * d)
        log_lb = lbt_ref[0:1, 512 * d:512 * (d + 1)]
        log1m_lb = lbt_ref[1:2, 512 * d:512 * (d + 1)]
        one_m_lb = lbt_ref[2:3, 512 * d:512 * (d + 1)]
        sp = jnp.maximum(-z, 0.0) + jnp.log(1.0 + jnp.exp(-jnp.abs(z)))
        b2 = log1m_lb - sp
        mx = jnp.maximum(log_lb, b2)
        lf_ref[...] = mx + jnp.log(1.0 + jnp.exp(-jnp.abs(log_lb - b2)))
        k_ref[...] = (one_m_lb * jnp.exp(-(sp + z))).astype(BF16)
    hg_ref[...] = _silu(proj(2816, 3328)).astype(BF16)
    qc = qk_prep(proj(3328, 3584), gc, gains_ref[2:3, :], ra[3], ra[4], ra[5], 16)
    qc_ref[...] = qc.astype(BF16)
    kc = qk_prep(proj(3584, 3712), gc[0:128, 0:128], gains_ref[3:4, 0:128],
                 ra[3][:, 0:128], ra[4][:, 0:128], ra[5][:, 0:128], 16)
    kc_ref[...] = kc.astype(BF16)
    vc_ref[...] = proj(3712, 3840).astype(BF16)


def _in_projection(xc, mod, g, w, rope, gains, lbt, ga, gc, tiles_per_seq):
    n = xc.shape[0]
    nt = n // ROW_TILE
    tps = tiles_per_seq

    def row(i):
        return (i, 0)

    def mod_idx(i):
        return (i, 0, 0)

    const = lambda i: (0, 0)
    widths = [(256, BF16), (256, BF16), (256, BF16), (512, BF16), (512, BF16), (512, BF16), (512, BF16),
              (512, F32), (512, F32), (512, BF16), (256, BF16), (128, BF16), (128, BF16)]
    return pl.pallas_call(
        _inproj_kernel,
        grid=(nt,),
        in_specs=[
            pl.BlockSpec((ROW_TILE, D_MODEL), row),
            pl.BlockSpec((None, 2, D_MODEL), mod_idx),
            pl.BlockSpec((1, D_MODEL), const),
            pl.BlockSpec((D_MODEL, D_IN), const),
            pl.BlockSpec((ROW_TILE, 6 * 256), lambda i: (i % tps, 0)),
            pl.BlockSpec((4, 256), const),
            pl.BlockSpec((3, 1024), const),
            pl.BlockSpec((256, 256), const),
            pl.BlockSpec((256, 256), const),
        ],
        out_specs=[pl.BlockSpec((ROW_TILE, wd), row) for wd, _ in widths],
        out_shape=[jax.ShapeDtypeStruct((n, wd), dt) for wd, dt in widths],
        compiler_params=_cparams(("parallel",)),
        name="in_projection",
    )(xc, mod, g, w, rope, gains, lbt, ga, gc)


def _diff_attend(q, k, v, lam):
    lane = lax.broadcasted_iota(jnp.int32, q.shape, 1)
    out = jnp.zeros(q.shape, F32)
    for hd in range(DIFF_HEADS):
        probs = []
        for m in range(2):
            g = 2 * hd + m
            qg = jnp.where(lane // DIFF_HEAD_DIM == g, q, jnp.zeros_like(q))
            s = _dot_nt(qg, k)
            mx = jnp.max(s, axis=-1, keepdims=True)
            e = jnp.exp(s - mx)
            probs.append(e * (1.0 / jnp.sum(e, axis=-1, keepdims=True)))
        a = (probs[0] - lam * probs[1]).astype(BF16)
        o = _dot(a, v)
        out = out + jnp.where(lane // DIFF_V_DIM == hd, o, 0.0)
    return out


def _diff_kernel(q_ref, k_ref, v_ref, lam_ref, post_ref, g64_ref, o_ref, *, n_ctx):
    j = pl.program_id(1)
    lam = lam_ref[0:1, 0:1]

    def finish(o):
        msq = _group_mean_sq(o, g64_ref[...])
        o_ref[...] = (o * lax.rsqrt(msq + NORM_EPS) * post_ref[...]).astype(BF16)

    @pl.when(j == 0)
    def _():
        finish(_diff_attend(q_ref[...], k_ref[0:n_ctx, :], v_ref[0:n_ctx, :], lam))

    @pl.when(j > 0)
    def _():
        finish(_diff_attend(q_ref[...], k_ref[...], v_ref[...], lam))


def _diff_attention(qa, ka, va, lam_row, post_row, g64, n_batch, t_len, n_ctx):
    tps = t_len // ROW_TILE
    q3 = qa.reshape(n_batch, t_len, DIFF_WIDTH)
    k3 = ka.reshape(n_batch, t_len, DIFF_WIDTH)
    v3 = va.reshape(n_batch, t_len, DIFF_WIDTH)
    const = lambda b, j: (0, 0)
    out = pl.pallas_call(
        functools.partial(_diff_kernel, n_ctx=n_ctx),
        grid=(n_batch, tps),
        in_specs=[
            pl.BlockSpec((None, ROW_TILE, DIFF_WIDTH), lambda b, j: (b, j, 0)),
            pl.BlockSpec((None, t_len, DIFF_WIDTH), lambda b, j: (b, 0, 0)),
            pl.BlockSpec((None, t_len, DIFF_WIDTH), lambda b, j: (b, 0, 0)),
            pl.BlockSpec((1, 128), const),
            pl.BlockSpec((1, DIFF_WIDTH), const),
            pl.BlockSpec((256, 256), const),
        ],
        out_specs=pl.BlockSpec((None, ROW_TILE, DIFF_WIDTH), lambda b, j: (b, j, 0)),
        out_shape=jax.ShapeDtypeStruct((n_batch, t_len, DIFF_WIDTH), BF16),
        compiler_params=_cparams(("parallel", "arbitrary")),
        name="diff_attention",
    )(q3, k3, v3, lam_row, post_row, g64)
    return out.reshape(n_batch * t_len, DIFF_WIDTH)


def _swa_kernel(q_ref, k_ref, v_ref, sink_ref, o_ref, *, n_ctx, n_lat):
    j = pl.program_id(1)
    q = q_ref[...]
    tq = q.shape[0]
    kc = k_ref[0:n_ctx, :]
    vc = v_ref[0:n_ctx, :]
    lane = lax.broadcasted_iota(jnp.int32, (tq, SWA_KV_WIDTH), 1)
    span = ROW_TILE + 2 * WINDOW

    def run(kw, vw, valid):
        for g in range(SWA_GROUP):
            qg = q[:, SWA_KV_WIDTH * g:SWA_KV_WIDTH * (g + 1)]
            out = jnp.zeros((tq, SWA_KV_WIDTH), F32)
            for hd in range(SWA_KV_HEADS):
                sink = sink_ref[SWA_GROUP * hd + g:SWA_GROUP * hd + g + 1, 0:1]
                qh = jnp.where(lane // SWA_HEAD_DIM == hd, qg, jnp.zeros_like(qg))
                s_c = _dot_nt(qh, kc)
                mx = jnp.maximum(jnp.max(s_c, axis=-1, keepdims=True), sink)
                if kw is not None:
                    s_w = jnp.where(valid, _dot_nt(qh, kw), NEG_BIG)
                    mx = jnp.maximum(mx, jnp.max(s_w, axis=-1, keepdims=True))
                e_c = jnp.exp(s_c - mx)
                den = jnp.sum(e_c, axis=-1, keepdims=True) + jnp.exp(sink - mx)
                o = _dot(e_c.astype(BF16), vc)
                if kw is not None:
                    e_w = jnp.exp(s_w - mx)
                    den = den + jnp.sum(e_w, axis=-1, keepdims=True)
                    o = o + _dot(e_w.astype(BF16), vw)
                out = out + jnp.where(lane // SWA_HEAD_DIM == hd, o * (1.0 / den), 0.0)
            o_ref[:, SWA_KV_WIDTH * g:SWA_KV_WIDTH * (g + 1)] = out.astype(BF16)

    @pl.when(j == 0)
    def _():
        run(None, None, None)

    @pl.when(j > 0)
    def _():
        q0 = (j - 1) * ROW_TILE
        ks = jnp.clip(q0 - WINDOW, 0, n_lat - span)
        ks = pl.multiple_of(ks, WINDOW)
        kw = k_ref[pl.ds(n_ctx + ks, span), :]
        vw = v_ref[pl.ds(n_ctx + ks, span), :]
        qpos = q0 + lax.broadcasted_iota(jnp.int32, (tq, span), 0)
        kpos = ks + lax.broadcasted_iota(jnp.int32, (tq, span), 1)
        run(kw, vw, jnp.abs(qpos - kpos) <= WINDOW)


def _swa_attention(qc, kc, vc, sink_tab, n_batch, t_len, n_ctx):
    tps = t_len // ROW_TILE
    q3 = qc.reshape(n_batch, t_len, SWA_WIDTH)
    k3 = kc.reshape(n_batch, t_len, SWA_KV_WIDTH)
    v3 = vc.reshape(n_batch, t_len, SWA_KV_WIDTH)
    out = pl.pallas_call(
        functools.partial(_swa_kernel, n_ctx=n_ctx, n_lat=t_len - n_ctx),
        grid=(n_batch, tps),
        in_specs=[
            pl.BlockSpec((None, ROW_TILE, SWA_WIDTH), lambda b, j: (b, j, 0)),
            pl.BlockSpec((None, t_len, SWA_KV_WIDTH), lambda b, j: (b, 0, 0)),
            pl.BlockSpec((None, t_len, SWA_KV_WIDTH), lambda b, j: (b, 0, 0)),
            pl.BlockSpec((SWA_Q_HEADS, 128), lambda b, j: (0, 0)),
        ],
        out_specs=pl.BlockSpec((None, ROW_TILE, SWA_WIDTH), lambda b, j: (b, j, 0)),
        out_shape=jax.ShapeDtypeStruct((n_batch, t_len, SWA_WIDTH), BF16),
        compiler_params=_cparams(("parallel", "arbitrary")),
        name="swa_attention",
    )(q3, k3, v3, sink_tab)
    return out.reshape(n_batch * t_len, SWA_WIDTH)


def _hgrn_kernel(q_ref, v_ref, kf_ref, kb_ref, lff_ref, lfb_ref, g_ref, ng_ref, o_ref,
                 of_ref, ob_ref, *, n_ctx, t_len):
    c = HGRN_CHUNK
    n_chunks = t_len // c
    n_ctx_chunks = n_ctx // c
    mid = c // 2
    ri = lax.broadcasted_iota(jnp.int32, (c, c), 0)
    ci = lax.broadcasted_iota(jnp.int32, (c, c), 1)
    causal = ri >= ci
    tri_f = jnp.where(causal, 1.0, 0.0).astype(BF16)
    tri_b = jnp.where(ri <= ci, 1.0, 0.0).astype(BF16)

    def direction(q, k, v, lf, st, tri, mask, ref_row, end_row):
        lf_hi, lf_lo = _split_bf16(lf)
        cum = _dot(tri, lf_hi) + _dot(tri, lf_lo)
        ref = cum[ref_row:ref_row + 1, :]
        tot = cum[end_row:end_row + 1, :]
        e_q = jnp.exp(cum - ref)
        e_k = jnp.exp(ref - cum)
        qt = q * e_q
        kt = k * e_k
        sc = jnp.where(mask, _dot_nt(qt.astype(BF16), kt.astype(BF16)), 0.0)
        o = _dot(sc.astype(BF16), v) + _dot_nt((qt * jnp.exp(ref)).astype(BF16), st.astype(BF16))
        kd = (kt * jnp.exp(tot - ref)).astype(BF16)
        st_new = st * jnp.exp(tot) + _dot_tn(v, kd)
        return o, st_new

    def body(i, carry):
        sf, sb = carry
        rf = pl.multiple_of(i * c, c)
        cb = jnp.where(i < n_ctx_chunks, n_ctx_chunks - 1 - i, n_chunks - 1 + n_ctx_chunks - i)
        rb = pl.multiple_of(cb * c, c)
        qf = q_ref[pl.ds(rf, c), :].astype(F32)
        o_f, sf = direction(qf, kf_ref[pl.ds(rf, c), :].astype(F32), v_ref[pl.ds(rf, c), :],
                            lff_ref[pl.ds(rf, c), :], sf, tri_f, causal, mid - 1, c - 1)
        of_ref[pl.ds(rf, c), :] = o_f
        qb = q_ref[pl.ds(rb, c), :].astype(F32)
        o_b, sb = direction(qb, kb_ref[pl.ds(rb, c), :].astype(F32), v_ref[pl.ds(rb, c), :],
                            lfb_ref[pl.ds(rb, c), :], sb, tri_b, ri <= ci, mid, 0)
        ob_ref[pl.ds(rb, c), :] = o_b
        return sf, sb

    zero = jnp.zeros((HGRN_K, HGRN_K), F32)
    lax.fori_loop(0, n_chunks, body, (zero, zero))

    def fin(t, _):
        r = pl.multiple_of(t * ROW_TILE, ROW_TILE)
        o = of_ref[pl.ds(r, ROW_TILE), :] + ob_ref[pl.ds(r, ROW_TILE), :]
        ms = jnp.mean(o * o, axis=-1, keepdims=True)
        o = o * lax.rsqrt(ms + NORM_EPS) * ng_ref[...]
        o_ref[pl.ds(r, ROW_TILE), :] = (o * g_ref[pl.ds(r, ROW_TILE), :].astype(F32)).astype(BF16)
        return 0

    lax.fori_loop(0, t_len // ROW_TILE, fin, 0)


def _hgrn(hq, hv, kf, kb, lff, lfb, hg, norm_g, n_batch, t_len, n_ctx):
    def view(a):
        return a.reshape(n_batch, t_len, HGRN_WIDTH)

    blk = pl.BlockSpec((None, t_len, HGRN_K), lambda b, h: (b, 0, h))
    out = pl.pallas_call(
        functools.partial(_hgrn_kernel, n_ctx=n_ctx, t_len=t_len),
        grid=(n_batch, HGRN_HEADS),
        in_specs=[blk] * 7 + [pl.BlockSpec((1, HGRN_K), lambda b, h: (0, 0))],
        out_specs=blk,
        out_shape=jax.ShapeDtypeStruct((n_batch, t_len, HGRN_WIDTH), BF16),
        scratch_shapes=[pltpu.VMEM((t_len, HGRN_K), F32), pltpu.VMEM((t_len, HGRN_K), F32)],
        compiler_params=_cparams(("parallel", "parallel")),
        name="hgrn2",
    )(view(hq), view(hv), view(kf), view(kb), view(lff), view(lfb), view(hg), norm_g)
    return out.reshape(n_batch * t_len, HGRN_WIDTH)


def _outproj_kernel(x_ref, a_ref, b_ref, c_ref, w_ref, mod_ref, g_ref, *rest, with_router):
    if with_router:
        r_ref, xo_ref, h_ref, gate_ref = rest
    else:
        xo_ref, h_ref = rest
    y = (_dot(a_ref[...], w_ref[0:256, :]) + _dot(b_ref[...], w_ref[256:768, :])
         + _dot(c_ref[...], w_ref[768:1024, :]))
    x = x_ref[...] + mod_ref[0:1, :] * y
    xo_ref[...] = x
    ms = jnp.mean(x * x, axis=-1, keepdims=True)
    h = (x * lax.rsqrt(ms + NORM_EPS) * g_ref[...]) * (1.0 + mod_ref[2:3, :]) + mod_ref[1:2, :]
    h_ref[...] = h.astype(BF16)
    if with_router:
        h_hi, h_lo = _split_bf16(h)
        r_hi, r_lo = _split_bf16(r_ref[...])
        logits = _dot(h_hi, r_hi) + _dot(h_lo, r_hi) + _dot(h_hi, r_lo)
        lane = lax.broadcasted_iota(jnp.int32, logits.shape, 1).astype(F32)
        logits = jnp.where(lane < N_EXPERTS, logits, NEG_BIG)
        v1 = jnp.max(logits, axis=-1, keepdims=True)
        i1 = jnp.min(jnp.where(logits == v1, lane, 128.0), axis=-1, keepdims=True)
        rest_l = jnp.where(lane == i1, NEG_BIG, logits)
        v2 = jnp.max(rest_l, axis=-1, keepdims=True)
        i2 = jnp.min(jnp.where(rest_l == v2, lane, 128.0), axis=-1, keepdims=True)
        e2 = jnp.exp(v2 - v1)
        w1 = 1.0 / (1.0 + e2)
        w2 = e2 * w1
        gate_ref[...] = jnp.where(lane == i1, w1, 0.0) + jnp.where(lane == i2, w2, 0.0)


def _out_projection(xc, a, b, c, w, mod, g, router):
    n = xc.shape[0]
    nt = n // ROW_TILE
    row = lambda i: (i, 0)
    const = lambda i: (0, 0)
    mod_idx = lambda i: (i, 0, 0)
    with_router = router is not None
    in_specs = [
        pl.BlockSpec((ROW_TILE, D_MODEL), row),
        pl.BlockSpec((ROW_TILE, DIFF_WIDTH), row),
        pl.BlockSpec((ROW_TILE, HGRN_WIDTH), row),
        pl.BlockSpec((ROW_TILE, SWA_WIDTH), row),
        pl.BlockSpec((D_MODEL, D_MODEL), const),
        pl.BlockSpec((None, 3, D_MODEL), mod_idx),
        pl.BlockSpec((1, D_MODEL), const),
    ]
    args = [xc, a, b, c, w, mod, g]
    out_specs = [pl.BlockSpec((ROW_TILE, D_MODEL), row), pl.BlockSpec((ROW_TILE, D_MODEL), row)]
    out_shape = [jax.ShapeDtypeStruct((n, D_MODEL), F32), jax.ShapeDtypeStruct((n, D_MODEL), BF16)]
    if with_router:
        in_specs.append(pl.BlockSpec((D_MODEL, 128), const))
        args.append(router)
        out_specs.append(pl.BlockSpec((ROW_TILE, 128), row))
        out_shape.append(jax.ShapeDtypeStruct((n, 128), F32))
    return pl.pallas_call(
        functools.partial(_outproj_kernel, with_router=with_router),
        grid=(nt,),
        in_specs=in_specs,
        out_specs=out_specs,
        out_shape=out_shape,
        compiler_params=_cparams(("parallel",)),
        name="out_projection",
    )(*args)


def _ffn_kernel(x_ref, h_ref, w1_ref, w3_ref, w2_ref, mod_ref, o_ref):
    h = h_ref[...]
    u = _dot(h, w1_ref[...])
    act = (_silu(u) * _dot(h, w3_ref[...])).astype(BF16)
    y = _dot(act, w2_ref[...])
    o_ref[...] = x_ref[...] + mod_ref[...] * y


def _dense_ffn(x, h, w1, w3, w2, mod):
    n = x.shape[0]
    nt = n // ROW_TILE
    row = lambda i: (i, 0)
    const = lambda i: (0, 0)
    mod_idx = lambda i: (i, 0, 0)
    return pl.pallas_call(
        _ffn_kernel,
        grid=(nt,),
        in_specs=[
            pl.BlockSpec((ROW_TILE, D_MODEL), row),
            pl.BlockSpec((ROW_TILE, D_MODEL), row),
            pl.BlockSpec((D_MODEL, D_FF), const),
            pl.BlockSpec((D_MODEL, D_FF), const),
            pl.BlockSpec((D_FF, D_MODEL), const),
            pl.BlockSpec((None, 1, D_MODEL), mod_idx),
        ],
        out_specs=pl.BlockSpec((ROW_TILE, D_MODEL), row),
        out_shape=jax.ShapeDtypeStruct((n, D_MODEL), F32),
        compiler_params=_cparams(("parallel",)),
        name="dense_ffn",
    )(x, h, w1, w3, w2, mod)


MOE_F_CHUNK = 1408
MOE_ROW_TILE = 2 * ROW_TILE


def _moe_kernel(x_ref, h_ref, gate_ref, w1_ref, w3_ref, w2_ref, mod_ref, o_ref, acc_ref):
    e = pl.program_id(1)
    f = pl.program_id(2)

    @pl.when((e == 0) & (f == 0))
    def _():
        acc_ref[...] = jnp.zeros_like(acc_ref)

    h = h_ref[...]
    u = _dot(h, w1_ref[...])
    act = (_silu(u) * _dot(h, w3_ref[...])).astype(BF16)
    y = _dot(act, w2_ref[...])
    gates = gate_ref[...]
    lane = lax.broadcasted_iota(jnp.int32, gates.shape, 1)
    gate = jnp.sum(jnp.where(lane == e, gates, 0.0), axis=-1, keepdims=True)
    acc_ref[...] += gate * y

    @pl.when((e == pl.num_programs(1) - 1) & (f == pl.num_programs(2) - 1))
    def _():
        for s in range(mod_ref.shape[0]):
            rows = slice(ROW_TILE * s, ROW_TILE * (s + 1))
            o_ref[rows, :] = x_ref[rows, :] + mod_ref[s] * acc_ref[rows, :]


def _moe_ffn(x, h, gates, w1, w3, w2, mod, tm):
    n = x.shape[0]
    nt = n // tm
    nf = D_FF // MOE_F_CHUNK
    row = lambda i, e, f: (i, 0)
    mod_idx = lambda i, e, f: (i, 0, 0)
    return pl.pallas_call(
        _moe_kernel,
        grid=(nt, N_EXPERTS, nf),
        in_specs=[
            pl.BlockSpec((tm, D_MODEL), row),
            pl.BlockSpec((tm, D_MODEL), row),
            pl.BlockSpec((tm, 128), row),
            pl.BlockSpec((None, D_MODEL, MOE_F_CHUNK), lambda i, e, f: (e, 0, f)),
            pl.BlockSpec((None, D_MODEL, MOE_F_CHUNK), lambda i, e, f: (e, 0, f)),
            pl.BlockSpec((None, MOE_F_CHUNK, D_MODEL), lambda i, e, f: (e, f, 0)),
            pl.BlockSpec((tm // ROW_TILE, 1, D_MODEL), mod_idx),
        ],
        out_specs=pl.BlockSpec((tm, D_MODEL), row),
        out_shape=jax.ShapeDtypeStruct((n, D_MODEL), F32),
        scratch_shapes=[pltpu.VMEM((tm, D_MODEL), F32)],
        compiler_params=_cparams(("parallel", "arbitrary", "arbitrary")),
        name="moe_ffn",
    )(x, h, gates, w1, w3, w2, mod)


def _rope_tables(n_ctx, n_lat):
    pos_r = jnp.arange(n_lat, dtype=jnp.int32) // GRID_W
    pos_c = jnp.arange(n_lat, dtype=jnp.int32) % GRID_W

    def per_head(head_dim):
        nf = head_dim // 4
        inv = ROPE_BASE ** (-jnp.arange(nf, dtype=F32) / nf)
        ang_r = pos_r.astype(F32)[:, None] * inv[None, :]
        ang_c = pos_c.astype(F32)[:, None] * inv[None, :]
        z = jnp.zeros_like(ang_r)
        cos = jnp.concatenate([jnp.cos(ang_r)] * 2 + [jnp.cos(ang_c)] * 2, axis=-1)
        sa = jnp.concatenate([-jnp.sin(ang_r), z, -jnp.sin(ang_c), z], axis=-1)
        sb = jnp.concatenate([z, jnp.sin(ang_r), z, jnp.sin(ang_c)], axis=-1)
        reps = 256 // head_dim
        tabs = [jnp.tile(t, (1, reps)) for t in (cos, sa, sb)]
        ctx = [jnp.ones((n_ctx, 256), F32), jnp.zeros((n_ctx, 256), F32), jnp.zeros((n_ctx, 256), F32)]
        return [jnp.concatenate([c, t], axis=0) for c, t in zip(ctx, tabs)]

    return jnp.concatenate(per_head(DIFF_HEAD_DIM) + per_head(SWA_HEAD_DIM), axis=-1)


def _block_diag_mean(group):
    idx = jnp.arange(256) // group
    return jnp.where(idx[:, None] == idx[None, :], 1.0 / group, 0.0).astype(BF16)


def kernel(x, c, ctx, c_ctx, ada_w, ada_b, norm_mix_g, norm_ffn_g, w_in, w_out, diff_qk_norm_g, diff_lambda,
           diff_subln_g, hgrn_lb_logits, hgrn_norm_g, swa_qk_norm_g, swa_sink, ffn_w1, ffn_w3, ffn_w2,
           moe_router, moe_w1, moe_w3, moe_w2):
    n_batch, n_lat, _ = x.shape
    n_ctx = ctx.shape[1]
    depth = ada_w.shape[0]
    t_len = n_ctx + n_lat
    tps = t_len // ROW_TILE
    assert n_ctx == ROW_TILE and n_lat % ROW_TILE == 0 and n_lat >= ROW_TILE + 2 * WINDOW
    assert (n_batch * t_len) % MOE_ROW_TILE == 0

    xc = jnp.concatenate([ctx, x], axis=1).reshape(n_batch * t_len, D_MODEL)

    n_rows = -(-(n_batch + 1) // 8) * 8
    cond = jnp.concatenate([c, c_ctx[None, :], jnp.zeros((n_rows - n_batch - 1, D_MODEL), F32)], axis=0)
    mods = _ada_modulation(cond, ada_w, ada_b).reshape(depth, n_rows, 6, D_MODEL)
    m_lat = jnp.broadcast_to(mods[:, :n_batch, None], (depth, n_batch, tps - 1, 6, D_MODEL))
    m_ctx = jnp.broadcast_to(mods[:, n_batch, None, None], (depth, n_batch, 1, 6, D_MODEL))
    mods = jnp.concatenate([m_ctx, m_lat], axis=2).reshape(depth, n_batch * tps, 6, D_MODEL)

    lb = jnp.cumsum(jax.nn.softmax(hgrn_lb_logits.astype(F32), axis=1), axis=1)
    lb = lb - lb[:, :1]
    rope = _rope_tables(n_ctx, n_lat)
    g32 = _block_diag_mean(DIFF_HEAD_DIM)
    g64 = _block_diag_mean(SWA_HEAD_DIM)

    perm_q = jnp.arange(SWA_WIDTH).reshape(SWA_KV_HEADS, SWA_GROUP, SWA_HEAD_DIM).transpose(1, 0, 2).reshape(-1)
    qc0 = 3 * DIFF_WIDTH + 5 * HGRN_WIDTH
    col_perm = jnp.concatenate([jnp.arange(qc0), qc0 + perm_q, jnp.arange(qc0 + SWA_WIDTH, D_IN)])
    oc0 = DIFF_WIDTH + HGRN_WIDTH
    row_perm = jnp.concatenate([jnp.arange(oc0), oc0 + perm_q])

    for layer in range(depth):
        lam_init = 0.8 - 0.6 * math.exp(-0.3 * layer)
        mod = mods[layer]
        w_in_l = w_in[layer][:, col_perm].astype(BF16)
        w_out_l = w_out[layer][row_perm, :].astype(BF16)
        gains = jnp.stack([
            jnp.tile(diff_qk_norm_g[layer, 0], 8) * (DIFF_HEAD_DIM ** -0.5),
            jnp.tile(diff_qk_norm_g[layer, 1], 8),
            jnp.tile(swa_qk_norm_g[layer, 0], 4) * (SWA_HEAD_DIM ** -0.5),
            jnp.tile(swa_qk_norm_g[layer, 1], 4),
        ]).astype(F32)
        lbt = jnp.stack([
            jnp.log(lb[:, layer]).reshape(-1),
            jnp.log1p(-lb[:, layer]).reshape(-1),
            (1.0 - lb[:, layer]).reshape(-1),
        ]).astype(F32)
        lv = diff_lambda[layer].astype(F32)
        lam = jnp.exp(jnp.sum(lv[0] * lv[1])) - jnp.exp(jnp.sum(lv[2] * lv[3])) + lam_init
        lam_row = jnp.full((1, 128), lam, F32)
        post_row = (jnp.tile(diff_subln_g[layer], DIFF_HEADS) * (1.0 - lam_init)).reshape(1, DIFF_WIDTH).astype(F32)
        sink_tab = jnp.broadcast_to(swa_sink[layer].astype(F32)[:, None], (SWA_Q_HEADS, 128))

        (qa, ka, va, hq, hv, kf, kb, lff, lfb, hg, qc, kc, vc) = _in_projection(
            xc, mod[:, 0:2], norm_mix_g[layer].reshape(1, D_MODEL), w_in_l, rope, gains, lbt, g32, g64, tps)
        a = _diff_attention(qa, ka, va, lam_row, post_row, g64, n_batch, t_len, n_ctx)
        b = _hgrn(hq, hv, kf, kb, lff, lfb, hg, hgrn_norm_g[layer].reshape(1, HGRN_K), n_batch, t_len, n_ctx)
        cc = _swa_attention(qc, kc, vc, sink_tab, n_batch, t_len, n_ctx)

        is_moe = layer % 2 == 1
        jj = layer // 2
        router = None
        if is_moe:
            router = jnp.pad(moe_router[jj].astype(F32), ((0, 0), (0, 128 - N_EXPERTS)))
        outs = _out_projection(xc, a, b, cc, w_out_l, mod[:, 2:5], norm_ffn_g[layer].reshape(1, D_MODEL),
                               router)
        if is_moe:
            x_mid, h2, gates = outs
            xc = _moe_ffn(x_mid, h2, gates, moe_w1[jj].astype(BF16), moe_w3[jj].astype(BF16),
                          moe_w2[jj].astype(BF16), mod[:, 5:6], MOE_ROW_TILE)
        else:
            x_mid, h2 = outs
            xc = _dense_ffn(x_mid, h2, ffn_w1[jj].astype(BF16), ffn_w3[jj].astype(BF16),
                            ffn_w2[jj].astype(BF16), mod[:, 5:6])

    return xc.reshape(n_batch, t_len, D_MODEL)[:, n_ctx:, :]
```

```python
import functools
import math

import jax
import jax.numpy as jnp
from jax import lax
from jax.experimental import pallas as pl
from jax.experimental.pallas import tpu as pltpu

D_MODEL = 1024
GRID_W = 64
DIFF_HEADS = 4
DIFF_HEAD_DIM = 32
DIFF_V_DIM = 64
DIFF_WIDTH = 256
HGRN_HEADS = 4
HGRN_K = 128
HGRN_WIDTH = 512
SWA_Q_HEADS = 4
SWA_KV_HEADS = 2
SWA_GROUP = 2
SWA_HEAD_DIM = 64
SWA_WIDTH = 256
SWA_KV_WIDTH = 128
WINDOW = 128
D_FF = 2816
N_EXPERTS = 8
ROPE_BASE = 10000.0
NORM_EPS = 1e-6
D_IN = 3840

F32 = jnp.float32
BF16 = jnp.bfloat16

ROW_TILE = 256
HGRN_CHUNK = 64
NEG_BIG = -1e30
LOG2_E = 1.4426950408889634
VMEM_LIMIT = 56 * 1024 * 1024


def _cparams(sem):
    return pltpu.CompilerParams(dimension_semantics=sem, vmem_limit_bytes=VMEM_LIMIT)


def _split_bf16(v):
    hi = v.astype(BF16)
    lo = (v - hi.astype(F32)).astype(BF16)
    return hi, lo


def _dot(a, b):
    return jnp.dot(a, b, preferred_element_type=F32)


def _dot_nt(a, b):
    return lax.dot_general(a, b, (((1,), (1,)), ((), ())), preferred_element_type=F32)


def _dot_tn(a, b):
    return lax.dot_general(a, b, (((0,), (0,)), ((), ())), preferred_element_type=F32)


def _group_mean_sq(y, gmat):
    hi, lo = _split_bf16(y * y)
    return _dot(hi, gmat) + _dot(lo, gmat)


def _silu(v):
    return v * (1.0 / (1.0 + jnp.exp(-v)))


def _ada_kernel(s_ref, w_ref, b_ref, o_ref):
    s = s_ref[...]
    s = _silu(s)
    s_hi, s_lo = _split_bf16(s)
    w_hi, w_lo = _split_bf16(w_ref[...])
    o_ref[...] = _dot(s_hi, w_hi) + _dot(s_lo, w_hi) + _dot(s_hi, w_lo) + b_ref[...]


def _ada_modulation(cond, ada_w, ada_b):
    depth = ada_w.shape[0]
    r = cond.shape[0]
    nblk = 6 * D_MODEL // 1024
    return pl.pallas_call(
        _ada_kernel,
        grid=(depth, nblk),
        in_specs=[
            pl.BlockSpec((r, D_MODEL), lambda l, n: (0, 0)),
            pl.BlockSpec((None, D_MODEL, 1024), lambda l, n: (l, 0, n)),
            pl.BlockSpec((None, 1, 1024), lambda l, n: (l, 0, n)),
        ],
        out_specs=pl.BlockSpec((None, r, 1024), lambda l, n: (l, 0, n)),
        out_shape=jax.ShapeDtypeStruct((depth, r, 6 * D_MODEL), F32),
        compiler_params=_cparams(("parallel", "parallel")),
        name="ada_modulation",
    )(cond, ada_w, ada_b.reshape(depth, 1, 6 * D_MODEL))


def _rope(y, cos, sa, sb, half):
    w = y.shape[-1]
    fwd = pltpu.roll(y, w - half, 1)
    bwd = pltpu.roll(y, half, 1)
    return y * cos + fwd * sa + bwd * sb


def _inproj_kernel(x_ref, mod_ref, g_ref, w_ref, rope_ref, gains_ref, lbt_ref, ga_ref, gc_ref,
                   qa_ref, ka_ref, va_ref, hq_ref, hv_ref, lff_ref, lfb_ref, hg_ref,
                   qc_ref, kc_ref, vc_ref):
    x = x_ref[...]
    shift = mod_ref[0:1, :]
    scale = mod_ref[1:2, :]
    ms = jnp.mean(x * x, axis=-1, keepdims=True)
    h = (x * lax.rsqrt(ms + NORM_EPS) * g_ref[...]) * (1.0 + scale) + shift
    hb = h.astype(BF16)

    def proj(a, b):
        return _dot(hb, w_ref[:, a:b])

    def qk_prep(y, gmat, gain, cos, sa, sb, half):
        msq = _group_mean_sq(y, gmat)
        y = y * lax.rsqrt(msq + NORM_EPS) * gain
        return _rope(y, cos, sa, sb, half)

    ga = ga_ref[...]
    gc = gc_ref[...]
    ra = [rope_ref[:, 256 * i:256 * (i + 1)] for i in range(6)]
    qa = qk_prep(proj(0, 256), ga, gains_ref[0:1, :], ra[0], ra[1], ra[2], 8)
    qa_ref[...] = qa.astype(BF16)
    ka = qk_prep(proj(256, 512), ga, gains_ref[1:2, :], ra[0], ra[1], ra[2], 8)
    ka_ref[...] = ka.astype(BF16)
    va_ref[...] = proj(512, 768).astype(BF16)
    hq_ref[...] = _silu(proj(768, 1280)).astype(BF16)
    hv_ref[...] = proj(1280, 1792).astype(BF16)
    for d, lf_ref in enumerate((lff_ref, lfb_ref)):
        z = proj(1792 + 512 * d, 2304 + 512 * d)
        log_lb = lbt_ref[0:1, 512 * d:512 * (d + 1)]
        log1m_lb = lbt_ref[1:2, 512 * d:512 * (d + 1)]
        sp = jnp.maximum(-z, 0.0) + jnp.log(1.0 + jnp.exp(-jnp.abs(z)))
        b2 = log1m_lb - sp
        mx = jnp.maximum(log_lb, b2)
        lf_ref[...] = mx + jnp.log(1.0 + jnp.exp(-jnp.abs(log_lb - b2)))
    hg_ref[...] = _silu(proj(2816, 3328)).astype(BF16)
    qc = qk_prep(proj(3328, 3584), gc, gains_ref[2:3, :], ra[3], ra[4], ra[5], 16)
    qc_ref[...] = qc.astype(BF16)
    kc = qk_prep(proj(3584, 3712), gc[0:128, 0:128], gains_ref[3:4, 0:128],
                 ra[3][:, 0:128], ra[4][:, 0:128], ra[5][:, 0:128], 16)
    kc_ref[...] = kc.astype(BF16)
    vc_ref[...] = proj(3712, 3840).astype(BF16)


def _in_projection(xc, mod, g, w, rope, gains, lbt, ga, gc, tiles_per_seq):
    n = xc.shape[0]
    nt = n // ROW_TILE
    tps = tiles_per_seq

    def row(i):
        return (i, 0)

    def mod_idx(i):
        return (i, 0, 0)

    const = lambda i: (0, 0)
    widths = [(256, BF16), (256, BF16), (256, BF16), (512, BF16), (512, BF16),
              (512, F32), (512, F32), (512, BF16), (256, BF16), (128, BF16), (128, BF16)]
    return pl.pallas_call(
        _inproj_kernel,
        grid=(nt,),
        in_specs=[
            pl.BlockSpec((ROW_TILE, D_MODEL), row),
            pl.BlockSpec((None, 2, D_MODEL), mod_idx),
            pl.BlockSpec((1, D_MODEL), const),
            pl.BlockSpec((D_MODEL, D_IN), const),
            pl.BlockSpec((ROW_TILE, 6 * 256), lambda i: (i % tps, 0)),
            pl.BlockSpec((4, 256), const),
            pl.BlockSpec((2, 1024), const),
            pl.BlockSpec((256, 256), const),
            pl.BlockSpec((256, 256), const),
        ],
        out_specs=[pl.BlockSpec((ROW_TILE, wd), row) for wd, _ in widths],
        out_shape=[jax.ShapeDtypeStruct((n, wd), dt) for wd, dt in widths],
        compiler_params=_cparams(("parallel",)),
        name="in_projection",
    )(xc, mod, g, w, rope, gains, lbt, ga, gc)


def _diff_attend(q, k, v, lam):
    lane = lax.broadcasted_iota(jnp.int32, q.shape, 1)
    out = jnp.zeros(q.shape, F32)
    for hd in range(DIFF_HEADS):
        es, rs = [], []
        for m in range(2):
            g = 2 * hd + m
            qg = jnp.where(lane // DIFF_HEAD_DIM == g, q, jnp.zeros_like(q))
            s = _dot_nt(qg, k)
            e = jnp.exp2(s - jnp.max(s, axis=-1, keepdims=True))
            es.append(e)
            rs.append(1.0 / jnp.sum(e, axis=-1, keepdims=True))
        a = (es[0] * rs[0] - es[1] * (lam * rs[1])).astype(BF16)
        o = _dot(a, v)
        out = out + jnp.where(lane // DIFF_V_DIM == hd, o, 0.0)
    return out


def _diff_kernel(q_ref, k_ref, v_ref, lam_ref, post_ref, g64_ref, o_ref, *, n_ctx):
    j = pl.program_id(1)
    lam = lam_ref[0:1, 0:1]

    def finish(o):
        msq = _group_mean_sq(o, g64_ref[...])
        o_ref[...] = (o * lax.rsqrt(msq + NORM_EPS) * post_ref[...]).astype(BF16)

    @pl.when(j == 0)
    def _():
        finish(_diff_attend(q_ref[...], k_ref[0:n_ctx, :], v_ref[0:n_ctx, :], lam))

    @pl.when(j > 0)
    def _():
        finish(_diff_attend(q_ref[...], k_ref[...], v_ref[...], lam))


def _diff_attention(qa, ka, va, lam_row, post_row, g64, n_batch, t_len, n_ctx):
    tps = t_len // ROW_TILE
    q3 = qa.reshape(n_batch, t_len, DIFF_WIDTH)
    k3 = ka.reshape(n_batch, t_len, DIFF_WIDTH)
    v3 = va.reshape(n_batch, t_len, DIFF_WIDTH)
    const = lambda b, j: (0, 0)
    out = pl.pallas_call(
        functools.partial(_diff_kernel, n_ctx=n_ctx),
        grid=(n_batch, tps),
        in_specs=[
            pl.BlockSpec((None, ROW_TILE, DIFF_WIDTH), lambda b, j: (b, j, 0)),
            pl.BlockSpec((None, t_len, DIFF_WIDTH), lambda b, j: (b, 0, 0)),
            pl.BlockSpec((None, t_len, DIFF_WIDTH), lambda b, j: (b, 0, 0)),
            pl.BlockSpec((1, 128), const),
            pl.BlockSpec((1, DIFF_WIDTH), const),
            pl.BlockSpec((256, 256), const),
        ],
        out_specs=pl.BlockSpec((None, ROW_TILE, DIFF_WIDTH), lambda b, j: (b, j, 0)),
        out_shape=jax.ShapeDtypeStruct((n_batch, t_len, DIFF_WIDTH), BF16),
        compiler_params=_cparams(("parallel", "arbitrary")),
        name="diff_attention",
    )(q3, k3, v3, lam_row, post_row, g64)
    return out.reshape(n_batch * t_len, DIFF_WIDTH)


def _swa_kernel(q_ref, k_ref, v_ref, sink_ref, o_ref, *, n_ctx, n_lat):
    j = pl.program_id(1)
    q = q_ref[...]
    tq = q.shape[0]
    kc = k_ref[0:n_ctx, :]
    vc = v_ref[0:n_ctx, :]
    lane = lax.broadcasted_iota(jnp.int32, (tq, SWA_KV_WIDTH), 1)
    span = ROW_TILE + 2 * WINDOW

    def run(kw, vw, valid):
        for g in range(SWA_GROUP):
            qg = q[:, SWA_KV_WIDTH * g:SWA_KV_WIDTH * (g + 1)]
            out = jnp.zeros((tq, SWA_KV_WIDTH), F32)
            for hd in range(SWA_KV_HEADS):
                sink = sink_ref[SWA_GROUP * hd + g:SWA_GROUP * hd + g + 1, 0:1]
                qh = jnp.where(lane // SWA_HEAD_DIM == hd, qg, jnp.zeros_like(qg))
                s_c = _dot_nt(qh, kc)
                mx = jnp.maximum(jnp.max(s_c, axis=-1, keepdims=True), sink)
                if kw is not None:
                    s_w = jnp.where(valid, _dot_nt(qh, kw), NEG_BIG)
                    mx = jnp.maximum(mx, jnp.max(s_w, axis=-1, keepdims=True))
                e_c = jnp.exp2(s_c - mx)
                den = jnp.sum(e_c, axis=-1, keepdims=True) + jnp.exp2(sink - mx)
                o = _dot(e_c.astype(BF16), vc)
                if kw is not None:
                    e_w = jnp.exp2(s_w - mx)
                    den = den + jnp.sum(e_w, axis=-1, keepdims=True)
                    o = o + _dot(e_w.astype(BF16), vw)
                out = out + jnp.where(lane // SWA_HEAD_DIM == hd, o * (1.0 / den), 0.0)
            o_ref[:, SWA_KV_WIDTH * g:SWA_KV_WIDTH * (g + 1)] = out.astype(BF16)

    @pl.when(j == 0)
    def _():
        run(None, None, None)

    @pl.when(j > 0)
    def _():
        q0 = (j - 1) * ROW_TILE
        ks = jnp.clip(q0 - WINDOW, 0, n_lat - span)
        ks = pl.multiple_of(ks, WINDOW)
        kw = k_ref[pl.ds(n_ctx + ks, span), :]
        vw = v_ref[pl.ds(n_ctx + ks, span), :]
        qpos = q0 + lax.broadcasted_iota(jnp.int32, (tq, span), 0)
        kpos = ks + lax.broadcasted_iota(jnp.int32, (tq, span), 1)
        run(kw, vw, jnp.abs(qpos - kpos) <= WINDOW)


def _swa_attention(qc, kc, vc, sink_tab, n_batch, t_len, n_ctx):
    tps = t_len // ROW_TILE
    q3 = qc.reshape(n_batch, t_len, SWA_WIDTH)
    k3 = kc.reshape(n_batch, t_len, SWA_KV_WIDTH)
    v3 = vc.reshape(n_batch, t_len, SWA_KV_WIDTH)
    out = pl.pallas_call(
        functools.partial(_swa_kernel, n_ctx=n_ctx, n_lat=t_len - n_ctx),
        grid=(n_batch, tps),
        in_specs=[
            pl.BlockSpec((None, ROW_TILE, SWA_WIDTH), lambda b, j: (b, j, 0)),
            pl.BlockSpec((None, t_len, SWA_KV_WIDTH), lambda b, j: (b, 0, 0)),
            pl.BlockSpec((None, t_len, SWA_KV_WIDTH), lambda b, j: (b, 0, 0)),
            pl.BlockSpec((SWA_Q_HEADS, 128), lambda b, j: (0, 0)),
        ],
        out_specs=pl.BlockSpec((None, ROW_TILE, SWA_WIDTH), lambda b, j: (b, j, 0)),
        out_shape=jax.ShapeDtypeStruct((n_batch, t_len, SWA_WIDTH), BF16),
        compiler_params=_cparams(("parallel", "arbitrary")),
        name="swa_attention",
    )(q3, k3, v3, sink_tab)
    return out.reshape(n_batch * t_len, SWA_WIDTH)


def _hgrn_kernel(q_ref, v_ref, lff_ref, lfb_ref, g_ref, ng_ref, o_ref, of_ref, ob_ref, st_ref, *, n_ctx, t_len):
    c = HGRN_CHUNK
    n_chunks = t_len // c
    n_ctx_chunks = n_ctx // c
    mid = c // 2
    ri = lax.broadcasted_iota(jnp.int32, (c, c), 0)
    ci = lax.broadcasted_iota(jnp.int32, (c, c), 1)
    causal = ri >= ci
    anti = ri <= ci
    tri_f = jnp.where(causal, 1.0, 0.0).astype(BF16)
    tri_b = jnp.where(anti, 1.0, 0.0).astype(BF16)

    st_ref[...] = jnp.zeros_like(st_ref)

    def body(i, _):
        rf = pl.multiple_of(i * c, c)
        cb = jnp.where(i < n_ctx_chunks, n_ctx_chunks - 1 - i, n_chunks - 1 + n_ctx_chunks - i)
        rb = pl.multiple_of(cb * c, c)
        chains = []
        for hd in range(HGRN_HEADS):
            cols = slice(HGRN_K * hd, HGRN_K * (hd + 1))
            chains.append((2 * hd, pl.ds(rf, c), cols, lff_ref, of_ref, tri_f, causal, mid - 1, c - 1))
            chains.append((2 * hd + 1, pl.ds(rb, c), cols, lfb_ref, ob_ref, tri_b, anti, mid, 0))
        cums = []
        for _, rows, cols, lf_ref, _, tri, _, _, _ in chains:
            lf_hi, lf_lo = _split_bf16(lf_ref[rows, cols])
            cums.append(_dot(tri, lf_hi) + _dot(tri, lf_lo))
        prods = []
        for (slot, rows, cols, lf_ref, _, _, mask, ref_row, end_row), cum in zip(chains, cums):
            ref = cum[ref_row:ref_row + 1, :]
            tot = cum[end_row:end_row + 1, :]
            v = v_ref[rows, cols]
            st = st_ref[slot]
            k = 1.0 - jnp.exp(lf_ref[rows, cols])
            qt = q_ref[rows, cols].astype(F32) * jnp.exp(cum - ref)
            kt = k * jnp.exp(ref - cum)
            sc = _dot_nt(qt.astype(BF16), kt.astype(BF16))
            inter = _dot_nt((qt * jnp.exp(ref)).astype(BF16), st.astype(BF16))
            upd = _dot_tn(v, (kt * jnp.exp(tot - ref)).astype(BF16))
            prods.append((sc, inter, st * jnp.exp(tot) + upd, v))
        for (slot, rows, cols, _, o_ref_d, _, mask, _, _), (sc, inter, st_new, v) in zip(chains, prods):
            o = _dot(jnp.where(mask, sc, 0.0).astype(BF16), v) + inter
            o_ref_d[rows, cols] = o.astype(BF16)
            st_ref[slot] = st_new
        return 0

    lax.fori_loop(0, n_chunks, body, 0)

    def fin(t, _):
        r = pl.multiple_of(t * ROW_TILE, ROW_TILE)
        for hd in range(HGRN_HEADS):
            cols = slice(HGRN_K * hd, HGRN_K * (hd + 1))
            o = of_ref[pl.ds(r, ROW_TILE), cols].astype(F32) + ob_ref[pl.ds(r, ROW_TILE), cols].astype(F32)
            ms = jnp.mean(o * o, axis=-1, keepdims=True)
            o = o * lax.rsqrt(ms + NORM_EPS) * ng_ref[...]
            o_ref[pl.ds(r, ROW_TILE), cols] = (o * g_ref[pl.ds(r, ROW_TILE), cols].astype(F32)).astype(BF16)
        return 0

    lax.fori_loop(0, t_len // ROW_TILE, fin, 0)


def _hgrn(hq, hv, lff, lfb, hg, norm_g, n_batch, t_len, n_ctx):
    def view(a):
        return a.reshape(n_batch, t_len, HGRN_WIDTH)

    blk = pl.BlockSpec((None, t_len, HGRN_WIDTH), lambda b: (b, 0, 0))
    out = pl.pallas_call(
        functools.partial(_hgrn_kernel, n_ctx=n_ctx, t_len=t_len),
        grid=(n_batch,),
        in_specs=[blk] * 5 + [pl.BlockSpec((1, HGRN_K), lambda b: (0, 0))],
        out_specs=blk,
        out_shape=jax.ShapeDtypeStruct((n_batch, t_len, HGRN_WIDTH), BF16),
        scratch_shapes=[pltpu.VMEM((t_len, HGRN_WIDTH), BF16), pltpu.VMEM((t_len, HGRN_WIDTH), BF16),
                        pltpu.VMEM((2 * HGRN_HEADS, HGRN_K, HGRN_K), F32)],
        compiler_params=_cparams(("parallel",)),
        name="hgrn2",
    )(view(hq), view(hv), view(lff), view(lfb), view(hg), norm_g)
    return out.reshape(n_batch * t_len, HGRN_WIDTH)


def _outproj_kernel(x_ref, a_ref, b_ref, c_ref, w_ref, mod_ref, g_ref, *rest, with_router):
    if with_router:
        r_ref, xo_ref, h_ref, gate_ref = rest
    else:
        xo_ref, h_ref = rest
    y = (_dot(a_ref[...], w_ref[0:256, :]) + _dot(b_ref[...], w_ref[256:768, :])
         + _dot(c_ref[...], w_ref[768:1024, :]))
    x = x_ref[...] + mod_ref[0:1, :] * y
    xo_ref[...] = x
    ms = jnp.mean(x * x, axis=-1, keepdims=True)
    h = (x * lax.rsqrt(ms + NORM_EPS) * g_ref[...]) * (1.0 + mod_ref[2:3, :]) + mod_ref[1:2, :]
    h_ref[...] = h.astype(BF16)
    if with_router:
        h_hi, h_lo = _split_bf16(h)
        r_hi, r_lo = _split_bf16(r_ref[...])
        logits = _dot(h_hi, r_hi) + _dot(h_lo, r_hi) + _dot(h_hi, r_lo)
        lane = lax.broadcasted_iota(jnp.int32, logits.shape, 1).astype(F32)
        logits = jnp.where(lane < N_EXPERTS, logits, NEG_BIG)
        v1 = jnp.max(logits, axis=-1, keepdims=True)
        i1 = jnp.min(jnp.where(logits == v1, lane, 128.0), axis=-1, keepdims=True)
        rest_l = jnp.where(lane == i1, NEG_BIG, logits)
        v2 = jnp.max(rest_l, axis=-1, keepdims=True)
        i2 = jnp.min(jnp.where(rest_l == v2, lane, 128.0), axis=-1, keepdims=True)
        e2 = jnp.exp(v2 - v1)
        w1 = 1.0 / (1.0 + e2)
        w2 = e2 * w1
        gate_ref[...] = jnp.where(lane == i1, w1, 0.0) + jnp.where(lane == i2, w2, 0.0)


def _out_projection(xc, a, b, c, w, mod, g, router):
    n = xc.shape[0]
    nt = n // ROW_TILE
    row = lambda i: (i, 0)
    const = lambda i: (0, 0)
    mod_idx = lambda i: (i, 0, 0)
    with_router = router is not None
    in_specs = [
        pl.BlockSpec((ROW_TILE, D_MODEL), row),
        pl.BlockSpec((ROW_TILE, DIFF_WIDTH), row),
        pl.BlockSpec((ROW_TILE, HGRN_WIDTH), row),
        pl.BlockSpec((ROW_TILE, SWA_WIDTH), row),
        pl.BlockSpec((D_MODEL, D_MODEL), const),
        pl.BlockSpec((None, 3, D_MODEL), mod_idx),
        pl.BlockSpec((1, D_MODEL), const),
    ]
    args = [xc, a, b, c, w, mod, g]
    out_specs = [pl.BlockSpec((ROW_TILE, D_MODEL), row), pl.BlockSpec((ROW_TILE, D_MODEL), row)]
    out_shape = [jax.ShapeDtypeStruct((n, D_MODEL), F32), jax.ShapeDtypeStruct((n, D_MODEL), BF16)]
    if with_router:
        in_specs.append(pl.BlockSpec((D_MODEL, 128), const))
        args.append(router)
        out_specs.append(pl.BlockSpec((ROW_TILE, 128), row))
        out_shape.append(jax.ShapeDtypeStruct((n, 128), F32))
    return pl.pallas_call(
        functools.partial(_outproj_kernel, with_router=with_router),
        grid=(nt,),
        in_specs=in_specs,
        out_specs=out_specs,
        out_shape=out_shape,
        compiler_params=_cparams(("parallel",)),
        name="out_projection",
    )(*args)


def _ffn_kernel(x_ref, h_ref, w1_ref, w3_ref, w2_ref, mod_ref, o_ref):
    h = h_ref[...]
    u = _dot(h, w1_ref[...])
    act = (_silu(u) * _dot(h, w3_ref[...])).astype(BF16)
    y = _dot(act, w2_ref[...])
    o_ref[...] = x_ref[...] + mod_ref[...] * y


def _dense_ffn(x, h, w1, w3, w2, mod):
    n = x.shape[0]
    nt = n // ROW_TILE
    row = lambda i: (i, 0)
    const = lambda i: (0, 0)
    mod_idx = lambda i: (i, 0, 0)
    return pl.pallas_call(
        _ffn_kernel,
        grid=(nt,),
        in_specs=[
            pl.BlockSpec((ROW_TILE, D_MODEL), row),
            pl.BlockSpec((ROW_TILE, D_MODEL), row),
            pl.BlockSpec((D_MODEL, D_FF), const),
            pl.BlockSpec((D_MODEL, D_FF), const),
            pl.BlockSpec((D_FF, D_MODEL), const),
            pl.BlockSpec((None, 1, D_MODEL), mod_idx),
        ],
        out_specs=pl.BlockSpec((ROW_TILE, D_MODEL), row),
        out_shape=jax.ShapeDtypeStruct((n, D_MODEL), F32),
        compiler_params=_cparams(("parallel",)),
        name="dense_ffn",
    )(x, h, w1, w3, w2, mod)


MOE_TM = 256
MOE_TS = 512


def _moe_gather_kernel(rt_ref, e_ref, s_ref, first_ref, last_ref, valid_ref,
                       h_ref, dest_ref, gates_ref, xs_ref, gs_ref, acc_ref, gacc_ref):
    w = pl.program_id(0)

    @pl.when(first_ref[w] == 1)
    def _():
        acc_ref[...] = jnp.zeros_like(acc_ref)
        gacc_ref[...] = jnp.zeros_like(gacc_ref)

    @pl.when(valid_ref[w] == 1)
    def _():
        rows = rt_ref[w] * MOE_TM + lax.broadcasted_iota(jnp.int32, (MOE_TM, MOE_TS), 0)
        onehot = jnp.where(dest_ref[...] == rows, 1.0, 0.0).astype(BF16)
        acc_ref[...] += _dot(onehot, h_ref[...])
        g = gates_ref[...]
        g_hi, g_rest = g.astype(BF16), g - g.astype(BF16).astype(F32)
        g_mid, g_lo = _split_bf16(g_rest)
        gacc_ref[...] += _dot(onehot, g_hi) + _dot(onehot, g_mid) + _dot(onehot, g_lo)

    @pl.when(last_ref[w] == 1)
    def _():
        xs_ref[...] = acc_ref[...].astype(BF16)
        lane = lax.broadcasted_iota(jnp.int32, gacc_ref.shape, 1)
        gate = jnp.sum(jnp.where(lane == e_ref[w], gacc_ref[...], 0.0), axis=-1, keepdims=True)
        gs_ref[...] = jnp.broadcast_to(gate, gs_ref.shape)


def _moe_expert_kernel(te_ref, tv_ref, tx_ref, xs_ref, gs_ref, w1_ref, w3_ref, w2_ref, ys_ref):
    r = pl.program_id(0)

    @pl.when(tv_ref[r] == 1)
    def _():
        h = xs_ref[...]
        u = _dot(h, w1_ref[...])
        act = (_silu(u) * _dot(h, w3_ref[...])).astype(BF16)
        ys_ref[...] = (gs_ref[:, 0:1] * _dot(act, w2_ref[...])).astype(BF16)

    @pl.when(tv_ref[r] == 0)
    def _():
        ys_ref[...] = jnp.zeros_like(ys_ref)


def _moe_combine_kernel(rt_ref, e_ref, s_ref, first_ref, last_ref, valid_ref,
                        ys_ref, dest_ref, x_ref, mod_ref, o_ref, acc_ref):
    w = pl.program_id(0)

    @pl.when(first_ref[w] == 1)
    def _():
        acc_ref[...] = jnp.zeros_like(acc_ref)

    @pl.when(valid_ref[w] == 1)
    def _():
        d = dest_ref[...]
        lane = lax.broadcasted_iota(jnp.int32, d.shape, 1)
        dcol = jnp.sum(jnp.where(lane == e_ref[w], d, 0.0), axis=-1, keepdims=True)
        cols = (rt_ref[w] * MOE_TM + lax.broadcasted_iota(jnp.int32, (MOE_TS, MOE_TM), 1)).astype(F32)
        onehot = jnp.where(dcol == cols, 1.0, 0.0).astype(BF16)
        acc_ref[...] += _dot(onehot, ys_ref[...])

    @pl.when(last_ref[w] == 1)
    def _():
        for s in range(mod_ref.shape[0]):
            rows = slice(ROW_TILE * s, ROW_TILE * (s + 1))
            o_ref[rows, :] = x_ref[rows, :] + mod_ref[s] * acc_ref[rows, :]


def _moe_plan(gates, n_rt, n_items):
    n = gates.shape[0]
    tm, ts = MOE_TM, MOE_TS
    nb = n // ts
    sel = gates[:, :N_EXPERTS] != 0.0
    si = sel.astype(jnp.int32)
    rank = jnp.cumsum(si, axis=0) - si
    counts = jnp.sum(si, axis=0)
    ntile = (counts + tm - 1) // tm
    tile_end = jnp.cumsum(ntile)
    tile_off = tile_end - ntile
    used = tile_end[-1]
    dest = jnp.where(sel, tile_off[None, :] * tm + rank, -1)
    blk_cnt = si.reshape(nb, ts, N_EXPERTS).sum(axis=1)
    blk_start = jnp.cumsum(blk_cnt, axis=0) - blk_cnt
    first_t = blk_start // tm
    last_t = (blk_start + blk_cnt - 1) // tm
    pair_items = jnp.where(blk_cnt > 0, last_t - first_t + 1, 0)
    first_rt = tile_off[None, :] + first_t

    def items(expert_major):
        eye = jnp.arange(N_EXPERTS, dtype=jnp.int32)
        blk = jnp.arange(nb, dtype=jnp.int32)
        if expert_major:
            cnt, frt = pair_items.T.reshape(-1), first_rt.T.reshape(-1)
            e_id, s_id = jnp.repeat(eye, nb), jnp.tile(blk, N_EXPERTS)
        else:
            cnt, frt = pair_items.reshape(-1), first_rt.reshape(-1)
            e_id, s_id = jnp.tile(eye, nb), jnp.repeat(blk, N_EXPERTS)
        ends = jnp.cumsum(cnt)
        starts = ends - cnt
        total = ends[-1]
        w = jnp.arange(n_items, dtype=jnp.int32)
        wc = jnp.minimum(w, total - 1)
        p = jnp.searchsorted(ends, wc, side="right").astype(jnp.int32)
        rt = (frt[p] + wc - starts[p]).astype(jnp.int32)
        e_w, s_w = e_id[p], s_id[p]
        valid = w < total
        key = rt if expert_major else s_w
        prev = jnp.concatenate([jnp.full((1,), -1, jnp.int32), key[:-1]])
        nxt = jnp.concatenate([key[1:], jnp.full((1,), -1, jnp.int32)])
        first = valid & (key != prev)
        last = valid & ((key != nxt) | (w == total - 1))
        i32 = lambda v: v.astype(jnp.int32)
        return rt, e_w, s_w, i32(first), i32(last), i32(valid)

    r = jnp.arange(n_rt, dtype=jnp.int32)
    rc = jnp.minimum(r, used - 1)
    tile_expert = jnp.minimum(jnp.searchsorted(tile_end, rc, side="right"), N_EXPERTS - 1).astype(jnp.int32)
    tile_valid = (r < used).astype(jnp.int32)
    return dest, items(True), items(False), (tile_expert, tile_valid, rc.astype(jnp.int32))


def _moe_ffn(x, h, gates, w1, w3, w2, mod):
    n = x.shape[0]
    tm, ts = MOE_TM, MOE_TS
    nb = n // ts
    n_rt = 2 * n // tm + N_EXPERTS
    n_items = n_rt + N_EXPERTS * nb
    n_rows = n_rt * tm
    dest, g_items, c_items, tiles = _moe_plan(gates, n_rt, n_items)
    dest_rows = dest.T.reshape(N_EXPERTS * nb, 1, ts)
    dest_cols = dest.astype(F32)

    xs, gs = pl.pallas_call(
        _moe_gather_kernel,
        grid_spec=pltpu.PrefetchScalarGridSpec(
            num_scalar_prefetch=6,
            grid=(n_items,),
            in_specs=[
                pl.BlockSpec((ts, D_MODEL), lambda w, rt, e, s, *_: (s[w], 0)),
                pl.BlockSpec((None, 1, ts), lambda w, rt, e, s, *_: (e[w] * nb + s[w], 0, 0)),
                pl.BlockSpec((ts, 128), lambda w, rt, e, s, *_: (s[w], 0)),
            ],
            out_specs=[
                pl.BlockSpec((tm, D_MODEL), lambda w, rt, *_: (rt[w], 0)),
                pl.BlockSpec((tm, 128), lambda w, rt, *_: (rt[w], 0)),
            ],
            scratch_shapes=[pltpu.VMEM((tm, D_MODEL), F32), pltpu.VMEM((tm, 128), F32)],
        ),
        out_shape=[jax.ShapeDtypeStruct((n_rows, D_MODEL), BF16), jax.ShapeDtypeStruct((n_rows, 128), F32)],
        compiler_params=_cparams(("arbitrary",)),
        name="moe_gather",
    )(*g_items, h, dest_rows, gates)

    ys = pl.pallas_call(
        _moe_expert_kernel,
        grid_spec=pltpu.PrefetchScalarGridSpec(
            num_scalar_prefetch=3,
            grid=(n_rt,),
            in_specs=[
                pl.BlockSpec((tm, D_MODEL), lambda r, te, tv, tx: (tx[r], 0)),
                pl.BlockSpec((tm, 128), lambda r, te, tv, tx: (tx[r], 0)),
                pl.BlockSpec((None, D_MODEL, D_FF), lambda r, te, tv, tx: (te[r], 0, 0)),
                pl.BlockSpec((None, D_MODEL, D_FF), lambda r, te, tv, tx: (te[r], 0, 0)),
                pl.BlockSpec((None, D_FF, D_MODEL), lambda r, te, tv, tx: (te[r], 0, 0)),
            ],
            out_specs=pl.BlockSpec((tm, D_MODEL), lambda r, te, tv, tx: (r, 0)),
        ),
        out_shape=jax.ShapeDtypeStruct((n_rows, D_MODEL), BF16),
        compiler_params=_cparams(("arbitrary",)),
        name="moe_experts",
    )(*tiles, xs, gs, w1, w3, w2)

    return pl.pallas_call(
        _moe_combine_kernel,
        grid_spec=pltpu.PrefetchScalarGridSpec(
            num_scalar_prefetch=6,
            grid=(n_items,),
            in_specs=[
                pl.BlockSpec((tm, D_MODEL), lambda w, rt, *_: (rt[w], 0)),
                pl.BlockSpec((ts, N_EXPERTS), lambda w, rt, e, s, *_: (s[w], 0)),
                pl.BlockSpec((ts, D_MODEL), lambda w, rt, e, s, *_: (s[w], 0)),
                pl.BlockSpec((ts // ROW_TILE, 1, D_MODEL), lambda w, rt, e, s, *_: (s[w], 0, 0)),
            ],
            out_specs=pl.BlockSpec((ts, D_MODEL), lambda w, rt, e, s, *_: (s[w], 0)),
            scratch_shapes=[pltpu.VMEM((ts, D_MODEL), F32)],
        ),
        out_shape=jax.ShapeDtypeStruct((n, D_MODEL), F32),
        compiler_params=_cparams(("arbitrary",)),
        name="moe_combine",
    )(*c_items, ys, dest_cols, x, mod)


def _rope_tables(n_ctx, n_lat):
    pos_r = jnp.arange(n_lat, dtype=jnp.int32) // GRID_W
    pos_c = jnp.arange(n_lat, dtype=jnp.int32) % GRID_W

    def per_head(head_dim):
        nf = head_dim // 4
        inv = ROPE_BASE ** (-jnp.arange(nf, dtype=F32) / nf)
        ang_r = pos_r.astype(F32)[:, None] * inv[None, :]
        ang_c = pos_c.astype(F32)[:, None] * inv[None, :]
        z = jnp.zeros_like(ang_r)
        cos = jnp.concatenate([jnp.cos(ang_r)] * 2 + [jnp.cos(ang_c)] * 2, axis=-1)
        sa = jnp.concatenate([-jnp.sin(ang_r), z, -jnp.sin(ang_c), z], axis=-1)
        sb = jnp.concatenate([z, jnp.sin(ang_r), z, jnp.sin(ang_c)], axis=-1)
        reps = 256 // head_dim
        tabs = [jnp.tile(t, (1, reps)) for t in (cos, sa, sb)]
        ctx = [jnp.ones((n_ctx, 256), F32), jnp.zeros((n_ctx, 256), F32), jnp.zeros((n_ctx, 256), F32)]
        return [jnp.concatenate([c, t], axis=0) for c, t in zip(ctx, tabs)]

    return jnp.concatenate(per_head(DIFF_HEAD_DIM) + per_head(SWA_HEAD_DIM), axis=-1)


def _block_diag_mean(group):
    idx = jnp.arange(256) // group
    return jnp.where(idx[:, None] == idx[None, :], 1.0 / group, 0.0).astype(BF16)


def kernel(x, c, ctx, c_ctx, ada_w, ada_b, norm_mix_g, norm_ffn_g, w_in, w_out, diff_qk_norm_g, diff_lambda,
           diff_subln_g, hgrn_lb_logits, hgrn_norm_g, swa_qk_norm_g, swa_sink, ffn_w1, ffn_w3, ffn_w2,
           moe_router, moe_w1, moe_w3, moe_w2):
    n_batch, n_lat, _ = x.shape
    n_ctx = ctx.shape[1]
    depth = ada_w.shape[0]
    t_len = n_ctx + n_lat
    tps = t_len // ROW_TILE
    assert n_ctx == ROW_TILE and n_lat % ROW_TILE == 0 and n_lat >= ROW_TILE + 2 * WINDOW
    assert (n_batch * t_len) % MOE_TS == 0

    xc = jnp.concatenate([ctx, x], axis=1).reshape(n_batch * t_len, D_MODEL)

    n_rows = -(-(n_batch + 1) // 8) * 8
    cond = jnp.concatenate([c, c_ctx[None, :], jnp.zeros((n_rows - n_batch - 1, D_MODEL), F32)], axis=0)
    mods = _ada_modulation(cond, ada_w, ada_b).reshape(depth, n_rows, 6, D_MODEL)
    m_lat = jnp.broadcast_to(mods[:, :n_batch, None], (depth, n_batch, tps - 1, 6, D_MODEL))
    m_ctx = jnp.broadcast_to(mods[:, n_batch, None, None], (depth, n_batch, 1, 6, D_MODEL))
    mods = jnp.concatenate([m_ctx, m_lat], axis=2).reshape(depth, n_batch * tps, 6, D_MODEL)

    lb = jnp.cumsum(jax.nn.softmax(hgrn_lb_logits.astype(F32), axis=1), axis=1)
    lb = lb - lb[:, :1]
    rope = _rope_tables(n_ctx, n_lat)
    g32 = _block_diag_mean(DIFF_HEAD_DIM)
    g64 = _block_diag_mean(SWA_HEAD_DIM)

    perm_q = jnp.arange(SWA_WIDTH).reshape(SWA_KV_HEADS, SWA_GROUP, SWA_HEAD_DIM).transpose(1, 0, 2).reshape(-1)
    qc0 = 3 * DIFF_WIDTH + 5 * HGRN_WIDTH
    col_perm = jnp.concatenate([jnp.arange(qc0), qc0 + perm_q, jnp.arange(qc0 + SWA_WIDTH, D_IN)])
    oc0 = DIFF_WIDTH + HGRN_WIDTH
    row_perm = jnp.concatenate([jnp.arange(oc0), oc0 + perm_q])

    for layer in range(depth):
        lam_init = 0.8 - 0.6 * math.exp(-0.3 * layer)
        mod = mods[layer]
        w_in_l = w_in[layer][:, col_perm].astype(BF16)
        w_out_l = w_out[layer][row_perm, :].astype(BF16)
        gains = jnp.stack([
            jnp.tile(diff_qk_norm_g[layer, 0], 8) * (DIFF_HEAD_DIM ** -0.5 * LOG2_E),
            jnp.tile(diff_qk_norm_g[layer, 1], 8),
            jnp.tile(swa_qk_norm_g[layer, 0], 4) * (SWA_HEAD_DIM ** -0.5 * LOG2_E),
            jnp.tile(swa_qk_norm_g[layer, 1], 4),
        ]).astype(F32)
        lbt = jnp.stack([
            jnp.log(lb[:, layer]).reshape(-1),
            jnp.log1p(-lb[:, layer]).reshape(-1),
        ]).astype(F32)
        lv = diff_lambda[layer].astype(F32)
        lam = jnp.exp(jnp.sum(lv[0] * lv[1])) - jnp.exp(jnp.sum(lv[2] * lv[3])) + lam_init
        lam_row = jnp.full((1, 128), lam, F32)
        post_row = (jnp.tile(diff_subln_g[layer], DIFF_HEADS) * (1.0 - lam_init)).reshape(1, DIFF_WIDTH).astype(F32)
        sink_tab = jnp.broadcast_to(swa_sink[layer].astype(F32)[:, None] * LOG2_E, (SWA_Q_HEADS, 128))

        (qa, ka, va, hq, hv, lff, lfb, hg, qc, kc, vc) = _in_projection(
            xc, mod[:, 0:2], norm_mix_g[layer].reshape(1, D_MODEL), w_in_l, rope, gains, lbt, g32, g64, tps)
        a = _diff_attention(qa, ka, va, lam_row, post_row, g64, n_batch, t_len, n_ctx)
        b = _hgrn(hq, hv, lff, lfb, hg, hgrn_norm_g[layer].reshape(1, HGRN_K), n_batch, t_len, n_ctx)
        cc = _swa_attention(qc, kc, vc, sink_tab, n_batch, t_len, n_ctx)

        is_moe = layer % 2 == 1
        jj = layer // 2
        router = None
        if is_moe:
            router = jnp.pad(moe_router[jj].astype(F32), ((0, 0), (0, 128 - N_EXPERTS)))
        outs = _out_projection(xc, a, b, cc, w_out_l, mod[:, 2:5], norm_ffn_g[layer].reshape(1, D_MODEL),
                               router)
        if is_moe:
            x_mid, h2, gates = outs
            xc = _moe_ffn(x_mid, h2, gates, moe_w1[jj].astype(BF16), moe_w3[jj].astype(BF16),
                          moe_w2[jj].astype(BF16), mod[:, 5:6])
        else:
            x_mid, h2 = outs
            xc = _dense_ffn(x_mid, h2, ffn_w1[jj].astype(BF16), ffn_w3[jj].astype(BF16),
                            ffn_w2[jj].astype(BF16), mod[:, 5:6])

    return xc.reshape(n_batch, t_len, D_MODEL)[:, n_ctx:, :]
```

```python
import functools
import math

import jax
import jax.numpy as jnp
from jax import lax
from jax.experimental import pallas as pl
from jax.experimental.pallas import tpu as pltpu

D_MODEL = 1024
GRID_W = 64
DIFF_HEADS = 4
DIFF_HEAD_DIM = 32
DIFF_V_DIM = 64
DIFF_WIDTH = 256
HGRN_HEADS = 4
HGRN_K = 128
HGRN_WIDTH = 512
SWA_Q_HEADS = 4
SWA_KV_HEADS = 2
SWA_GROUP = 2
SWA_HEAD_DIM = 64
SWA_WIDTH = 256
SWA_KV_WIDTH = 128
WINDOW = 128
D_FF = 2816
N_EXPERTS = 8
ROPE_BASE = 10000.0
NORM_EPS = 1e-6
D_IN = 3840

F32 = jnp.float32
BF16 = jnp.bfloat16

ROW_TILE = 256
HGRN_CHUNK = 64
HGRN_UNROLL = 4
NEG_BIG = -1e30
LOG2_E = 1.4426950408889634
VMEM_LIMIT = 56 * 1024 * 1024


def _cparams(sem):
    return pltpu.CompilerParams(dimension_semantics=sem, vmem_limit_bytes=VMEM_LIMIT)


def _split_bf16(v):
    hi = v.astype(BF16)
    lo = (v - hi.astype(F32)).astype(BF16)
    return hi, lo


def _dot(a, b):
    return jnp.dot(a, b, preferred_element_type=F32)


def _dot_nt(a, b):
    return lax.dot_general(a, b, (((1,), (1,)), ((), ())), preferred_element_type=F32)


def _dot_tn(a, b):
    return lax.dot_general(a, b, (((0,), (0,)), ((), ())), preferred_element_type=F32)


def _group_mean_sq(y, gmat):
    hi, lo = _split_bf16(y * y)
    return _dot(hi, gmat) + _dot(lo, gmat)


def _silu(v):
    return v * (1.0 / (1.0 + jnp.exp(-v)))


def _ada_kernel(s_ref, w_ref, b_ref, o_ref):
    s = s_ref[...]
    s = _silu(s)
    s_hi, s_lo = _split_bf16(s)
    w_hi, w_lo = _split_bf16(w_ref[...])
    o_ref[...] = _dot(s_hi, w_hi) + _dot(s_lo, w_hi) + _dot(s_hi, w_lo) + b_ref[...]


def _ada_modulation(cond, ada_w, ada_b):
    depth = ada_w.shape[0]
    r = cond.shape[0]
    nblk = 6 * D_MODEL // 1024
    return pl.pallas_call(
        _ada_kernel,
        grid=(depth, nblk),
        in_specs=[
            pl.BlockSpec((r, D_MODEL), lambda l, n: (0, 0)),
            pl.BlockSpec((None, D_MODEL, 1024), lambda l, n: (l, 0, n)),
            pl.BlockSpec((None, 1, 1024), lambda l, n: (l, 0, n)),
        ],
        out_specs=pl.BlockSpec((None, r, 1024), lambda l, n: (l, 0, n)),
        out_shape=jax.ShapeDtypeStruct((depth, r, 6 * D_MODEL), F32),
        compiler_params=_cparams(("parallel", "parallel")),
        name="ada_modulation",
    )(cond, ada_w, ada_b.reshape(depth, 1, 6 * D_MODEL))


def _rope(y, cos, sa, sb, half):
    w = y.shape[-1]
    fwd = pltpu.roll(y, w - half, 1)
    bwd = pltpu.roll(y, half, 1)
    return y * cos + fwd * sa + bwd * sb


def _inproj_kernel(x_ref, mod_ref, g_ref, w_ref, rope_ref, gains_ref, lbt_ref, ga_ref, gc_ref,
                   qa_ref, ka_ref, va_ref, hq_ref, hv_ref, lff_ref, lfb_ref, hg_ref,
                   qc_ref, kc_ref, vc_ref):
    x = x_ref[...]
    shift = mod_ref[0:1, :]
    scale = mod_ref[1:2, :]
    ms = jnp.mean(x * x, axis=-1, keepdims=True)
    h = (x * lax.rsqrt(ms + NORM_EPS) * g_ref[...]) * (1.0 + scale) + shift
    hb = h.astype(BF16)

    def proj(a, b):
        return _dot(hb, w_ref[:, a:b])

    def qk_prep(y, gmat, gain, cos, sa, sb, half):
        msq = _group_mean_sq(y, gmat)
        y = y * lax.rsqrt(msq + NORM_EPS) * gain
        return _rope(y, cos, sa, sb, half)

    ga = ga_ref[...]
    gc = gc_ref[...]
    ra = [rope_ref[:, 256 * i:256 * (i + 1)] for i in range(6)]
    qa = qk_prep(proj(0, 256), ga, gains_ref[0:1, :], ra[0], ra[1], ra[2], 8)
    qa_ref[...] = qa.astype(BF16)
    ka = qk_prep(proj(256, 512), ga, gains_ref[1:2, :], ra[0], ra[1], ra[2], 8)
    ka_ref[...] = ka.astype(BF16)
    va_ref[...] = proj(512, 768).astype(BF16)
    hq_ref[...] = _silu(proj(768, 1280)).astype(BF16)
    hv_ref[...] = proj(1280, 1792).astype(BF16)
    for d, lf_ref in enumerate((lff_ref, lfb_ref)):
        z = proj(1792 + 512 * d, 2304 + 512 * d)
        log_lb = lbt_ref[0:1, 512 * d:512 * (d + 1)]
        log1m_lb = lbt_ref[1:2, 512 * d:512 * (d + 1)]
        sp = jnp.maximum(-z, 0.0) + jnp.log(1.0 + jnp.exp(-jnp.abs(z)))
        b2 = log1m_lb - sp
        mx = jnp.maximum(log_lb, b2)
        lf_ref[...] = mx + jnp.log(1.0 + jnp.exp(-jnp.abs(log_lb - b2)))
    hg_ref[...] = _silu(proj(2816, 3328)).astype(BF16)
    qc = qk_prep(proj(3328, 3584), gc, gains_ref[2:3, :], ra[3], ra[4], ra[5], 16)
    qc_ref[...] = qc.astype(BF16)
    kc = qk_prep(proj(3584, 3712), gc[0:128, 0:128], gains_ref[3:4, 0:128],
                 ra[3][:, 0:128], ra[4][:, 0:128], ra[5][:, 0:128], 16)
    kc_ref[...] = kc.astype(BF16)
    vc_ref[...] = proj(3712, 3840).astype(BF16)


def _in_projection(xc, mod, g, w, rope, gains, lbt, ga, gc, tiles_per_seq):
    n = xc.shape[0]
    nt = n // ROW_TILE
    tps = tiles_per_seq

    def row(i):
        return (i, 0)

    def mod_idx(i):
        return (i, 0, 0)

    const = lambda i: (0, 0)
    widths = [(256, BF16), (256, BF16), (256, BF16), (512, BF16), (512, BF16),
              (512, F32), (512, F32), (512, BF16), (256, BF16), (128, BF16), (128, BF16)]
    return pl.pallas_call(
        _inproj_kernel,
        grid=(nt,),
        in_specs=[
            pl.BlockSpec((ROW_TILE, D_MODEL), row),
            pl.BlockSpec((None, 2, D_MODEL), mod_idx),
            pl.BlockSpec((1, D_MODEL), const),
            pl.BlockSpec((D_MODEL, D_IN), const),
            pl.BlockSpec((ROW_TILE, 6 * 256), lambda i: (i % tps, 0)),
            pl.BlockSpec((4, 256), const),
            pl.BlockSpec((2, 1024), const),
            pl.BlockSpec((256, 256), const),
            pl.BlockSpec((256, 256), const),
        ],
        out_specs=[pl.BlockSpec((ROW_TILE, wd), row) for wd, _ in widths],
        out_shape=[jax.ShapeDtypeStruct((n, wd), dt) for wd, dt in widths],
        compiler_params=_cparams(("parallel",)),
        name="in_projection",
    )(xc, mod, g, w, rope, gains, lbt, ga, gc)


def _diff_attend(q, k, v, lam):
    lane = lax.broadcasted_iota(jnp.int32, q.shape, 1)
    out = jnp.zeros(q.shape, F32)
    for hd in range(DIFF_HEADS):
        parts = []
        for m in range(2):
            g = 2 * hd + m
            qg = jnp.where(lane // DIFF_HEAD_DIM == g, q, jnp.zeros_like(q))
            s = _dot_nt(qg, k)
            e = jnp.exp2(s - jnp.max(s, axis=-1, keepdims=True)).astype(BF16)
            o = _dot(e, v[hd])
            parts.append(o * (1.0 / pltpu.roll(o, DIFF_WIDTH - DIFF_V_DIM, 1)))
        out = out + jnp.where(lane // DIFF_V_DIM == hd, parts[0] - lam * parts[1], 0.0)
    return out


def _diff_kernel(q_ref, k_ref, v_ref, lam_ref, post_ref, g64_ref, o_ref, vaug_ref, *, n_ctx):
    j = pl.program_id(1)
    lam = lam_ref[0:1, 0:1]

    @pl.when(j == 0)
    def _():
        v = v_ref[...]
        lane = lax.broadcasted_iota(jnp.int32, v.shape, 1)
        for hd in range(DIFF_HEADS):
            vaug_ref[hd] = jnp.where(lane // DIFF_V_DIM == (hd + 1) % DIFF_HEADS, jnp.ones_like(v), v)

    def finish(o):
        msq = _group_mean_sq(o, g64_ref[...])
        o_ref[...] = (o * lax.rsqrt(msq + NORM_EPS) * post_ref[...]).astype(BF16)

    @pl.when(j == 0)
    def _():
        finish(_diff_attend(q_ref[...], k_ref[0:n_ctx, :], [vaug_ref[hd, 0:n_ctx, :] for hd in range(DIFF_HEADS)],
                            lam))

    @pl.when(j > 0)
    def _():
        finish(_diff_attend(q_ref[...], k_ref[...], [vaug_ref[hd] for hd in range(DIFF_HEADS)], lam))


def _diff_attention(qa, ka, va, lam_row, post_row, g64, n_batch, t_len, n_ctx):
    tps = t_len // ROW_TILE
    q3 = qa.reshape(n_batch, t_len, DIFF_WIDTH)
    k3 = ka.reshape(n_batch, t_len, DIFF_WIDTH)
    v3 = va.reshape(n_batch, t_len, DIFF_WIDTH)
    const = lambda b, j: (0, 0)
    out = pl.pallas_call(
        functools.partial(_diff_kernel, n_ctx=n_ctx),
        grid=(n_batch, tps),
        in_specs=[
            pl.BlockSpec((None, ROW_TILE, DIFF_WIDTH), lambda b, j: (b, j, 0)),
            pl.BlockSpec((None, t_len, DIFF_WIDTH), lambda b, j: (b, 0, 0)),
            pl.BlockSpec((None, t_len, DIFF_WIDTH), lambda b, j: (b, 0, 0)),
            pl.BlockSpec((1, 128), const),
            pl.BlockSpec((1, DIFF_WIDTH), const),
            pl.BlockSpec((256, 256), const),
        ],
        out_specs=pl.BlockSpec((None, ROW_TILE, DIFF_WIDTH), lambda b, j: (b, j, 0)),
        out_shape=jax.ShapeDtypeStruct((n_batch, t_len, DIFF_WIDTH), BF16),
        scratch_shapes=[pltpu.VMEM((DIFF_HEADS, t_len, DIFF_WIDTH), BF16)],
        compiler_params=_cparams(("parallel", "arbitrary")),
        name="diff_attention",
    )(q3, k3, v3, lam_row, post_row, g64)
    return out.reshape(n_batch * t_len, DIFF_WIDTH)


def _swa_kernel(q_ref, k_ref, v_ref, sink_ref, o_ref, *, n_ctx, n_lat):
    j = pl.program_id(1)
    q = q_ref[...]
    tq = q.shape[0]
    lane = lax.broadcasted_iota(jnp.int32, (tq, SWA_KV_WIDTH), 1)
    span = ROW_TILE + 2 * WINDOW
    heads = [(g, hd) for g in range(SWA_GROUP) for hd in range(SWA_KV_HEADS)]

    def run(k, v, valid):
        vlane = lax.broadcasted_iota(jnp.int32, v.shape, 1)
        vaug = [jnp.where(vlane // SWA_HEAD_DIM == hd, v, jnp.ones_like(v)) for hd in range(SWA_KV_HEADS)]
        sinks = [sink_ref[SWA_GROUP * hd + g:SWA_GROUP * hd + g + 1, 0:1] for g, hd in heads]
        scores = []
        for g, hd in heads:
            qg = q[:, SWA_KV_WIDTH * g:SWA_KV_WIDTH * (g + 1)]
            s = _dot_nt(jnp.where(lane // SWA_HEAD_DIM == hd, qg, jnp.zeros_like(qg)), k)
            scores.append(s if valid is None else jnp.where(valid, s, NEG_BIG))
        maxes = [jnp.maximum(jnp.max(s, axis=-1, keepdims=True), sink) for s, sink in zip(scores, sinks)]
        outs = [_dot(jnp.exp2(s - mx).astype(BF16), vaug[hd]) for s, mx, (g, hd) in zip(scores, maxes, heads)]
        for g in range(SWA_GROUP):
            out = jnp.zeros((tq, SWA_KV_WIDTH), F32)
            for i, (gi, hd) in enumerate(heads):
                if gi == g:
                    den = pltpu.roll(outs[i], SWA_HEAD_DIM, 1) + jnp.exp2(sinks[i] - maxes[i])
                    out = out + jnp.where(lane // SWA_HEAD_DIM == hd, outs[i] * (1.0 / den), 0.0)
            o_ref[:, SWA_KV_WIDTH * g:SWA_KV_WIDTH * (g + 1)] = out.astype(BF16)

    @pl.when(j == 0)
    def _():
        run(k_ref[0:n_ctx, :], v_ref[0:n_ctx, :], None)

    @pl.when(j > 0)
    def _():
        q0 = (j - 1) * ROW_TILE
        ks = jnp.clip(q0 - WINDOW, 0, n_lat - span)
        ks = pl.multiple_of(ks, WINDOW)
        k = jnp.concatenate([k_ref[0:n_ctx, :], k_ref[pl.ds(n_ctx + ks, span), :]], axis=0)
        v = jnp.concatenate([v_ref[0:n_ctx, :], v_ref[pl.ds(n_ctx + ks, span), :]], axis=0)
        qpos = q0 + lax.broadcasted_iota(jnp.int32, (tq, n_ctx + span), 0)
        kpos = ks - n_ctx + lax.broadcasted_iota(jnp.int32, (tq, n_ctx + span), 1)
        run(k, v, (kpos < ks) | (jnp.abs(qpos - kpos) <= WINDOW))


def _swa_attention(qc, kc, vc, sink_tab, n_batch, t_len, n_ctx):
    tps = t_len // ROW_TILE
    q3 = qc.reshape(n_batch, t_len, SWA_WIDTH)
    k3 = kc.reshape(n_batch, t_len, SWA_KV_WIDTH)
    v3 = vc.reshape(n_batch, t_len, SWA_KV_WIDTH)
    out = pl.pallas_call(
        functools.partial(_swa_kernel, n_ctx=n_ctx, n_lat=t_len - n_ctx),
        grid=(n_batch, tps),
        in_specs=[
            pl.BlockSpec((None, ROW_TILE, SWA_WIDTH), lambda b, j: (b, j, 0)),
            pl.BlockSpec((None, t_len, SWA_KV_WIDTH), lambda b, j: (b, 0, 0)),
            pl.BlockSpec((None, t_len, SWA_KV_WIDTH), lambda b, j: (b, 0, 0)),
            pl.BlockSpec((SWA_Q_HEADS, 128), lambda b, j: (0, 0)),
        ],
        out_specs=pl.BlockSpec((None, ROW_TILE, SWA_WIDTH), lambda b, j: (b, j, 0)),
        out_shape=jax.ShapeDtypeStruct((n_batch, t_len, SWA_WIDTH), BF16),
        compiler_params=_cparams(("parallel", "arbitrary")),
        name="swa_attention",
    )(q3, k3, v3, sink_tab)
    return out.reshape(n_batch * t_len, SWA_WIDTH)


def _hgrn_kernel(q_ref, v_ref, lff_ref, lfb_ref, g_ref, ng_ref, o_ref, of_ref, ob_ref, st_ref, *, n_ctx, t_len):
    c = HGRN_CHUNK
    n_chunks = t_len // c
    n_ctx_chunks = n_ctx // c
    mid = c // 2
    ri = lax.broadcasted_iota(jnp.int32, (c, c), 0)
    ci = lax.broadcasted_iota(jnp.int32, (c, c), 1)
    causal = ri >= ci
    anti = ri <= ci
    tri_f = jnp.where(causal, 1.0, 0.0).astype(BF16)
    tri_b = jnp.where(anti, 1.0, 0.0).astype(BF16)

    st_ref[...] = jnp.zeros_like(st_ref)

    def body(i, _):
        chains = []
        for u in range(HGRN_UNROLL):
            step = i * HGRN_UNROLL + u
            rf = pl.multiple_of(step * c, c)
            cb = jnp.where(step < n_ctx_chunks, n_ctx_chunks - 1 - step, n_chunks - 1 + n_ctx_chunks - step)
            rb = pl.multiple_of(cb * c, c)
            for hd in range(HGRN_HEADS):
                cols = slice(HGRN_K * hd, HGRN_K * (hd + 1))
                chains.append((2 * hd, pl.ds(rf, c), cols, lff_ref, of_ref, tri_f, causal, mid - 1, c - 1))
                chains.append((2 * hd + 1, pl.ds(rb, c), cols, lfb_ref, ob_ref, tri_b, anti, mid, 0))
        cums = []
        for _, rows, cols, lf_ref, _, tri, _, _, _ in chains:
            lf_hi, lf_lo = _split_bf16(lf_ref[rows, cols])
            cums.append(_dot(tri, lf_hi) + _dot(tri, lf_lo))
        prods = []
        for (slot, rows, cols, lf_ref, _, _, mask, ref_row, end_row), cum in zip(chains, cums):
            ref = cum[ref_row:ref_row + 1, :]
            tot = cum[end_row:end_row + 1, :]
            v = v_ref[rows, cols]
            k = 1.0 - jnp.exp(lf_ref[rows, cols])
            qt = q_ref[rows, cols].astype(F32) * jnp.exp(cum - ref)
            kt = k * jnp.exp(ref - cum)
            sc = _dot_nt(qt.astype(BF16), kt.astype(BF16))
            upd = _dot_tn(v, (kt * jnp.exp(tot - ref)).astype(BF16))
            prods.append((sc, (qt * jnp.exp(ref)).astype(BF16), jnp.exp(tot), upd, v))
        states = {}
        for (slot, rows, cols, _, o_ref_d, _, mask, _, _), (sc, q_in, decay, upd, v) in zip(chains, prods):
            st = states[slot] if slot in states else st_ref[slot]
            o = _dot(jnp.where(mask, sc, 0.0).astype(BF16), v) + _dot_nt(q_in, st.astype(BF16))
            o_ref_d[rows, cols] = o.astype(BF16)
            states[slot] = st * decay + upd
        for slot, st in states.items():
            st_ref[slot] = st
        return 0

    lax.fori_loop(0, n_chunks // HGRN_UNROLL, body, 0)

    def fin(t, _):
        r = pl.multiple_of(t * ROW_TILE, ROW_TILE)
        for hd in range(HGRN_HEADS):
            cols = slice(HGRN_K * hd, HGRN_K * (hd + 1))
            o = of_ref[pl.ds(r, ROW_TILE), cols].astype(F32) + ob_ref[pl.ds(r, ROW_TILE), cols].astype(F32)
            ms = jnp.mean(o * o, axis=-1, keepdims=True)
            o = o * lax.rsqrt(ms + NORM_EPS) * ng_ref[...]
            o_ref[pl.ds(r, ROW_TILE), cols] = (o * g_ref[pl.ds(r, ROW_TILE), cols].astype(F32)).astype(BF16)
        return 0

    lax.fori_loop(0, t_len // ROW_TILE, fin, 0)


def _hgrn(hq, hv, lff, lfb, hg, norm_g, n_batch, t_len, n_ctx):
    def view(a):
        return a.reshape(n_batch, t_len, HGRN_WIDTH)

    blk = pl.BlockSpec((None, t_len, HGRN_WIDTH), lambda b: (b, 0, 0))
    out = pl.pallas_call(
        functools.partial(_hgrn_kernel, n_ctx=n_ctx, t_len=t_len),
        grid=(n_batch,),
        in_specs=[blk] * 5 + [pl.BlockSpec((1, HGRN_K), lambda b: (0, 0))],
        out_specs=blk,
        out_shape=jax.ShapeDtypeStruct((n_batch, t_len, HGRN_WIDTH), BF16),
        scratch_shapes=[pltpu.VMEM((t_len, HGRN_WIDTH), BF16), pltpu.VMEM((t_len, HGRN_WIDTH), BF16),
                        pltpu.VMEM((2 * HGRN_HEADS, HGRN_K, HGRN_K), F32)],
        compiler_params=_cparams(("parallel",)),
        name="hgrn2",
    )(view(hq), view(hv), view(lff), view(lfb), view(hg), norm_g)
    return out.reshape(n_batch * t_len, HGRN_WIDTH)


def _outproj_kernel(x_ref, a_ref, b_ref, c_ref, w_ref, mod_ref, g_ref, *rest, with_router):
    if with_router:
        r_ref, xo_ref, h_ref, gate_ref = rest
    else:
        xo_ref, h_ref = rest
    y = (_dot(a_ref[...], w_ref[0:256, :]) + _dot(b_ref[...], w_ref[256:768, :])
         + _dot(c_ref[...], w_ref[768:1024, :]))
    x = x_ref[...] + mod_ref[0:1, :] * y
    xo_ref[...] = x
    ms = jnp.mean(x * x, axis=-1, keepdims=True)
    h = (x * lax.rsqrt(ms + NORM_EPS) * g_ref[...]) * (1.0 + mod_ref[2:3, :]) + mod_ref[1:2, :]
    h_ref[...] = h.astype(BF16)
    if with_router:
        h_hi, h_lo = _split_bf16(h)
        r_hi, r_lo = _split_bf16(r_ref[...])
        logits = _dot(h_hi, r_hi) + _dot(h_lo, r_hi) + _dot(h_hi, r_lo)
        lane = lax.broadcasted_iota(jnp.int32, logits.shape, 1).astype(F32)
        logits = jnp.where(lane < N_EXPERTS, logits, NEG_BIG)
        v1 = jnp.max(logits, axis=-1, keepdims=True)
        i1 = jnp.min(jnp.where(logits == v1, lane, 128.0), axis=-1, keepdims=True)
        rest_l = jnp.where(lane == i1, NEG_BIG, logits)
        v2 = jnp.max(rest_l, axis=-1, keepdims=True)
        i2 = jnp.min(jnp.where(rest_l == v2, lane, 128.0), axis=-1, keepdims=True)
        e2 = jnp.exp(v2 - v1)
        w1 = 1.0 / (1.0 + e2)
        w2 = e2 * w1
        gate_ref[...] = jnp.where(lane == i1, w1, 0.0) + jnp.where(lane == i2, w2, 0.0)


def _out_projection(xc, a, b, c, w, mod, g, router):
    n = xc.shape[0]
    nt = n // ROW_TILE
    row = lambda i: (i, 0)
    const = lambda i: (0, 0)
    mod_idx = lambda i: (i, 0, 0)
    with_router = router is not None
    in_specs = [
        pl.BlockSpec((ROW_TILE, D_MODEL), row),
        pl.BlockSpec((ROW_TILE, DIFF_WIDTH), row),
        pl.BlockSpec((ROW_TILE, HGRN_WIDTH), row),
        pl.BlockSpec((ROW_TILE, SWA_WIDTH), row),
        pl.BlockSpec((D_MODEL, D_MODEL), const),
        pl.BlockSpec((None, 3, D_MODEL), mod_idx),
        pl.BlockSpec((1, D_MODEL), const),
    ]
    args = [xc, a, b, c, w, mod, g]
    out_specs = [pl.BlockSpec((ROW_TILE, D_MODEL), row), pl.BlockSpec((ROW_TILE, D_MODEL), row)]
    out_shape = [jax.ShapeDtypeStruct((n, D_MODEL), F32), jax.ShapeDtypeStruct((n, D_MODEL), BF16)]
    if with_router:
        in_specs.append(pl.BlockSpec((D_MODEL, 128), const))
        args.append(router)
        out_specs.append(pl.BlockSpec((ROW_TILE, 128), row))
        out_shape.append(jax.ShapeDtypeStruct((n, 128), F32))
    return pl.pallas_call(
        functools.partial(_outproj_kernel, with_router=with_router),
        grid=(nt,),
        in_specs=in_specs,
        out_specs=out_specs,
        out_shape=out_shape,
        compiler_params=_cparams(("parallel",)),
        name="out_projection",
    )(*args)


def _ffn_kernel(x_ref, h_ref, w1_ref, w3_ref, w2_ref, mod_ref, o_ref):
    h = h_ref[...]
    u = _dot(h, w1_ref[...])
    act = (_silu(u) * _dot(h, w3_ref[...])).astype(BF16)
    y = _dot(act, w2_ref[...])
    o_ref[...] = x_ref[...] + mod_ref[...] * y


def _dense_ffn(x, h, w1, w3, w2, mod):
    n = x.shape[0]
    nt = n // ROW_TILE
    row = lambda i: (i, 0)
    const = lambda i: (0, 0)
    mod_idx = lambda i: (i, 0, 0)
    return pl.pallas_call(
        _ffn_kernel,
        grid=(nt,),
        in_specs=[
            pl.BlockSpec((ROW_TILE, D_MODEL), row),
            pl.BlockSpec((ROW_TILE, D_MODEL), row),
            pl.BlockSpec((D_MODEL, D_FF), const),
            pl.BlockSpec((D_MODEL, D_FF), const),
            pl.BlockSpec((D_FF, D_MODEL), const),
            pl.BlockSpec((None, 1, D_MODEL), mod_idx),
        ],
        out_specs=pl.BlockSpec((ROW_TILE, D_MODEL), row),
        out_shape=jax.ShapeDtypeStruct((n, D_MODEL), F32),
        compiler_params=_cparams(("parallel",)),
        name="dense_ffn",
    )(x, h, w1, w3, w2, mod)


MOE_TM = 256
MOE_TS = 512


def _moe_expert_kernel(rt_ref, e_ref, s_ref, first_ref, last_ref, valid_ref,
                       h_ref, dest_ref, gate_ref, w1_ref, w3_ref, w2_ref, ys_ref, acc_ref, gacc_ref):
    w = pl.program_id(0)

    @pl.when(first_ref[w] == 1)
    def _():
        acc_ref[...] = jnp.zeros_like(acc_ref)
        gacc_ref[...] = jnp.zeros_like(gacc_ref)

    @pl.when(valid_ref[w] == 1)
    def _():
        rows = rt_ref[w] * MOE_TM + lax.broadcasted_iota(jnp.int32, (MOE_TM, MOE_TS), 0)
        hit = dest_ref[...] == rows
        acc_ref[...] += _dot(jnp.where(hit, 1.0, 0.0).astype(BF16), h_ref[...])
        gate = jnp.sum(jnp.where(hit, gate_ref[...], 0.0), axis=-1, keepdims=True)
        gacc_ref[...] += jnp.broadcast_to(gate, gacc_ref.shape)

    @pl.when(last_ref[w] == 1)
    def _():
        h = acc_ref[...].astype(BF16)
        u = _dot(h, w1_ref[...])
        act = (_silu(u) * _dot(h, w3_ref[...])).astype(BF16)
        ys_ref[...] = (gacc_ref[:, 0:1] * _dot(act, w2_ref[...])).astype(BF16)


def _moe_combine_kernel(rt_ref, e_ref, s_ref, first_ref, last_ref, valid_ref,
                        ys_ref, dest_ref, x_ref, mod_ref, o_ref, acc_ref):
    w = pl.program_id(0)

    @pl.when(first_ref[w] == 1)
    def _():
        acc_ref[...] = jnp.zeros_like(acc_ref)

    @pl.when(valid_ref[w] == 1)
    def _():
        d = dest_ref[...]
        lane = lax.broadcasted_iota(jnp.int32, d.shape, 1)
        dcol = jnp.sum(jnp.where(lane == e_ref[w], d, 0.0), axis=-1, keepdims=True)
        cols = (rt_ref[w] * MOE_TM + lax.broadcasted_iota(jnp.int32, (MOE_TS, MOE_TM), 1)).astype(F32)
        onehot = jnp.where(dcol == cols, 1.0, 0.0).astype(BF16)
        acc_ref[...] += _dot(onehot, ys_ref[...])

    @pl.when(last_ref[w] == 1)
    def _():
        for s in range(mod_ref.shape[0]):
            rows = slice(ROW_TILE * s, ROW_TILE * (s + 1))
            o_ref[rows, :] = x_ref[rows, :] + mod_ref[s] * acc_ref[rows, :]


def _moe_plan(gates, n_rt, n_items):
    n = gates.shape[0]
    tm, ts = MOE_TM, MOE_TS
    nb = n // ts
    sel = gates[:, :N_EXPERTS] != 0.0
    si = sel.astype(jnp.int32)
    rank = jnp.cumsum(si, axis=0) - si
    counts = jnp.sum(si, axis=0)
    ntile = (counts + tm - 1) // tm
    tile_end = jnp.cumsum(ntile)
    tile_off = tile_end - ntile
    used = tile_end[-1]
    dest = jnp.where(sel, tile_off[None, :] * tm + rank, -1)
    blk_cnt = si.reshape(nb, ts, N_EXPERTS).sum(axis=1)
    blk_start = jnp.cumsum(blk_cnt, axis=0) - blk_cnt
    first_t = blk_start // tm
    last_t = (blk_start + blk_cnt - 1) // tm
    pair_items = jnp.where(blk_cnt > 0, last_t - first_t + 1, 0)
    first_rt = tile_off[None, :] + first_t

    def items(expert_major):
        if expert_major:
            cnt, frt = pair_items.T.reshape(-1), first_rt.T.reshape(-1)
        else:
            cnt, frt = pair_items.reshape(-1), first_rt.reshape(-1)
        ends = jnp.cumsum(cnt)
        starts = ends - cnt
        total = ends[-1]
        w = jnp.arange(n_items, dtype=jnp.int32)
        wc = jnp.minimum(w, total - 1)
        p = jnp.sum((ends[None, :] <= wc[:, None]).astype(jnp.int32), axis=1)
        rt = (frt[p] + wc - starts[p]).astype(jnp.int32)
        e_w, s_w = (p // nb, p % nb) if expert_major else (p % N_EXPERTS, p // N_EXPERTS)
        valid = w < total
        key = rt if expert_major else s_w
        prev = jnp.concatenate([jnp.full((1,), -1, jnp.int32), key[:-1]])
        nxt = jnp.concatenate([key[1:], jnp.full((1,), -1, jnp.int32)])
        first = valid & (key != prev)
        last = valid & ((key != nxt) | (w == total - 1))
        i32 = lambda v: v.astype(jnp.int32)
        return rt, e_w, s_w, i32(first), i32(last), i32(valid)

    return dest, items(True), items(False)


def _moe_ffn(x, h, gates, w1, w3, w2, mod):
    n = x.shape[0]
    tm, ts = MOE_TM, MOE_TS
    nb = n // ts
    n_rt = 2 * n // tm + N_EXPERTS
    n_items = n_rt + N_EXPERTS * nb
    n_rows = n_rt * tm
    dest, g_items, c_items = _moe_plan(gates, n_rt, n_items)
    dest_rows = dest.T.reshape(N_EXPERTS * nb, 1, ts)
    gate_rows = gates[:, :N_EXPERTS].T.reshape(N_EXPERTS * nb, 1, ts)
    dest_cols = dest.astype(F32)

    pair_row = lambda w, rt, e, s, *_: (e[w] * nb + s[w], 0, 0)
    expert_w = lambda w, rt, e, *_: (e[w], 0, 0)
    ys = pl.pallas_call(
        _moe_expert_kernel,
        grid_spec=pltpu.PrefetchScalarGridSpec(
            num_scalar_prefetch=6,
            grid=(n_items,),
            in_specs=[
                pl.BlockSpec((ts, D_MODEL), lambda w, rt, e, s, *_: (s[w], 0)),
                pl.BlockSpec((None, 1, ts), pair_row),
                pl.BlockSpec((None, 1, ts), pair_row),
                pl.BlockSpec((None, D_MODEL, D_FF), expert_w),
                pl.BlockSpec((None, D_MODEL, D_FF), expert_w),
                pl.BlockSpec((None, D_FF, D_MODEL), expert_w),
            ],
            out_specs=pl.BlockSpec((tm, D_MODEL), lambda w, rt, *_: (rt[w], 0)),
            scratch_shapes=[pltpu.VMEM((tm, D_MODEL), F32), pltpu.VMEM((tm, 128), F32)],
        ),
        out_shape=jax.ShapeDtypeStruct((n_rows, D_MODEL), BF16),
        compiler_params=_cparams(("arbitrary",)),
        name="moe_experts",
    )(*g_items, h, dest_rows, gate_rows, w1, w3, w2)

    return pl.pallas_call(
        _moe_combine_kernel,
        grid_spec=pltpu.PrefetchScalarGridSpec(
            num_scalar_prefetch=6,
            grid=(n_items,),
            in_specs=[
                pl.BlockSpec((tm, D_MODEL), lambda w, rt, *_: (rt[w], 0)),
                pl.BlockSpec((ts, N_EXPERTS), lambda w, rt, e, s, *_: (s[w], 0)),
                pl.BlockSpec((ts, D_MODEL), lambda w, rt, e, s, *_: (s[w], 0)),
                pl.BlockSpec((ts // ROW_TILE, 1, D_MODEL), lambda w, rt, e, s, *_: (s[w], 0, 0)),
            ],
            out_specs=pl.BlockSpec((ts, D_MODEL), lambda w, rt, e, s, *_: (s[w], 0)),
            scratch_shapes=[pltpu.VMEM((ts, D_MODEL), F32)],
        ),
        out_shape=jax.ShapeDtypeStruct((n, D_MODEL), F32),
        compiler_params=_cparams(("arbitrary",)),
        name="moe_combine",
    )(*c_items, ys, dest_cols, x, mod)


def _rope_tables(n_ctx, n_lat):
    pos_r = jnp.arange(n_lat, dtype=jnp.int32) // GRID_W
    pos_c = jnp.arange(n_lat, dtype=jnp.int32) % GRID_W

    def per_head(head_dim):
        nf = head_dim // 4
        inv = ROPE_BASE ** (-jnp.arange(nf, dtype=F32) / nf)
        ang_r = pos_r.astype(F32)[:, None] * inv[None, :]
        ang_c = pos_c.astype(F32)[:, None] * inv[None, :]
        z = jnp.zeros_like(ang_r)
        cos = jnp.concatenate([jnp.cos(ang_r)] * 2 + [jnp.cos(ang_c)] * 2, axis=-1)
        sa = jnp.concatenate([-jnp.sin(ang_r), z, -jnp.sin(ang_c), z], axis=-1)
        sb = jnp.concatenate([z, jnp.sin(ang_r), z, jnp.sin(ang_c)], axis=-1)
        reps = 256 // head_dim
        tabs = [jnp.tile(t, (1, reps)) for t in (cos, sa, sb)]
        ctx = [jnp.ones((n_ctx, 256), F32), jnp.zeros((n_ctx, 256), F32), jnp.zeros((n_ctx, 256), F32)]
        return [jnp.concatenate([c, t], axis=0) for c, t in zip(ctx, tabs)]

    return jnp.concatenate(per_head(DIFF_HEAD_DIM) + per_head(SWA_HEAD_DIM), axis=-1)


def _block_diag_mean(group):
    idx = jnp.arange(256) // group
    return jnp.where(idx[:, None] == idx[None, :], 1.0 / group, 0.0).astype(BF16)


def kernel(x, c, ctx, c_ctx, ada_w, ada_b, norm_mix_g, norm_ffn_g, w_in, w_out, diff_qk_norm_g, diff_lambda,
           diff_subln_g, hgrn_lb_logits, hgrn_norm_g, swa_qk_norm_g, swa_sink, ffn_w1, ffn_w3, ffn_w2,
           moe_router, moe_w1, moe_w3, moe_w2):
    n_batch, n_lat, _ = x.shape
    n_ctx = ctx.shape[1]
    depth = ada_w.shape[0]
    t_len = n_ctx + n_lat
    tps = t_len // ROW_TILE
    assert n_ctx == ROW_TILE and n_lat % ROW_TILE == 0 and n_lat >= ROW_TILE + 2 * WINDOW
    assert (n_batch * t_len) % MOE_TS == 0

    xc = jnp.concatenate([ctx, x], axis=1).reshape(n_batch * t_len, D_MODEL)

    n_rows = -(-(n_batch + 1) // 8) * 8
    cond = jnp.concatenate([c, c_ctx[None, :], jnp.zeros((n_rows - n_batch - 1, D_MODEL), F32)], axis=0)
    mods = _ada_modulation(cond, ada_w, ada_b).reshape(depth, n_rows, 6, D_MODEL)
    m_lat = jnp.broadcast_to(mods[:, :n_batch, None], (depth, n_batch, tps - 1, 6, D_MODEL))
    m_ctx = jnp.broadcast_to(mods[:, n_batch, None, None], (depth, n_batch, 1, 6, D_MODEL))
    mods = jnp.concatenate([m_ctx, m_lat], axis=2).reshape(depth, n_batch * tps, 6, D_MODEL)

    lb = jnp.cumsum(jax.nn.softmax(hgrn_lb_logits.astype(F32), axis=1), axis=1)
    lb = lb - lb[:, :1]
    rope = _rope_tables(n_ctx, n_lat)
    g32 = _block_diag_mean(DIFF_HEAD_DIM)
    g64 = _block_diag_mean(SWA_HEAD_DIM)

    perm_q = jnp.arange(SWA_WIDTH).reshape(SWA_KV_HEADS, SWA_GROUP, SWA_HEAD_DIM).transpose(1, 0, 2).reshape(-1)
    qc0 = 3 * DIFF_WIDTH + 5 * HGRN_WIDTH
    col_perm = jnp.concatenate([jnp.arange(qc0), qc0 + perm_q, jnp.arange(qc0 + SWA_WIDTH, D_IN)])
    oc0 = DIFF_WIDTH + HGRN_WIDTH
    row_perm = jnp.concatenate([jnp.arange(oc0), oc0 + perm_q])

    for layer in range(depth):
        lam_init = 0.8 - 0.6 * math.exp(-0.3 * layer)
        mod = mods[layer]
        w_in_l = w_in[layer][:, col_perm].astype(BF16)
        w_out_l = w_out[layer][row_perm, :].astype(BF16)
        gains = jnp.stack([
            jnp.tile(diff_qk_norm_g[layer, 0], 8) * (DIFF_HEAD_DIM ** -0.5 * LOG2_E),
            jnp.tile(diff_qk_norm_g[layer, 1], 8),
            jnp.tile(swa_qk_norm_g[layer, 0], 4) * (SWA_HEAD_DIM ** -0.5 * LOG2_E),
            jnp.tile(swa_qk_norm_g[layer, 1], 4),
        ]).astype(F32)
        lbt = jnp.stack([
            jnp.log(lb[:, layer]).reshape(-1),
            jnp.log1p(-lb[:, layer]).reshape(-1),
        ]).astype(F32)
        lv = diff_lambda[layer].astype(F32)
        lam = jnp.exp(jnp.sum(lv[0] * lv[1])) - jnp.exp(jnp.sum(lv[2] * lv[3])) + lam_init
        lam_row = jnp.full((1, 128), lam, F32)
        post_row = (jnp.tile(diff_subln_g[layer], DIFF_HEADS) * (1.0 - lam_init)).reshape(1, DIFF_WIDTH).astype(F32)
        sink_tab = jnp.broadcast_to(swa_sink[layer].astype(F32)[:, None] * LOG2_E, (SWA_Q_HEADS, 128))

        (qa, ka, va, hq, hv, lff, lfb, hg, qc, kc, vc) = _in_projection(
            xc, mod[:, 0:2], norm_mix_g[layer].reshape(1, D_MODEL), w_in_l, rope, gains, lbt, g32, g64, tps)
        a = _diff_attention(qa, ka, va, lam_row, post_row, g64, n_batch, t_len, n_ctx)
        b = _hgrn(hq, hv, lff, lfb, hg, hgrn_norm_g[layer].reshape(1, HGRN_K), n_batch, t_len, n_ctx)
        cc = _swa_attention(qc, kc, vc, sink_tab, n_batch, t_len, n_ctx)

        is_moe = layer % 2 == 1
        jj = layer // 2
        router = None
        if is_moe:
            router = jnp.pad(moe_router[jj].astype(F32), ((0, 0), (0, 128 - N_EXPERTS)))
        outs = _out_projection(xc, a, b, cc, w_out_l, mod[:, 2:5], norm_ffn_g[layer].reshape(1, D_MODEL),
                               router)
        if is_moe:
            x_mid, h2, gates = outs
            if layer == depth - 1:
                is_lat = (jnp.arange(n_batch * t_len, dtype=jnp.int32) % t_len) >= n_ctx
                gates = jnp.where(is_lat[:, None], gates, 0.0)
            xc = _moe_ffn(x_mid, h2, gates, moe_w1[jj].astype(BF16), moe_w3[jj].astype(BF16),
                          moe_w2[jj].astype(BF16), mod[:, 5:6])
        else:
            x_mid, h2 = outs
            xc = _dense_ffn(x_mid, h2, ffn_w1[jj].astype(BF16), ffn_w3[jj].astype(BF16),
                            ffn_w2[jj].astype(BF16), mod[:, 5:6])

    return xc.reshape(n_batch, t_len, D_MODEL)[:, n_ctx:, :]
```

```python
import functools
import math

import jax
import jax.numpy as jnp
from jax import lax
from jax.experimental import pallas as pl
from jax.experimental.pallas import tpu as pltpu

D_MODEL = 1024
GRID_W = 64
DIFF_HEADS = 4
DIFF_HEAD_DIM = 32
DIFF_V_DIM = 64
DIFF_WIDTH = 256
HGRN_HEADS = 4
HGRN_K = 128
HGRN_WIDTH = 512
SWA_Q_HEADS = 4
SWA_KV_HEADS = 2
SWA_GROUP = 2
SWA_HEAD_DIM = 64
SWA_WIDTH = 256
SWA_KV_WIDTH = 128
WINDOW = 128
D_FF = 2816
N_EXPERTS = 8
ROPE_BASE = 10000.0
NORM_EPS = 1e-6
D_IN = 3840

F32 = jnp.float32
BF16 = jnp.bfloat16

ROW_TILE = 256
HGRN_CHUNK = 64
HGRN_UNROLL = 4
NEG_BIG = -1e30
LOG2_E = 1.4426950408889634
VMEM_LIMIT = 56 * 1024 * 1024


def _cparams(sem):
    return pltpu.CompilerParams(dimension_semantics=sem, vmem_limit_bytes=VMEM_LIMIT)


def _split_bf16(v):
    hi = v.astype(BF16)
    lo = (v - hi.astype(F32)).astype(BF16)
    return hi, lo


def _dot(a, b):
    return jnp.dot(a, b, preferred_element_type=F32)


def _dot_nt(a, b):
    return lax.dot_general(a, b, (((1,), (1,)), ((), ())), preferred_element_type=F32)


def _dot_tn(a, b):
    return lax.dot_general(a, b, (((0,), (0,)), ((), ())), preferred_element_type=F32)


def _group_mean_sq(y, gmat):
    hi, lo = _split_bf16(y * y)
    return _dot(hi, gmat) + _dot(lo, gmat)


def _silu(v):
    return v * (1.0 / (1.0 + jnp.exp(-v)))


def _ada_kernel(s_ref, w_ref, b_ref, o_ref):
    s = s_ref[...]
    s = _silu(s)
    s_hi, s_lo = _split_bf16(s)
    w_hi, w_lo = _split_bf16(w_ref[...])
    o_ref[...] = _dot(s_hi, w_hi) + _dot(s_lo, w_hi) + _dot(s_hi, w_lo) + b_ref[...]


def _ada_modulation(cond, ada_w, ada_b):
    depth = ada_w.shape[0]
    r = cond.shape[0]
    nblk = 6 * D_MODEL // 1024
    return pl.pallas_call(
        _ada_kernel,
        grid=(depth, nblk),
        in_specs=[
            pl.BlockSpec((r, D_MODEL), lambda l, n: (0, 0)),
            pl.BlockSpec((None, D_MODEL, 1024), lambda l, n: (l, 0, n)),
            pl.BlockSpec((None, 1, 1024), lambda l, n: (l, 0, n)),
        ],
        out_specs=pl.BlockSpec((None, r, 1024), lambda l, n: (l, 0, n)),
        out_shape=jax.ShapeDtypeStruct((depth, r, 6 * D_MODEL), F32),
        compiler_params=_cparams(("parallel", "parallel")),
        name="ada_modulation",
    )(cond, ada_w, ada_b.reshape(depth, 1, 6 * D_MODEL))


def _rope(y, cos, sa, sb, half):
    w = y.shape[-1]
    fwd = pltpu.roll(y, w - half, 1)
    bwd = pltpu.roll(y, half, 1)
    return y * cos + fwd * sa + bwd * sb


def _inproj_kernel(x_ref, mod_ref, g_ref, w_ref, rope_ref, gains_ref, lbt_ref, ga_ref, gc_ref,
                   qa_ref, ka_ref, va_ref, hq_ref, hv_ref, lff_ref, lfb_ref, hg_ref,
                   qc_ref, kc_ref, vc_ref):
    x = x_ref[...]
    shift = mod_ref[0:1, :]
    scale = mod_ref[1:2, :]
    ms = jnp.mean(x * x, axis=-1, keepdims=True)
    h = (x * lax.rsqrt(ms + NORM_EPS) * g_ref[...]) * (1.0 + scale) + shift
    hb = h.astype(BF16)

    def proj(a, b):
        return _dot(hb, w_ref[:, a:b])

    def qk_prep(y, gmat, gain, cos, sa, sb, half):
        msq = _group_mean_sq(y, gmat)
        y = y * lax.rsqrt(msq + NORM_EPS) * gain
        return _rope(y, cos, sa, sb, half)

    ga = ga_ref[...]
    gc = gc_ref[...]
    ra = [rope_ref[:, 256 * i:256 * (i + 1)] for i in range(6)]
    qa = qk_prep(proj(0, 256), ga, gains_ref[0:1, :], ra[0], ra[1], ra[2], 8)
    qa_ref[...] = qa.astype(BF16)
    ka = qk_prep(proj(256, 512), ga, gains_ref[1:2, :], ra[0], ra[1], ra[2], 8)
    ka_ref[...] = ka.astype(BF16)
    va_ref[...] = proj(512, 768).astype(BF16)
    hq_ref[...] = _silu(proj(768, 1280)).astype(BF16)
    hv_ref[...] = proj(1280, 1792).astype(BF16)
    for d, lf_ref in enumerate((lff_ref, lfb_ref)):
        z = proj(1792 + 512 * d, 2304 + 512 * d)
        log_lb = lbt_ref[0:1, 512 * d:512 * (d + 1)]
        log1m_lb = lbt_ref[1:2, 512 * d:512 * (d + 1)]
        sp = jnp.maximum(-z, 0.0) + jnp.log(1.0 + jnp.exp(-jnp.abs(z)))
        b2 = log1m_lb - sp
        mx = jnp.maximum(log_lb, b2)
        lf_ref[...] = mx + jnp.log(1.0 + jnp.exp(-jnp.abs(log_lb - b2)))
    hg_ref[...] = _silu(proj(2816, 3328)).astype(BF16)
    qc = qk_prep(proj(3328, 3584), gc, gains_ref[2:3, :], ra[3], ra[4], ra[5], 16)
    qc_ref[...] = qc.astype(BF16)
    kc = qk_prep(proj(3584, 3712), gc[0:128, 0:128], gains_ref[3:4, 0:128],
                 ra[3][:, 0:128], ra[4][:, 0:128], ra[5][:, 0:128], 16)
    kc_ref[...] = kc.astype(BF16)
    vc_ref[...] = proj(3712, 3840).astype(BF16)


def _in_projection(xc, mod, g, w, rope, gains, lbt, ga, gc, tiles_per_seq):
    n = xc.shape[0]
    nt = n // ROW_TILE
    tps = tiles_per_seq

    def row(i):
        return (i, 0)

    def mod_idx(i):
        return (i, 0, 0)

    const = lambda i: (0, 0)
    widths = [(256, BF16), (256, BF16), (256, BF16), (512, BF16), (512, BF16),
              (512, F32), (512, F32), (512, BF16), (256, BF16), (128, BF16), (128, BF16)]
    return pl.pallas_call(
        _inproj_kernel,
        grid=(nt,),
        in_specs=[
            pl.BlockSpec((ROW_TILE, D_MODEL), row),
            pl.BlockSpec((None, 2, D_MODEL), mod_idx),
            pl.BlockSpec((1, D_MODEL), const),
            pl.BlockSpec((D_MODEL, D_IN), const),
            pl.BlockSpec((ROW_TILE, 6 * 256), lambda i: (i % tps, 0)),
            pl.BlockSpec((4, 256), const),
            pl.BlockSpec((2, 1024), const),
            pl.BlockSpec((256, 256), const),
            pl.BlockSpec((256, 256), const),
        ],
        out_specs=[pl.BlockSpec((ROW_TILE, wd), row) for wd, _ in widths],
        out_shape=[jax.ShapeDtypeStruct((n, wd), dt) for wd, dt in widths],
        compiler_params=_cparams(("parallel",)),
        name="in_projection",
    )(xc, mod, g, w, rope, gains, lbt, ga, gc)


def _diff_attend(q, k, v, lam):
    lane = lax.broadcasted_iota(jnp.int32, q.shape, 1)
    n_maps = 2 * DIFF_HEADS

    def scores(g):
        return _dot_nt(jnp.where(lane // DIFF_HEAD_DIM == g, q, jnp.zeros_like(q)), k)

    out = jnp.zeros(q.shape, F32)
    parts = []
    s_next = scores(0)
    for g in range(n_maps):
        s = s_next
        if g + 1 < n_maps:
            s_next = scores(g + 1)
        e = jnp.exp2(s - jnp.max(s, axis=-1, keepdims=True)).astype(BF16)
        o = _dot(e, v[g // 2])
        parts.append(o * (1.0 / pltpu.roll(o, DIFF_WIDTH - DIFF_V_DIM, 1)))
        if g % 2 == 1:
            out = out + jnp.where(lane // DIFF_V_DIM == g // 2, parts[g - 1] - lam * parts[g], 0.0)
    return out


def _diff_kernel(q_ref, k_ref, v_ref, lam_ref, post_ref, g64_ref, o_ref, vaug_ref, *, n_ctx):
    j = pl.program_id(1)
    lam = lam_ref[0:1, 0:1]

    @pl.when(j == 0)
    def _():
        v = v_ref[...]
        lane = lax.broadcasted_iota(jnp.int32, v.shape, 1)
        for hd in range(DIFF_HEADS):
            vaug_ref[hd] = jnp.where(lane // DIFF_V_DIM == (hd + 1) % DIFF_HEADS, jnp.ones_like(v), v)

    def finish(o):
        msq = _group_mean_sq(o, g64_ref[...])
        o_ref[...] = (o * lax.rsqrt(msq + NORM_EPS) * post_ref[...]).astype(BF16)

    @pl.when(j == 0)
    def _():
        finish(_diff_attend(q_ref[...], k_ref[0:n_ctx, :], [vaug_ref[hd, 0:n_ctx, :] for hd in range(DIFF_HEADS)],
                            lam))

    @pl.when(j > 0)
    def _():
        finish(_diff_attend(q_ref[...], k_ref[...], [vaug_ref[hd] for hd in range(DIFF_HEADS)], lam))


def _diff_attention(qa, ka, va, lam_row, post_row, g64, n_batch, t_len, n_ctx):
    tps = t_len // ROW_TILE
    q3 = qa.reshape(n_batch, t_len, DIFF_WIDTH)
    k3 = ka.reshape(n_batch, t_len, DIFF_WIDTH)
    v3 = va.reshape(n_batch, t_len, DIFF_WIDTH)
    const = lambda b, j: (0, 0)
    out = pl.pallas_call(
        functools.partial(_diff_kernel, n_ctx=n_ctx),
        grid=(n_batch, tps),
        in_specs=[
            pl.BlockSpec((None, ROW_TILE, DIFF_WIDTH), lambda b, j: (b, j, 0)),
            pl.BlockSpec((None, t_len, DIFF_WIDTH), lambda b, j: (b, 0, 0)),
            pl.BlockSpec((None, t_len, DIFF_WIDTH), lambda b, j: (b, 0, 0)),
            pl.BlockSpec((1, 128), const),
            pl.BlockSpec((1, DIFF_WIDTH), const),
            pl.BlockSpec((256, 256), const),
        ],
        out_specs=pl.BlockSpec((None, ROW_TILE, DIFF_WIDTH), lambda b, j: (b, j, 0)),
        out_shape=jax.ShapeDtypeStruct((n_batch, t_len, DIFF_WIDTH), BF16),
        scratch_shapes=[pltpu.VMEM((DIFF_HEADS, t_len, DIFF_WIDTH), BF16)],
        compiler_params=_cparams(("parallel", "arbitrary")),
        name="diff_attention",
    )(q3, k3, v3, lam_row, post_row, g64)
    return out.reshape(n_batch * t_len, DIFF_WIDTH)


def _swa_kernel(q_ref, k_ref, v_ref, sink_ref, o_ref, *, n_ctx, n_lat):
    j = pl.program_id(1)
    q = q_ref[...]
    tq = q.shape[0]
    lane = lax.broadcasted_iota(jnp.int32, (tq, SWA_KV_WIDTH), 1)
    span = ROW_TILE + 2 * WINDOW
    heads = [(g, hd) for g in range(SWA_GROUP) for hd in range(SWA_KV_HEADS)]

    def run(k, v, valid):
        vlane = lax.broadcasted_iota(jnp.int32, v.shape, 1)
        vaug = [jnp.where(vlane // SWA_HEAD_DIM == hd, v, jnp.ones_like(v)) for hd in range(SWA_KV_HEADS)]
        sinks = [sink_ref[SWA_GROUP * hd + g:SWA_GROUP * hd + g + 1, 0:1] for g, hd in heads]
        scores = []
        for g, hd in heads:
            qg = q[:, SWA_KV_WIDTH * g:SWA_KV_WIDTH * (g + 1)]
            s = _dot_nt(jnp.where(lane // SWA_HEAD_DIM == hd, qg, jnp.zeros_like(qg)), k)
            scores.append(s if valid is None else jnp.where(valid, s, NEG_BIG))
        maxes = [jnp.maximum(jnp.max(s, axis=-1, keepdims=True), sink) for s, sink in zip(scores, sinks)]
        outs = [_dot(jnp.exp2(s - mx).astype(BF16), vaug[hd]) for s, mx, (g, hd) in zip(scores, maxes, heads)]
        for g in range(SWA_GROUP):
            out = jnp.zeros((tq, SWA_KV_WIDTH), F32)
            for i, (gi, hd) in enumerate(heads):
                if gi == g:
                    den = pltpu.roll(outs[i], SWA_HEAD_DIM, 1) + jnp.exp2(sinks[i] - maxes[i])
                    out = out + jnp.where(lane // SWA_HEAD_DIM == hd, outs[i] * (1.0 / den), 0.0)
            o_ref[:, SWA_KV_WIDTH * g:SWA_KV_WIDTH * (g + 1)] = out.astype(BF16)

    @pl.when(j == 0)
    def _():
        run(k_ref[0:n_ctx, :], v_ref[0:n_ctx, :], None)

    @pl.when(j > 0)
    def _():
        q0 = (j - 1) * ROW_TILE
        ks = jnp.clip(q0 - WINDOW, 0, n_lat - span)
        ks = pl.multiple_of(ks, WINDOW)
        k = jnp.concatenate([k_ref[0:n_ctx, :], k_ref[pl.ds(n_ctx + ks, span), :]], axis=0)
        v = jnp.concatenate([v_ref[0:n_ctx, :], v_ref[pl.ds(n_ctx + ks, span), :]], axis=0)
        qpos = q0 + lax.broadcasted_iota(jnp.int32, (tq, n_ctx + span), 0)
        kpos = ks - n_ctx + lax.broadcasted_iota(jnp.int32, (tq, n_ctx + span), 1)
        run(k, v, (kpos < ks) | (jnp.abs(qpos - kpos) <= WINDOW))


def _swa_attention(qc, kc, vc, sink_tab, n_batch, t_len, n_ctx):
    tps = t_len // ROW_TILE
    q3 = qc.reshape(n_batch, t_len, SWA_WIDTH)
    k3 = kc.reshape(n_batch, t_len, SWA_KV_WIDTH)
    v3 = vc.reshape(n_batch, t_len, SWA_KV_WIDTH)
    out = pl.pallas_call(
        functools.partial(_swa_kernel, n_ctx=n_ctx, n_lat=t_len - n_ctx),
        grid=(n_batch, tps),
        in_specs=[
            pl.BlockSpec((None, ROW_TILE, SWA_WIDTH), lambda b, j: (b, j, 0)),
            pl.BlockSpec((None, t_len, SWA_KV_WIDTH), lambda b, j: (b, 0, 0)),
            pl.BlockSpec((None, t_len, SWA_KV_WIDTH), lambda b, j: (b, 0, 0)),
            pl.BlockSpec((SWA_Q_HEADS, 128), lambda b, j: (0, 0)),
        ],
        out_specs=pl.BlockSpec((None, ROW_TILE, SWA_WIDTH), lambda b, j: (b, j, 0)),
        out_shape=jax.ShapeDtypeStruct((n_batch, t_len, SWA_WIDTH), BF16),
        compiler_params=_cparams(("parallel", "arbitrary")),
        name="swa_attention",
    )(q3, k3, v3, sink_tab)
    return out.reshape(n_batch * t_len, SWA_WIDTH)


def _hgrn_kernel(q_ref, v_ref, lff_ref, lfb_ref, g_ref, ng_ref, o_ref, of_ref, ob_ref, st_ref, *, n_ctx, t_len):
    c = HGRN_CHUNK
    n_chunks = t_len // c
    n_ctx_chunks = n_ctx // c
    mid = c // 2
    ri = lax.broadcasted_iota(jnp.int32, (c, c), 0)
    ci = lax.broadcasted_iota(jnp.int32, (c, c), 1)
    causal = ri >= ci
    anti = ri <= ci
    tri_f = jnp.where(causal, 1.0, 0.0).astype(BF16)
    tri_b = jnp.where(anti, 1.0, 0.0).astype(BF16)

    st_ref[...] = jnp.zeros_like(st_ref)

    def body(i, _):
        chains = []
        for u in range(HGRN_UNROLL):
            step = i * HGRN_UNROLL + u
            rf = pl.multiple_of(step * c, c)
            cb = jnp.where(step < n_ctx_chunks, n_ctx_chunks - 1 - step, n_chunks - 1 + n_ctx_chunks - step)
            rb = pl.multiple_of(cb * c, c)
            for hd in range(HGRN_HEADS):
                cols = slice(HGRN_K * hd, HGRN_K * (hd + 1))
                chains.append((2 * hd, pl.ds(rf, c), cols, lff_ref, of_ref, tri_f, causal, mid - 1, c - 1))
                chains.append((2 * hd + 1, pl.ds(rb, c), cols, lfb_ref, ob_ref, tri_b, anti, mid, 0))
        cums = []
        for _, rows, cols, lf_ref, _, tri, _, _, _ in chains:
            lf_hi, lf_lo = _split_bf16(lf_ref[rows, cols])
            cums.append(_dot(tri, lf_hi) + _dot(tri, lf_lo))
        prods = []
        for (slot, rows, cols, lf_ref, _, _, mask, ref_row, end_row), cum in zip(chains, cums):
            ref = cum[ref_row:ref_row + 1, :]
            tot = cum[end_row:end_row + 1, :]
            v = v_ref[rows, cols]
            k = 1.0 - jnp.exp(lf_ref[rows, cols])
            qt = q_ref[rows, cols].astype(F32) * jnp.exp(cum - ref)
            kt = k * jnp.exp(ref - cum)
            sc = _dot_nt(qt.astype(BF16), kt.astype(BF16))
            upd = _dot_tn(v, (kt * jnp.exp(tot - ref)).astype(BF16))
            prods.append((sc, (qt * jnp.exp(ref)).astype(BF16), jnp.exp(tot), upd, v))
        states = {}
        for (slot, rows, cols, _, o_ref_d, _, mask, _, _), (sc, q_in, decay, upd, v) in zip(chains, prods):
            st = states[slot] if slot in states else st_ref[slot]
            o = _dot(jnp.where(mask, sc, 0.0).astype(BF16), v) + _dot_nt(q_in, st.astype(BF16))
            o_ref_d[rows, cols] = o.astype(BF16)
            states[slot] = st * decay + upd
        for slot, st in states.items():
            st_ref[slot] = st
        return 0

    lax.fori_loop(0, n_chunks // HGRN_UNROLL, body, 0)

    def fin(t, _):
        r = pl.multiple_of(t * ROW_TILE, ROW_TILE)
        for hd in range(HGRN_HEADS):
            cols = slice(HGRN_K * hd, HGRN_K * (hd + 1))
            o = of_ref[pl.ds(r, ROW_TILE), cols].astype(F32) + ob_ref[pl.ds(r, ROW_TILE), cols].astype(F32)
            ms = jnp.mean(o * o, axis=-1, keepdims=True)
            o = o * lax.rsqrt(ms + NORM_EPS) * ng_ref[...]
            o_ref[pl.ds(r, ROW_TILE), cols] = (o * g_ref[pl.ds(r, ROW_TILE), cols].astype(F32)).astype(BF16)
        return 0

    lax.fori_loop(0, t_len // ROW_TILE, fin, 0)


def _hgrn(hq, hv, lff, lfb, hg, norm_g, n_batch, t_len, n_ctx):
    def view(a):
        return a.reshape(n_batch, t_len, HGRN_WIDTH)

    blk = pl.BlockSpec((None, t_len, HGRN_WIDTH), lambda b: (b, 0, 0))
    out = pl.pallas_call(
        functools.partial(_hgrn_kernel, n_ctx=n_ctx, t_len=t_len),
        grid=(n_batch,),
        in_specs=[blk] * 5 + [pl.BlockSpec((1, HGRN_K), lambda b: (0, 0))],
        out_specs=blk,
        out_shape=jax.ShapeDtypeStruct((n_batch, t_len, HGRN_WIDTH), BF16),
        scratch_shapes=[pltpu.VMEM((t_len, HGRN_WIDTH), BF16), pltpu.VMEM((t_len, HGRN_WIDTH), BF16),
                        pltpu.VMEM((2 * HGRN_HEADS, HGRN_K, HGRN_K), F32)],
        compiler_params=_cparams(("parallel",)),
        name="hgrn2",
    )(view(hq), view(hv), view(lff), view(lfb), view(hg), norm_g)
    return out.reshape(n_batch * t_len, HGRN_WIDTH)


def _mix_residual_norm(x_ref, a_ref, b_ref, c_ref, w_ref, mod_ref, g_ref):
    y = (_dot(a_ref[...], w_ref[0:256, :]) + _dot(b_ref[...], w_ref[256:768, :])
         + _dot(c_ref[...], w_ref[768:1024, :]))
    x = x_ref[...] + mod_ref[0:1, :] * y
    ms = jnp.mean(x * x, axis=-1, keepdims=True)
    h = (x * lax.rsqrt(ms + NORM_EPS) * g_ref[...]) * (1.0 + mod_ref[2:3, :]) + mod_ref[1:2, :]
    return x, h


def _outproj_router_kernel(x_ref, a_ref, b_ref, c_ref, w_ref, mod_ref, g_ref, r_ref, xo_ref, h_ref, gate_ref):
    x, h = _mix_residual_norm(x_ref, a_ref, b_ref, c_ref, w_ref, mod_ref, g_ref)
    xo_ref[...] = x
    h_ref[...] = h.astype(BF16)
    h_hi, h_lo = _split_bf16(h)
    r_hi, r_lo = _split_bf16(r_ref[...])
    logits = _dot(h_hi, r_hi) + _dot(h_lo, r_hi) + _dot(h_hi, r_lo)
    lane = lax.broadcasted_iota(jnp.int32, logits.shape, 1).astype(F32)
    logits = jnp.where(lane < N_EXPERTS, logits, NEG_BIG)
    v1 = jnp.max(logits, axis=-1, keepdims=True)
    i1 = jnp.min(jnp.where(logits == v1, lane, 128.0), axis=-1, keepdims=True)
    rest_l = jnp.where(lane == i1, NEG_BIG, logits)
    v2 = jnp.max(rest_l, axis=-1, keepdims=True)
    i2 = jnp.min(jnp.where(rest_l == v2, lane, 128.0), axis=-1, keepdims=True)
    e2 = jnp.exp(v2 - v1)
    w1 = 1.0 / (1.0 + e2)
    w2 = e2 * w1
    gate_ref[...] = jnp.where(lane == i1, w1, 0.0) + jnp.where(lane == i2, w2, 0.0)


def _out_projection_router(xc, a, b, c, w, mod, g, router):
    n = xc.shape[0]
    nt = n // ROW_TILE
    row = lambda i: (i, 0)
    const = lambda i: (0, 0)
    return pl.pallas_call(
        _outproj_router_kernel,
        grid=(nt,),
        in_specs=[
            pl.BlockSpec((ROW_TILE, D_MODEL), row),
            pl.BlockSpec((ROW_TILE, DIFF_WIDTH), row),
            pl.BlockSpec((ROW_TILE, HGRN_WIDTH), row),
            pl.BlockSpec((ROW_TILE, SWA_WIDTH), row),
            pl.BlockSpec((D_MODEL, D_MODEL), const),
            pl.BlockSpec((None, 3, D_MODEL), lambda i: (i, 0, 0)),
            pl.BlockSpec((1, D_MODEL), const),
            pl.BlockSpec((D_MODEL, 128), const),
        ],
        out_specs=[pl.BlockSpec((ROW_TILE, D_MODEL), row), pl.BlockSpec((ROW_TILE, D_MODEL), row),
                   pl.BlockSpec((ROW_TILE, 128), row)],
        out_shape=[jax.ShapeDtypeStruct((n, D_MODEL), F32), jax.ShapeDtypeStruct((n, D_MODEL), BF16),
                   jax.ShapeDtypeStruct((n, 128), F32)],
        compiler_params=_cparams(("parallel",)),
        name="out_projection",
    )(xc, a, b, c, w, mod, g, router)


def _outproj_ffn_kernel(x_ref, a_ref, b_ref, c_ref, wo_ref, mod_ref, g_ref, w1_ref, w3_ref, w2_ref, o_ref):
    x, h = _mix_residual_norm(x_ref, a_ref, b_ref, c_ref, wo_ref, mod_ref, g_ref)
    h = h.astype(BF16)
    u = _dot(h, w1_ref[...])
    act = (_silu(u) * _dot(h, w3_ref[...])).astype(BF16)
    o_ref[...] = x + mod_ref[3:4, :] * _dot(act, w2_ref[...])


def _outproj_dense_ffn(xc, a, b, c, wo, mod, g, w1, w3, w2):
    n = xc.shape[0]
    nt = n // ROW_TILE
    row = lambda i: (i, 0)
    const = lambda i: (0, 0)
    return pl.pallas_call(
        _outproj_ffn_kernel,
        grid=(nt,),
        in_specs=[
            pl.BlockSpec((ROW_TILE, D_MODEL), row),
            pl.BlockSpec((ROW_TILE, DIFF_WIDTH), row),
            pl.BlockSpec((ROW_TILE, HGRN_WIDTH), row),
            pl.BlockSpec((ROW_TILE, SWA_WIDTH), row),
            pl.BlockSpec((D_MODEL, D_MODEL), const),
            pl.BlockSpec((None, 4, D_MODEL), lambda i: (i, 0, 0)),
            pl.BlockSpec((1, D_MODEL), const),
            pl.BlockSpec((D_MODEL, D_FF), const),
            pl.BlockSpec((D_MODEL, D_FF), const),
            pl.BlockSpec((D_FF, D_MODEL), const),
        ],
        out_specs=pl.BlockSpec((ROW_TILE, D_MODEL), row),
        out_shape=jax.ShapeDtypeStruct((n, D_MODEL), F32),
        compiler_params=_cparams(("parallel",)),
        name="outproj_dense_ffn",
    )(xc, a, b, c, wo, mod, g, w1, w3, w2)


MOE_TM = 256
MOE_GATHER_TS = 1024
MOE_COMBINE_TS = 512


def _moe_expert_kernel(rt_ref, e_ref, s_ref, first_ref, last_ref, valid_ref,
                       h_ref, dest_ref, gate_ref, w1_ref, w3_ref, w2_ref, ys_ref, acc_ref, gacc_ref):
    w = pl.program_id(0)

    @pl.when(first_ref[w] == 1)
    def _():
        acc_ref[...] = jnp.zeros_like(acc_ref)
        gacc_ref[...] = jnp.zeros_like(gacc_ref)

    @pl.when(valid_ref[w] == 1)
    def _():
        rows = rt_ref[w] * MOE_TM + lax.broadcasted_iota(jnp.int32, (MOE_TM, h_ref.shape[0]), 0)
        hit = dest_ref[...] == rows
        acc_ref[...] += _dot(jnp.where(hit, 1.0, 0.0).astype(BF16), h_ref[...])
        gate = jnp.sum(jnp.where(hit, gate_ref[...], 0.0), axis=-1, keepdims=True)
        gacc_ref[...] += jnp.broadcast_to(gate, gacc_ref.shape)

    @pl.when(last_ref[w] == 1)
    def _():
        h = acc_ref[...].astype(BF16)
        u = _dot(h, w1_ref[...])
        act = (_silu(u) * _dot(h, w3_ref[...])).astype(BF16)
        ys_ref[...] = (gacc_ref[:, 0:1] * _dot(act, w2_ref[...])).astype(BF16)


def _moe_combine_kernel(rt_ref, e_ref, s_ref, first_ref, last_ref, valid_ref,
                        ys_ref, dest_ref, x_ref, mod_ref, o_ref, acc_ref):
    w = pl.program_id(0)

    @pl.when(first_ref[w] == 1)
    def _():
        acc_ref[...] = jnp.zeros_like(acc_ref)

    @pl.when(valid_ref[w] == 1)
    def _():
        d = dest_ref[...]
        lane = lax.broadcasted_iota(jnp.int32, d.shape, 1)
        dcol = jnp.sum(jnp.where(lane == e_ref[w], d, 0.0), axis=-1, keepdims=True)
        cols = (rt_ref[w] * MOE_TM + lax.broadcasted_iota(jnp.int32, (d.shape[0], MOE_TM), 1)).astype(F32)
        onehot = jnp.where(dcol == cols, 1.0, 0.0).astype(BF16)
        acc_ref[...] += _dot(onehot, ys_ref[...])

    @pl.when(last_ref[w] == 1)
    def _():
        for s in range(mod_ref.shape[0]):
            rows = slice(ROW_TILE * s, ROW_TILE * (s + 1))
            o_ref[rows, :] = x_ref[rows, :] + mod_ref[s] * acc_ref[rows, :]


def _moe_plan(gates, n_rt):
    n = gates.shape[0]
    tm = MOE_TM
    sel = gates[:, :N_EXPERTS] != 0.0
    si = sel.astype(jnp.int32)
    rank = jnp.cumsum(si, axis=0) - si
    counts = jnp.sum(si, axis=0)
    ntile = (counts + tm - 1) // tm
    tile_off = jnp.cumsum(ntile) - ntile
    dest = jnp.where(sel, tile_off[None, :] * tm + rank, -1)

    def items(expert_major, ts):
        nb = n // ts
        n_items = n_rt + N_EXPERTS * nb
        blk_cnt = si.reshape(nb, ts, N_EXPERTS).sum(axis=1)
        blk_start = jnp.cumsum(blk_cnt, axis=0) - blk_cnt
        first_t = blk_start // tm
        last_t = (blk_start + blk_cnt - 1) // tm
        pair_items = jnp.where(blk_cnt > 0, last_t - first_t + 1, 0)
        first_rt = tile_off[None, :] + first_t
        if expert_major:
            cnt, frt = pair_items.T.reshape(-1), first_rt.T.reshape(-1)
        else:
            cnt, frt = pair_items.reshape(-1), first_rt.reshape(-1)
        ends = jnp.cumsum(cnt)
        starts = ends - cnt
        total = ends[-1]
        w = jnp.arange(n_items, dtype=jnp.int32)
        wc = jnp.minimum(w, total - 1)
        p = jnp.sum((ends[None, :] <= wc[:, None]).astype(jnp.int32), axis=1)
        rt = (frt[p] + wc - starts[p]).astype(jnp.int32)
        e_w, s_w = (p // nb, p % nb) if expert_major else (p % N_EXPERTS, p // N_EXPERTS)
        valid = w < total
        key = rt if expert_major else s_w
        prev = jnp.concatenate([jnp.full((1,), -1, jnp.int32), key[:-1]])
        nxt = jnp.concatenate([key[1:], jnp.full((1,), -1, jnp.int32)])
        first = valid & (key != prev)
        last = valid & ((key != nxt) | (w == total - 1))
        i32 = lambda v: v.astype(jnp.int32)
        return rt, e_w, s_w, i32(first), i32(last), i32(valid)

    return dest, items


def _moe_ffn(x, h, gates, w1, w3, w2, mod):
    n = x.shape[0]
    tm = MOE_TM
    n_rt = 2 * n // tm + N_EXPERTS
    n_rows = n_rt * tm
    dest, items = _moe_plan(gates, n_rt)
    g_items = items(True, MOE_GATHER_TS)
    c_items = items(False, MOE_COMBINE_TS)

    ts = MOE_GATHER_TS
    nb = n // ts
    dest_rows = dest.T.reshape(N_EXPERTS * nb, 1, ts)
    gate_rows = gates[:, :N_EXPERTS].T.reshape(N_EXPERTS * nb, 1, ts)
    pair_row = lambda w, rt, e, s, *_: (e[w] * nb + s[w], 0, 0)
    expert_w = lambda w, rt, e, *_: (e[w], 0, 0)
    ys = pl.pallas_call(
        _moe_expert_kernel,
        grid_spec=pltpu.PrefetchScalarGridSpec(
            num_scalar_prefetch=6,
            grid=(g_items[0].shape[0],),
            in_specs=[
                pl.BlockSpec((ts, D_MODEL), lambda w, rt, e, s, *_: (s[w], 0)),
                pl.BlockSpec((None, 1, ts), pair_row),
                pl.BlockSpec((None, 1, ts), pair_row),
                pl.BlockSpec((None, D_MODEL, D_FF), expert_w),
                pl.BlockSpec((None, D_MODEL, D_FF), expert_w),
                pl.BlockSpec((None, D_FF, D_MODEL), expert_w),
            ],
            out_specs=pl.BlockSpec((tm, D_MODEL), lambda w, rt, *_: (rt[w], 0)),
            scratch_shapes=[pltpu.VMEM((tm, D_MODEL), F32), pltpu.VMEM((tm, 128), F32)],
        ),
        out_shape=jax.ShapeDtypeStruct((n_rows, D_MODEL), BF16),
        compiler_params=_cparams(("arbitrary",)),
        name="moe_experts",
    )(*g_items, h, dest_rows, gate_rows, w1, w3, w2)

    ts = MOE_COMBINE_TS
    dest_cols = dest.astype(F32)
    return pl.pallas_call(
        _moe_combine_kernel,
        grid_spec=pltpu.PrefetchScalarGridSpec(
            num_scalar_prefetch=6,
            grid=(c_items[0].shape[0],),
            in_specs=[
                pl.BlockSpec((tm, D_MODEL), lambda w, rt, *_: (rt[w], 0)),
                pl.BlockSpec((ts, N_EXPERTS), lambda w, rt, e, s, *_: (s[w], 0)),
                pl.BlockSpec((ts, D_MODEL), lambda w, rt, e, s, *_: (s[w], 0)),
                pl.BlockSpec((ts // ROW_TILE, 1, D_MODEL), lambda w, rt, e, s, *_: (s[w], 0, 0)),
            ],
            out_specs=pl.BlockSpec((ts, D_MODEL), lambda w, rt, e, s, *_: (s[w], 0)),
            scratch_shapes=[pltpu.VMEM((ts, D_MODEL), F32)],
        ),
        out_shape=jax.ShapeDtypeStruct((n, D_MODEL), F32),
        compiler_params=_cparams(("arbitrary",)),
        name="moe_combine",
    )(*c_items, ys, dest_cols, x, mod)


def _rope_tables(n_ctx, n_lat):
    pos_r = jnp.arange(n_lat, dtype=jnp.int32) // GRID_W
    pos_c = jnp.arange(n_lat, dtype=jnp.int32) % GRID_W

    def per_head(head_dim):
        nf = head_dim // 4
        inv = ROPE_BASE ** (-jnp.arange(nf, dtype=F32) / nf)
        ang_r = pos_r.astype(F32)[:, None] * inv[None, :]
        ang_c = pos_c.astype(F32)[:, None] * inv[None, :]
        z = jnp.zeros_like(ang_r)
        cos = jnp.concatenate([jnp.cos(ang_r)] * 2 + [jnp.cos(ang_c)] * 2, axis=-1)
        sa = jnp.concatenate([-jnp.sin(ang_r), z, -jnp.sin(ang_c), z], axis=-1)
        sb = jnp.concatenate([z, jnp.sin(ang_r), z, jnp.sin(ang_c)], axis=-1)
        reps = 256 // head_dim
        tabs = [jnp.tile(t, (1, reps)) for t in (cos, sa, sb)]
        ctx = [jnp.ones((n_ctx, 256), F32), jnp.zeros((n_ctx, 256), F32), jnp.zeros((n_ctx, 256), F32)]
        return [jnp.concatenate([c, t], axis=0) for c, t in zip(ctx, tabs)]

    return jnp.concatenate(per_head(DIFF_HEAD_DIM) + per_head(SWA_HEAD_DIM), axis=-1)


def _block_diag_mean(group):
    idx = jnp.arange(256) // group
    return jnp.where(idx[:, None] == idx[None, :], 1.0 / group, 0.0).astype(BF16)


def kernel(x, c, ctx, c_ctx, ada_w, ada_b, norm_mix_g, norm_ffn_g, w_in, w_out, diff_qk_norm_g, diff_lambda,
           diff_subln_g, hgrn_lb_logits, hgrn_norm_g, swa_qk_norm_g, swa_sink, ffn_w1, ffn_w3, ffn_w2,
           moe_router, moe_w1, moe_w3, moe_w2):
    n_batch, n_lat, _ = x.shape
    n_ctx = ctx.shape[1]
    depth = ada_w.shape[0]
    t_len = n_ctx + n_lat
    tps = t_len // ROW_TILE
    assert n_ctx == ROW_TILE and n_lat % ROW_TILE == 0 and n_lat >= ROW_TILE + 2 * WINDOW
    assert (n_batch * t_len) % MOE_GATHER_TS == 0 and (n_batch * t_len) % MOE_COMBINE_TS == 0

    xc = jnp.concatenate([ctx, x], axis=1).reshape(n_batch * t_len, D_MODEL)

    n_rows = -(-(n_batch + 1) // 8) * 8
    cond = jnp.concatenate([c, c_ctx[None, :], jnp.zeros((n_rows - n_batch - 1, D_MODEL), F32)], axis=0)
    mods = _ada_modulation(cond, ada_w, ada_b).reshape(depth, n_rows, 6, D_MODEL)
    m_lat = jnp.broadcast_to(mods[:, :n_batch, None], (depth, n_batch, tps - 1, 6, D_MODEL))
    m_ctx = jnp.broadcast_to(mods[:, n_batch, None, None], (depth, n_batch, 1, 6, D_MODEL))
    mods = jnp.concatenate([m_ctx, m_lat], axis=2).reshape(depth, n_batch * tps, 6, D_MODEL)

    lb = jnp.cumsum(jax.nn.softmax(hgrn_lb_logits.astype(F32), axis=1), axis=1)
    lb = lb - lb[:, :1]
    rope = _rope_tables(n_ctx, n_lat)
    g32 = _block_diag_mean(DIFF_HEAD_DIM)
    g64 = _block_diag_mean(SWA_HEAD_DIM)

    perm_q = jnp.arange(SWA_WIDTH).reshape(SWA_KV_HEADS, SWA_GROUP, SWA_HEAD_DIM).transpose(1, 0, 2).reshape(-1)
    qc0 = 3 * DIFF_WIDTH + 5 * HGRN_WIDTH
    col_perm = jnp.concatenate([jnp.arange(qc0), qc0 + perm_q, jnp.arange(qc0 + SWA_WIDTH, D_IN)])
    oc0 = DIFF_WIDTH + HGRN_WIDTH
    row_perm = jnp.concatenate([jnp.arange(oc0), oc0 + perm_q])

    for layer in range(depth):
        lam_init = 0.8 - 0.6 * math.exp(-0.3 * layer)
        mod = mods[layer]
        w_in_l = w_in[layer][:, col_perm].astype(BF16)
        w_out_l = w_out[layer][row_perm, :].astype(BF16)
        gains = jnp.stack([
            jnp.tile(diff_qk_norm_g[layer, 0], 8) * (DIFF_HEAD_DIM ** -0.5 * LOG2_E),
            jnp.tile(diff_qk_norm_g[layer, 1], 8),
            jnp.tile(swa_qk_norm_g[layer, 0], 4) * (SWA_HEAD_DIM ** -0.5 * LOG2_E),
            jnp.tile(swa_qk_norm_g[layer, 1], 4),
        ]).astype(F32)
        lbt = jnp.stack([
            jnp.log(lb[:, layer]).reshape(-1),
            jnp.log1p(-lb[:, layer]).reshape(-1),
        ]).astype(F32)
        lv = diff_lambda[layer].astype(F32)
        lam = jnp.exp(jnp.sum(lv[0] * lv[1])) - jnp.exp(jnp.sum(lv[2] * lv[3])) + lam_init
        lam_row = jnp.full((1, 128), lam, F32)
        post_row = (jnp.tile(diff_subln_g[layer], DIFF_HEADS) * (1.0 - lam_init)).reshape(1, DIFF_WIDTH).astype(F32)
        sink_tab = jnp.broadcast_to(swa_sink[layer].astype(F32)[:, None] * LOG2_E, (SWA_Q_HEADS, 128))

        (qa, ka, va, hq, hv, lff, lfb, hg, qc, kc, vc) = _in_projection(
            xc, mod[:, 0:2], norm_mix_g[layer].reshape(1, D_MODEL), w_in_l, rope, gains, lbt, g32, g64, tps)
        a = _diff_attention(qa, ka, va, lam_row, post_row, g64, n_batch, t_len, n_ctx)
        b = _hgrn(hq, hv, lff, lfb, hg, hgrn_norm_g[layer].reshape(1, HGRN_K), n_batch, t_len, n_ctx)
        cc = _swa_attention(qc, kc, vc, sink_tab, n_batch, t_len, n_ctx)

        jj = layer // 2
        g2 = norm_ffn_g[layer].reshape(1, D_MODEL)
        if layer % 2 == 1:
            router = jnp.pad(moe_router[jj].astype(F32), ((0, 0), (0, 128 - N_EXPERTS)))
            x_mid, h2, gates = _out_projection_router(xc, a, b, cc, w_out_l, mod[:, 2:5], g2, router)
            if layer == depth - 1:
                is_lat = (jnp.arange(n_batch * t_len, dtype=jnp.int32) % t_len) >= n_ctx
                gates = jnp.where(is_lat[:, None], gates, 0.0)
            xc = _moe_ffn(x_mid, h2, gates, moe_w1[jj].astype(BF16), moe_w3[jj].astype(BF16),
                          moe_w2[jj].astype(BF16), mod[:, 5:6])
        else:
            xc = _outproj_dense_ffn(xc, a, b, cc, w_out_l, mod[:, 2:6], g2, ffn_w1[jj].astype(BF16),
                                    ffn_w3[jj].astype(BF16), ffn_w2[jj].astype(BF16))

    return xc.reshape(n_batch, t_len, D_MODEL)[:, n_ctx:, :]
```

```python
import functools
import math

import jax
import jax.numpy as jnp
from jax import lax
from jax.experimental import pallas as pl
from jax.experimental.pallas import tpu as pltpu

D_MODEL = 1024
GRID_W = 64
DIFF_HEADS = 4
DIFF_HEAD_DIM = 32
DIFF_V_DIM = 64
DIFF_WIDTH = 256
HGRN_HEADS = 4
HGRN_K = 128
HGRN_WIDTH = 512
SWA_Q_HEADS = 4
SWA_KV_HEADS = 2
SWA_GROUP = 2
SWA_HEAD_DIM = 64
SWA_WIDTH = 256
SWA_KV_WIDTH = 128
WINDOW = 128
D_FF = 2816
N_EXPERTS = 8
ROPE_BASE = 10000.0
NORM_EPS = 1e-6
D_IN = 3840

F32 = jnp.float32
BF16 = jnp.bfloat16

ROW_TILE = 256
HGRN_CHUNK = 64
HGRN_UNROLL = 4
HGRN_SAFE_DECAY = 80.0
NEG_BIG = -1e30
LOG2_E = 1.4426950408889634
VMEM_LIMIT = 56 * 1024 * 1024


def _cparams(sem):
    return pltpu.CompilerParams(dimension_semantics=sem, vmem_limit_bytes=VMEM_LIMIT)


def _split_bf16(v):
    hi = v.astype(BF16)
    lo = (v - hi.astype(F32)).astype(BF16)
    return hi, lo


def _dot(a, b):
    return jnp.dot(a, b, preferred_element_type=F32)


def _dot_nt(a, b):
    return lax.dot_general(a, b, (((1,), (1,)), ((), ())), preferred_element_type=F32)


def _dot_tn(a, b):
    return lax.dot_general(a, b, (((0,), (0,)), ((), ())), preferred_element_type=F32)


def _group_mean_sq(y, gmat):
    hi, lo = _split_bf16(y * y)
    return _dot(hi, gmat) + _dot(lo, gmat)


def _silu(v):
    return v * (1.0 / (1.0 + jnp.exp(-v)))


def _ada_kernel(s_ref, w_ref, b_ref, o_ref):
    s = s_ref[...]
    s = _silu(s)
    s_hi, s_lo = _split_bf16(s)
    w_hi, w_lo = _split_bf16(w_ref[...])
    o_ref[...] = _dot(s_hi, w_hi) + _dot(s_lo, w_hi) + _dot(s_hi, w_lo) + b_ref[...]


def _ada_modulation(cond, ada_w, ada_b):
    depth = ada_w.shape[0]
    r = cond.shape[0]
    nblk = 6 * D_MODEL // 1024
    return pl.pallas_call(
        _ada_kernel,
        grid=(depth, nblk),
        in_specs=[
            pl.BlockSpec((r, D_MODEL), lambda l, n: (0, 0)),
            pl.BlockSpec((None, D_MODEL, 1024), lambda l, n: (l, 0, n)),
            pl.BlockSpec((None, 1, 1024), lambda l, n: (l, 0, n)),
        ],
        out_specs=pl.BlockSpec((None, r, 1024), lambda l, n: (l, 0, n)),
        out_shape=jax.ShapeDtypeStruct((depth, r, 6 * D_MODEL), F32),
        compiler_params=_cparams(("parallel", "parallel")),
        name="ada_modulation",
    )(cond, ada_w, ada_b.reshape(depth, 1, 6 * D_MODEL))


def _rope(y, cos, sa, sb, half):
    w = y.shape[-1]
    fwd = pltpu.roll(y, w - half, 1)
    bwd = pltpu.roll(y, half, 1)
    return y * cos + fwd * sa + bwd * sb


def _inproj_kernel(x_ref, mod_ref, g_ref, w_ref, rope_ref, gains_ref, lbt_ref, ga_ref, gc_ref,
                   qa_ref, ka_ref, va_ref, hq_ref, hv_ref, lff_ref, lfb_ref, hg_ref,
                   qc_ref, kc_ref, vc_ref):
    x = x_ref[...]
    shift = mod_ref[0:1, :]
    scale = mod_ref[1:2, :]
    ms = jnp.mean(x * x, axis=-1, keepdims=True)
    h = (x * lax.rsqrt(ms + NORM_EPS) * g_ref[...]) * (1.0 + scale) + shift
    hb = h.astype(BF16)

    def proj(a, b):
        return _dot(hb, w_ref[:, a:b])

    def qk_prep(y, gmat, gain, cos, sa, sb, half):
        msq = _group_mean_sq(y, gmat)
        y = y * lax.rsqrt(msq + NORM_EPS) * gain
        return _rope(y, cos, sa, sb, half)

    ga = ga_ref[...]
    gc = gc_ref[...]
    ra = [rope_ref[:, 256 * i:256 * (i + 1)] for i in range(6)]
    qa = qk_prep(proj(0, 256), ga, gains_ref[0:1, :], ra[0], ra[1], ra[2], 8)
    qa_ref[...] = qa.astype(BF16)
    ka = qk_prep(proj(256, 512), ga, gains_ref[1:2, :], ra[0], ra[1], ra[2], 8)
    ka_ref[...] = ka.astype(BF16)
    va_ref[...] = proj(512, 768).astype(BF16)
    hq_ref[...] = _silu(proj(768, 1280)).astype(BF16)
    hv_ref[...] = proj(1280, 1792).astype(BF16)
    for d, lf_ref in enumerate((lff_ref, lfb_ref)):
        z = proj(1792 + 512 * d, 2304 + 512 * d)
        log_lb = lbt_ref[0:1, 512 * d:512 * (d + 1)]
        log1m_lb = lbt_ref[1:2, 512 * d:512 * (d + 1)]
        sp = jnp.maximum(-z, 0.0) + jnp.log(1.0 + jnp.exp(-jnp.abs(z)))
        b2 = log1m_lb - sp
        mx = jnp.maximum(log_lb, b2)
        lf_ref[...] = mx + jnp.log(1.0 + jnp.exp(-jnp.abs(log_lb - b2)))
    hg_ref[...] = _silu(proj(2816, 3328)).astype(BF16)
    qc = qk_prep(proj(3328, 3584), gc, gains_ref[2:3, :], ra[3], ra[4], ra[5], 16)
    qc_ref[...] = qc.astype(BF16)
    kc = qk_prep(proj(3584, 3712), gc[0:128, 0:128], gains_ref[3:4, 0:128],
                 ra[3][:, 0:128], ra[4][:, 0:128], ra[5][:, 0:128], 16)
    kc_ref[...] = kc.astype(BF16)
    vc_ref[...] = proj(3712, 3840).astype(BF16)


def _in_projection(xc, mod, g, w, rope, gains, lbt, ga, gc, tiles_per_seq):
    n = xc.shape[0]
    nt = n // ROW_TILE
    tps = tiles_per_seq

    def row(i):
        return (i, 0)

    def mod_idx(i):
        return (i, 0, 0)

    const = lambda i: (0, 0)
    widths = [(256, BF16), (256, BF16), (256, BF16), (512, BF16), (512, BF16),
              (512, F32), (512, F32), (512, BF16), (256, BF16), (128, BF16), (128, BF16)]
    return pl.pallas_call(
        _inproj_kernel,
        grid=(nt,),
        in_specs=[
            pl.BlockSpec((ROW_TILE, D_MODEL), row),
            pl.BlockSpec((None, 2, D_MODEL), mod_idx),
            pl.BlockSpec((1, D_MODEL), const),
            pl.BlockSpec((D_MODEL, D_IN), const),
            pl.BlockSpec((ROW_TILE, 6 * 256), lambda i: (i % tps, 0)),
            pl.BlockSpec((4, 256), const),
            pl.BlockSpec((2, 1024), const),
            pl.BlockSpec((256, 256), const),
            pl.BlockSpec((256, 256), const),
        ],
        out_specs=[pl.BlockSpec((ROW_TILE, wd), row) for wd, _ in widths],
        out_shape=[jax.ShapeDtypeStruct((n, wd), dt) for wd, dt in widths],
        compiler_params=_cparams(("parallel",)),
        name="in_projection",
    )(xc, mod, g, w, rope, gains, lbt, ga, gc)


def _diff_attend(q, k, v, lam):
    lane = lax.broadcasted_iota(jnp.int32, q.shape, 1)
    n_maps = 2 * DIFF_HEADS

    def scores(g):
        return _dot_nt(jnp.where(lane // DIFF_HEAD_DIM == g, q, jnp.zeros_like(q)), k)

    out = jnp.zeros(q.shape, F32)
    parts = []
    s_next = scores(0)
    for g in range(n_maps):
        s = s_next
        if g + 1 < n_maps:
            s_next = scores(g + 1)
        e = jnp.exp2(s - jnp.max(s, axis=-1, keepdims=True)).astype(BF16)
        o = _dot(e, v[g // 2])
        parts.append(o * (1.0 / pltpu.roll(o, DIFF_WIDTH - DIFF_V_DIM, 1)))
        if g % 2 == 1:
            out = out + jnp.where(lane // DIFF_V_DIM == g // 2, parts[g - 1] - lam * parts[g], 0.0)
    return out


def _diff_kernel(q_ref, k_ref, v_ref, lam_ref, post_ref, g64_ref, o_ref, vaug_ref, *, n_ctx):
    j = pl.program_id(1)
    lam = lam_ref[0:1, 0:1]

    @pl.when(j == 0)
    def _():
        v = v_ref[...]
        lane = lax.broadcasted_iota(jnp.int32, v.shape, 1)
        for hd in range(DIFF_HEADS):
            vaug_ref[hd] = jnp.where(lane // DIFF_V_DIM == (hd + 1) % DIFF_HEADS, jnp.ones_like(v), v)

    def finish(o):
        msq = _group_mean_sq(o, g64_ref[...])
        o_ref[...] = (o * lax.rsqrt(msq + NORM_EPS) * post_ref[...]).astype(BF16)

    @pl.when(j == 0)
    def _():
        finish(_diff_attend(q_ref[...], k_ref[0:n_ctx, :], [vaug_ref[hd, 0:n_ctx, :] for hd in range(DIFF_HEADS)],
                            lam))

    @pl.when(j > 0)
    def _():
        finish(_diff_attend(q_ref[...], k_ref[...], [vaug_ref[hd] for hd in range(DIFF_HEADS)], lam))


def _diff_attention(qa, ka, va, lam_row, post_row, g64, n_batch, t_len, n_ctx):
    tps = t_len // ROW_TILE
    q3 = qa.reshape(n_batch, t_len, DIFF_WIDTH)
    k3 = ka.reshape(n_batch, t_len, DIFF_WIDTH)
    v3 = va.reshape(n_batch, t_len, DIFF_WIDTH)
    const = lambda b, j: (0, 0)
    out = pl.pallas_call(
        functools.partial(_diff_kernel, n_ctx=n_ctx),
        grid=(n_batch, tps),
        in_specs=[
            pl.BlockSpec((None, ROW_TILE, DIFF_WIDTH), lambda b, j: (b, j, 0)),
            pl.BlockSpec((None, t_len, DIFF_WIDTH), lambda b, j: (b, 0, 0)),
            pl.BlockSpec((None, t_len, DIFF_WIDTH), lambda b, j: (b, 0, 0)),
            pl.BlockSpec((1, 128), const),
            pl.BlockSpec((1, DIFF_WIDTH), const),
            pl.BlockSpec((256, 256), const),
        ],
        out_specs=pl.BlockSpec((None, ROW_TILE, DIFF_WIDTH), lambda b, j: (b, j, 0)),
        out_shape=jax.ShapeDtypeStruct((n_batch, t_len, DIFF_WIDTH), BF16),
        scratch_shapes=[pltpu.VMEM((DIFF_HEADS, t_len, DIFF_WIDTH), BF16)],
        compiler_params=_cparams(("parallel", "arbitrary")),
        name="diff_attention",
    )(q3, k3, v3, lam_row, post_row, g64)
    return out.reshape(n_batch * t_len, DIFF_WIDTH)


def _swa_kernel(q_ref, k_ref, v_ref, sink_ref, o_ref, *, n_ctx, n_lat):
    j = pl.program_id(1)
    q = q_ref[...]
    tq = q.shape[0]
    lane = lax.broadcasted_iota(jnp.int32, (tq, SWA_KV_WIDTH), 1)
    span = ROW_TILE + 2 * WINDOW
    heads = [(g, hd) for g in range(SWA_GROUP) for hd in range(SWA_KV_HEADS)]

    def run(k, v, valid):
        vlane = lax.broadcasted_iota(jnp.int32, v.shape, 1)
        vaug = [jnp.where(vlane // SWA_HEAD_DIM == hd, v, jnp.ones_like(v)) for hd in range(SWA_KV_HEADS)]
        sinks = [sink_ref[SWA_GROUP * hd + g:SWA_GROUP * hd + g + 1, 0:1] for g, hd in heads]
        scores = []
        for g, hd in heads:
            qg = q[:, SWA_KV_WIDTH * g:SWA_KV_WIDTH * (g + 1)]
            s = _dot_nt(jnp.where(lane // SWA_HEAD_DIM == hd, qg, jnp.zeros_like(qg)), k)
            scores.append(s if valid is None else jnp.where(valid, s, NEG_BIG))
        maxes = [jnp.maximum(jnp.max(s, axis=-1, keepdims=True), sink) for s, sink in zip(scores, sinks)]
        outs = [_dot(jnp.exp2(s - mx).astype(BF16), vaug[hd]) for s, mx, (g, hd) in zip(scores, maxes, heads)]
        for g in range(SWA_GROUP):
            out = jnp.zeros((tq, SWA_KV_WIDTH), F32)
            for i, (gi, hd) in enumerate(heads):
                if gi == g:
                    den = pltpu.roll(outs[i], SWA_HEAD_DIM, 1) + jnp.exp2(sinks[i] - maxes[i])
                    out = out + jnp.where(lane // SWA_HEAD_DIM == hd, outs[i] * (1.0 / den), 0.0)
            o_ref[:, SWA_KV_WIDTH * g:SWA_KV_WIDTH * (g + 1)] = out.astype(BF16)

    @pl.when(j == 0)
    def _():
        run(k_ref[0:n_ctx, :], v_ref[0:n_ctx, :], None)

    @pl.when(j > 0)
    def _():
        q0 = (j - 1) * ROW_TILE
        ks = jnp.clip(q0 - WINDOW, 0, n_lat - span)
        ks = pl.multiple_of(ks, WINDOW)
        k = jnp.concatenate([k_ref[0:n_ctx, :], k_ref[pl.ds(n_ctx + ks, span), :]], axis=0)
        v = jnp.concatenate([v_ref[0:n_ctx, :], v_ref[pl.ds(n_ctx + ks, span), :]], axis=0)
        qpos = q0 + lax.broadcasted_iota(jnp.int32, (tq, n_ctx + span), 0)
        kpos = ks - n_ctx + lax.broadcasted_iota(jnp.int32, (tq, n_ctx + span), 1)
        run(k, v, (kpos < ks) | (jnp.abs(qpos - kpos) <= WINDOW))


def _swa_attention(qc, kc, vc, sink_tab, n_batch, t_len, n_ctx):
    tps = t_len // ROW_TILE
    q3 = qc.reshape(n_batch, t_len, SWA_WIDTH)
    k3 = kc.reshape(n_batch, t_len, SWA_KV_WIDTH)
    v3 = vc.reshape(n_batch, t_len, SWA_KV_WIDTH)
    out = pl.pallas_call(
        functools.partial(_swa_kernel, n_ctx=n_ctx, n_lat=t_len - n_ctx),
        grid=(n_batch, tps),
        in_specs=[
            pl.BlockSpec((None, ROW_TILE, SWA_WIDTH), lambda b, j: (b, j, 0)),
            pl.BlockSpec((None, t_len, SWA_KV_WIDTH), lambda b, j: (b, 0, 0)),
            pl.BlockSpec((None, t_len, SWA_KV_WIDTH), lambda b, j: (b, 0, 0)),
            pl.BlockSpec((SWA_Q_HEADS, 128), lambda b, j: (0, 0)),
        ],
        out_specs=pl.BlockSpec((None, ROW_TILE, SWA_WIDTH), lambda b, j: (b, j, 0)),
        out_shape=jax.ShapeDtypeStruct((n_batch, t_len, SWA_WIDTH), BF16),
        compiler_params=_cparams(("parallel", "arbitrary")),
        name="swa_attention",
    )(q3, k3, v3, sink_tab)
    return out.reshape(n_batch * t_len, SWA_WIDTH)


def _hgrn_kernel(q_ref, v_ref, lff_ref, lfb_ref, g_ref, ng_ref, o_ref, of_ref, ob_ref, st_ref, cum_ref, qrow_ref,
                 strong_ref, *, n_ctx, t_len):
    c = HGRN_CHUNK
    n_chunks = t_len // c
    n_ctx_chunks = n_ctx // c
    mid = c // 2
    ri = lax.broadcasted_iota(jnp.int32, (c, c), 0)
    ci = lax.broadcasted_iota(jnp.int32, (c, c), 1)
    causal = ri >= ci
    anti = ri <= ci
    tri_f = jnp.where(causal, 1.0, 0.0).astype(BF16)
    tri_b = jnp.where(anti, 1.0, 0.0).astype(BF16)

    st_ref[...] = jnp.zeros_like(st_ref)
    n_iter = n_chunks // HGRN_UNROLL

    def chains_of(i):
        chains = []
        for u in range(HGRN_UNROLL):
            step = i * HGRN_UNROLL + u
            rf = pl.multiple_of(step * c, c)
            cb = jnp.where(step < n_ctx_chunks, n_ctx_chunks - 1 - step, n_chunks - 1 + n_ctx_chunks - step)
            rb = pl.multiple_of(cb * c, c)
            for hd in range(HGRN_HEADS):
                cols = slice(HGRN_K * hd, HGRN_K * (hd + 1))
                chains.append((2 * hd, pl.ds(rf, c), cols, lff_ref, of_ref, tri_f, causal, mid - 1, c - 1))
                chains.append((2 * hd + 1, pl.ds(rb, c), cols, lfb_ref, ob_ref, tri_b, anti, mid, 0))
        return chains

    def max_half_decay(i):
        worst = jnp.zeros((1, HGRN_K), F32)
        for _, rows, cols, lf_ref, _, _, _, _, _ in chains_of(i):
            lf = lf_ref[rows, cols]
            worst = jnp.maximum(worst, jnp.maximum(jnp.abs(jnp.sum(lf[0:mid, :], axis=0, keepdims=True)),
                                                   jnp.abs(jnp.sum(lf[mid:c, :], axis=0, keepdims=True))))
        return jnp.max(worst)

    def exact_step(i):
        for slot, rows, cols, lf_ref, o_ref_d, tri, mask, _, end_row in chains_of(i):
            lf = lf_ref[rows, cols]
            lf_hi, lf_lo = _split_bf16(lf)
            cum = _dot(tri, lf_hi) + _dot(tri, lf_lo)
            tot = cum[end_row:end_row + 1, :]
            v = v_ref[rows, cols]
            k = 1.0 - jnp.exp(lf)
            q = q_ref[rows, cols].astype(F32)
            cum_ref[...] = cum
            qrow_ref[...] = q

            def row(t, sct):
                w = jnp.exp(jnp.minimum(cum_ref[pl.ds(t, 1), :] - cum, 0.0))
                col = jnp.sum(qrow_ref[pl.ds(t, 1), :] * k * w, axis=-1, keepdims=True)
                return jnp.where(ci == t, col, sct)

            sct = lax.fori_loop(0, c, row, jnp.zeros((c, c), F32))
            valid_t = anti if mask is causal else causal
            st = st_ref[slot]
            o = (_dot_tn(jnp.where(valid_t, sct, 0.0).astype(BF16), v)
                 + _dot_nt((q * jnp.exp(cum)).astype(BF16), st.astype(BF16)))
            o_ref_d[rows, cols] = o.astype(BF16)
            st_ref[slot] = st * jnp.exp(tot) + _dot_tn(v, (k * jnp.exp(tot - cum)).astype(BF16))

    def fast_step(i):
        chains = chains_of(i)
        cums = []
        for _, rows, cols, lf_ref, _, tri, _, _, _ in chains:
            lf_hi, lf_lo = _split_bf16(lf_ref[rows, cols])
            cums.append(_dot(tri, lf_hi) + _dot(tri, lf_lo))
        prods = []
        for (slot, rows, cols, lf_ref, _, _, mask, ref_row, end_row), cum in zip(chains, cums):
            ref = cum[ref_row:ref_row + 1, :]
            tot = cum[end_row:end_row + 1, :]
            v = v_ref[rows, cols]
            k = 1.0 - jnp.exp(lf_ref[rows, cols])
            qt = q_ref[rows, cols].astype(F32) * jnp.exp(cum - ref)
            kt = k * jnp.exp(ref - cum)
            sc = _dot_nt(qt.astype(BF16), kt.astype(BF16))
            upd = _dot_tn(v, (kt * jnp.exp(tot - ref)).astype(BF16))
            prods.append((sc, (qt * jnp.exp(ref)).astype(BF16), jnp.exp(tot), upd, v))
        states = {}
        for (slot, rows, cols, _, o_ref_d, _, mask, _, _), (sc, q_in, decay, upd, v) in zip(chains, prods):
            st = states[slot] if slot in states else st_ref[slot]
            o = _dot(jnp.where(mask, sc, 0.0).astype(BF16), v) + _dot_nt(q_in, st.astype(BF16))
            o_ref_d[rows, cols] = o.astype(BF16)
            states[slot] = st * decay + upd
        for slot, st in states.items():
            st_ref[slot] = st

    def test_decay(i):
        strong_ref[0] = (max_half_decay(i) > HGRN_SAFE_DECAY).astype(jnp.int32)

    def body(i, _):
        strong = strong_ref[0]

        @pl.when(strong == 0)
        def _():
            test_decay(jnp.minimum(i + 1, n_iter - 1))
            fast_step(i)

        @pl.when(strong != 0)
        def _():
            test_decay(jnp.minimum(i + 1, n_iter - 1))
            exact_step(i)

        return 0

    test_decay(0)
    lax.fori_loop(0, n_iter, body, 0)

    def fin(t, _):
        r = pl.multiple_of(t * ROW_TILE, ROW_TILE)
        for hd in range(HGRN_HEADS):
            cols = slice(HGRN_K * hd, HGRN_K * (hd + 1))
            o = of_ref[pl.ds(r, ROW_TILE), cols].astype(F32) + ob_ref[pl.ds(r, ROW_TILE), cols].astype(F32)
            ms = jnp.mean(o * o, axis=-1, keepdims=True)
            o = o * lax.rsqrt(ms + NORM_EPS) * ng_ref[...]
            o_ref[pl.ds(r, ROW_TILE), cols] = (o * g_ref[pl.ds(r, ROW_TILE), cols].astype(F32)).astype(BF16)
        return 0

    lax.fori_loop(0, t_len // ROW_TILE, fin, 0)


def _hgrn(hq, hv, lff, lfb, hg, norm_g, n_batch, t_len, n_ctx):
    def view(a):
        return a.reshape(n_batch, t_len, HGRN_WIDTH)

    blk = pl.BlockSpec((None, t_len, HGRN_WIDTH), lambda b: (b, 0, 0))
    out = pl.pallas_call(
        functools.partial(_hgrn_kernel, n_ctx=n_ctx, t_len=t_len),
        grid=(n_batch,),
        in_specs=[blk] * 5 + [pl.BlockSpec((1, HGRN_K), lambda b: (0, 0))],
        out_specs=blk,
        out_shape=jax.ShapeDtypeStruct((n_batch, t_len, HGRN_WIDTH), BF16),
        scratch_shapes=[pltpu.VMEM((t_len, HGRN_WIDTH), BF16), pltpu.VMEM((t_len, HGRN_WIDTH), BF16),
                        pltpu.VMEM((2 * HGRN_HEADS, HGRN_K, HGRN_K), F32),
                        pltpu.VMEM((HGRN_CHUNK, HGRN_K), F32), pltpu.VMEM((HGRN_CHUNK, HGRN_K), F32),
                        pltpu.SMEM((1,), jnp.int32)],
        compiler_params=_cparams(("parallel",)),
        name="hgrn2",
    )(view(hq), view(hv), view(lff), view(lfb), view(hg), norm_g)
    return out.reshape(n_batch * t_len, HGRN_WIDTH)


def _mix_residual_norm(x_ref, a_ref, b_ref, c_ref, w_ref, mod_ref, g_ref):
    y = (_dot(a_ref[...], w_ref[0:256, :]) + _dot(b_ref[...], w_ref[256:768, :])
         + _dot(c_ref[...], w_ref[768:1024, :]))
    x = x_ref[...] + mod_ref[0:1, :] * y
    ms = jnp.mean(x * x, axis=-1, keepdims=True)
    h = (x * lax.rsqrt(ms + NORM_EPS) * g_ref[...]) * (1.0 + mod_ref[2:3, :]) + mod_ref[1:2, :]
    return x, h


def _outproj_router_kernel(x_ref, a_ref, b_ref, c_ref, w_ref, mod_ref, g_ref, r_ref, xo_ref, h_ref, gate_ref):
    x, h = _mix_residual_norm(x_ref, a_ref, b_ref, c_ref, w_ref, mod_ref, g_ref)
    xo_ref[...] = x
    h_ref[...] = h.astype(BF16)
    h_hi, h_lo = _split_bf16(h)
    r_hi, r_lo = _split_bf16(r_ref[...])
    logits = _dot(h_hi, r_hi) + _dot(h_lo, r_hi) + _dot(h_hi, r_lo)
    lane = lax.broadcasted_iota(jnp.int32, logits.shape, 1).astype(F32)
    logits = jnp.where(lane < N_EXPERTS, logits, NEG_BIG)
    v1 = jnp.max(logits, axis=-1, keepdims=True)
    i1 = jnp.min(jnp.where(logits == v1, lane, 128.0), axis=-1, keepdims=True)
    rest_l = jnp.where(lane == i1, NEG_BIG, logits)
    v2 = jnp.max(rest_l, axis=-1, keepdims=True)
    i2 = jnp.min(jnp.where(rest_l == v2, lane, 128.0), axis=-1, keepdims=True)
    e2 = jnp.exp(v2 - v1)
    w1 = 1.0 / (1.0 + e2)
    w2 = e2 * w1
    gate_ref[...] = jnp.where(lane == i1, w1, 0.0) + jnp.where(lane == i2, w2, 0.0)


def _out_projection_router(xc, a, b, c, w, mod, g, router):
    n = xc.shape[0]
    nt = n // ROW_TILE
    row = lambda i: (i, 0)
    const = lambda i: (0, 0)
    return pl.pallas_call(
        _outproj_router_kernel,
        grid=(nt,),
        in_specs=[
            pl.BlockSpec((ROW_TILE, D_MODEL), row),
            pl.BlockSpec((ROW_TILE, DIFF_WIDTH), row),
            pl.BlockSpec((ROW_TILE, HGRN_WIDTH), row),
            pl.BlockSpec((ROW_TILE, SWA_WIDTH), row),
            pl.BlockSpec((D_MODEL, D_MODEL), const),
            pl.BlockSpec((None, 3, D_MODEL), lambda i: (i, 0, 0)),
            pl.BlockSpec((1, D_MODEL), const),
            pl.BlockSpec((D_MODEL, 128), const),
        ],
        out_specs=[pl.BlockSpec((ROW_TILE, D_MODEL), row), pl.BlockSpec((ROW_TILE, D_MODEL), row),
                   pl.BlockSpec((ROW_TILE, 128), row)],
        out_shape=[jax.ShapeDtypeStruct((n, D_MODEL), F32), jax.ShapeDtypeStruct((n, D_MODEL), BF16),
                   jax.ShapeDtypeStruct((n, 128), F32)],
        compiler_params=_cparams(("parallel",)),
        name="out_projection",
    )(xc, a, b, c, w, mod, g, router)


def _outproj_ffn_kernel(x_ref, a_ref, b_ref, c_ref, wo_ref, mod_ref, g_ref, w1_ref, w3_ref, w2_ref, o_ref):
    x, h = _mix_residual_norm(x_ref, a_ref, b_ref, c_ref, wo_ref, mod_ref, g_ref)
    h = h.astype(BF16)
    u = _dot(h, w1_ref[...])
    act = (_silu(u) * _dot(h, w3_ref[...])).astype(BF16)
    o_ref[...] = x + mod_ref[3:4, :] * _dot(act, w2_ref[...])


def _outproj_dense_ffn(xc, a, b, c, wo, mod, g, w1, w3, w2):
    n = xc.shape[0]
    nt = n // ROW_TILE
    row = lambda i: (i, 0)
    const = lambda i: (0, 0)
    return pl.pallas_call(
        _outproj_ffn_kernel,
        grid=(nt,),
        in_specs=[
            pl.BlockSpec((ROW_TILE, D_MODEL), row),
            pl.BlockSpec((ROW_TILE, DIFF_WIDTH), row),
            pl.BlockSpec((ROW_TILE, HGRN_WIDTH), row),
            pl.BlockSpec((ROW_TILE, SWA_WIDTH), row),
            pl.BlockSpec((D_MODEL, D_MODEL), const),
            pl.BlockSpec((None, 4, D_MODEL), lambda i: (i, 0, 0)),
            pl.BlockSpec((1, D_MODEL), const),
            pl.BlockSpec((D_MODEL, D_FF), const),
            pl.BlockSpec((D_MODEL, D_FF), const),
            pl.BlockSpec((D_FF, D_MODEL), const),
        ],
        out_specs=pl.BlockSpec((ROW_TILE, D_MODEL), row),
        out_shape=jax.ShapeDtypeStruct((n, D_MODEL), F32),
        compiler_params=_cparams(("parallel",)),
        name="outproj_dense_ffn",
    )(xc, a, b, c, wo, mod, g, w1, w3, w2)


MOE_TM = 256
MOE_GATHER_TS = 1024
MOE_COMBINE_TS = 512
MOE_COMBINE_TM = 512


def _moe_expert_kernel(rt_ref, e_ref, s_ref, first_ref, last_ref, valid_ref,
                       h_ref, dest_ref, gate_ref, w1_ref, w3_ref, w2_ref, ys_ref, acc_ref, gacc_ref):
    w = pl.program_id(0)

    @pl.when(first_ref[w] == 1)
    def _():
        acc_ref[...] = jnp.zeros_like(acc_ref)
        gacc_ref[...] = jnp.zeros_like(gacc_ref)

    @pl.when(valid_ref[w] == 1)
    def _():
        rows = rt_ref[w] * MOE_TM + lax.broadcasted_iota(jnp.int32, (MOE_TM, h_ref.shape[0]), 0)
        hit = dest_ref[...] == rows
        acc_ref[...] += _dot(jnp.where(hit, 1.0, 0.0).astype(BF16), h_ref[...])
        gate = jnp.sum(jnp.where(hit, gate_ref[...], 0.0), axis=-1, keepdims=True)
        gacc_ref[...] += jnp.broadcast_to(gate, gacc_ref.shape)

    @pl.when(last_ref[w] == 1)
    def _():
        h = acc_ref[...].astype(BF16)
        u = _dot(h, w1_ref[...])
        act = (_silu(u) * _dot(h, w3_ref[...])).astype(BF16)
        ys_ref[...] = (gacc_ref[:, 0:1] * _dot(act, w2_ref[...])).astype(BF16)


def _moe_combine_kernel(rt_ref, e_ref, s_ref, first_ref, last_ref, valid_ref, used_ref,
                        ys_ref, dest_ref, x_ref, mod_ref, o_ref, acc_ref):
    w = pl.program_id(0)
    tile = ys_ref.shape[0]
    base = rt_ref[w] * tile
    written = base + tile <= used_ref[0]

    @pl.when(first_ref[w] == 1)
    def _():
        acc_ref[...] = jnp.zeros_like(acc_ref)

    def accumulate(ys):
        d = dest_ref[...]
        lane = lax.broadcasted_iota(jnp.int32, d.shape, 1)
        dcol = jnp.sum(jnp.where(lane == e_ref[w], d, 0.0), axis=-1, keepdims=True)
        cols = (base + lax.broadcasted_iota(jnp.int32, (d.shape[0], tile), 1)).astype(F32)
        onehot = jnp.where(dcol == cols, 1.0, 0.0).astype(BF16)
        acc_ref[...] += _dot(onehot, ys)

    @pl.when((valid_ref[w] == 1) & written)
    def _():
        accumulate(ys_ref[...])

    @pl.when((valid_ref[w] == 1) & jnp.logical_not(written))
    def _():
        ys = ys_ref[...]
        row = base + lax.broadcasted_iota(jnp.int32, ys.shape, 0)
        accumulate(jnp.where(row < used_ref[0], ys, jnp.zeros_like(ys)))

    @pl.when(last_ref[w] == 1)
    def _():
        for s in range(mod_ref.shape[0]):
            rows = slice(ROW_TILE * s, ROW_TILE * (s + 1))
            o_ref[rows, :] = x_ref[rows, :] + mod_ref[s] * acc_ref[rows, :]


def _moe_plan(gates, n_rt):
    n = gates.shape[0]
    tm = MOE_TM
    sel = gates[:, :N_EXPERTS] != 0.0
    si = sel.astype(jnp.int32)
    rank = jnp.cumsum(si, axis=0) - si
    counts = jnp.sum(si, axis=0)
    ntile = (counts + tm - 1) // tm
    tile_off = jnp.cumsum(ntile) - ntile
    dest = jnp.where(sel, tile_off[None, :] * tm + rank, -1)

    def items(expert_major, ts, rows_per_tile):
        nb = n // ts
        n_items = n_rt * tm // rows_per_tile + N_EXPERTS * nb
        blk_cnt = si.reshape(nb, ts, N_EXPERTS).sum(axis=1)
        first_row = tile_off[None, :] * tm + jnp.cumsum(blk_cnt, axis=0) - blk_cnt
        first_rt = first_row // rows_per_tile
        last_rt = (first_row + blk_cnt - 1) // rows_per_tile
        pair_items = jnp.where(blk_cnt > 0, last_rt - first_rt + 1, 0)
        if expert_major:
            cnt, frt = pair_items.T.reshape(-1), first_rt.T.reshape(-1)
        else:
            cnt, frt = pair_items.reshape(-1), first_rt.reshape(-1)
        ends = jnp.cumsum(cnt)
        starts = ends - cnt
        total = ends[-1]
        w = jnp.arange(n_items, dtype=jnp.int32)
        wc = jnp.minimum(w, total - 1)
        p = jnp.sum((ends[None, :] <= wc[:, None]).astype(jnp.int32), axis=1)
        rt = (frt[p] + wc - starts[p]).astype(jnp.int32)
        e_w, s_w = (p // nb, p % nb) if expert_major else (p % N_EXPERTS, p // N_EXPERTS)
        valid = w < total
        key = rt if expert_major else s_w
        prev = jnp.concatenate([jnp.full((1,), -1, jnp.int32), key[:-1]])
        nxt = jnp.concatenate([key[1:], jnp.full((1,), -1, jnp.int32)])
        first = valid & (key != prev)
        last = valid & ((key != nxt) | (w == total - 1))
        i32 = lambda v: v.astype(jnp.int32)
        return rt, e_w, s_w, i32(first), i32(last), i32(valid)

    used_rows = (jnp.sum(ntile) * tm).astype(jnp.int32).reshape(1)
    return dest, used_rows, items


def _moe_ffn(x, h, gates, w1, w3, w2, mod):
    n = x.shape[0]
    tm = MOE_TM
    n_rt = 2 * n // tm + N_EXPERTS
    n_rows = n_rt * tm
    assert n_rows % MOE_COMBINE_TM == 0
    dest, used_rows, items = _moe_plan(gates, n_rt)
    g_items = items(True, MOE_GATHER_TS, tm)
    c_items = items(False, MOE_COMBINE_TS, MOE_COMBINE_TM)

    ts = MOE_GATHER_TS
    nb = n // ts
    dest_rows = dest.T.reshape(N_EXPERTS * nb, 1, ts)
    gate_rows = gates[:, :N_EXPERTS].T.reshape(N_EXPERTS * nb, 1, ts)
    pair_row = lambda w, rt, e, s, *_: (e[w] * nb + s[w], 0, 0)
    expert_w = lambda w, rt, e, *_: (e[w], 0, 0)
    ys = pl.pallas_call(
        _moe_expert_kernel,
        grid_spec=pltpu.PrefetchScalarGridSpec(
            num_scalar_prefetch=6,
            grid=(g_items[0].shape[0],),
            in_specs=[
                pl.BlockSpec((ts, D_MODEL), lambda w, rt, e, s, *_: (s[w], 0)),
                pl.BlockSpec((None, 1, ts), pair_row),
                pl.BlockSpec((None, 1, ts), pair_row),
                pl.BlockSpec((None, D_MODEL, D_FF), expert_w),
                pl.BlockSpec((None, D_MODEL, D_FF), expert_w),
                pl.BlockSpec((None, D_FF, D_MODEL), expert_w),
            ],
            out_specs=pl.BlockSpec((tm, D_MODEL), lambda w, rt, *_: (rt[w], 0)),
            scratch_shapes=[pltpu.VMEM((tm, D_MODEL), F32), pltpu.VMEM((tm, 128), F32)],
        ),
        out_shape=jax.ShapeDtypeStruct((n_rows, D_MODEL), BF16),
        compiler_params=_cparams(("arbitrary",)),
        name="moe_experts",
    )(*g_items, h, dest_rows, gate_rows, w1, w3, w2)

    ts = MOE_COMBINE_TS
    dest_cols = dest.astype(F32)
    return pl.pallas_call(
        _moe_combine_kernel,
        grid_spec=pltpu.PrefetchScalarGridSpec(
            num_scalar_prefetch=7,
            grid=(c_items[0].shape[0],),
            in_specs=[
                pl.BlockSpec((MOE_COMBINE_TM, D_MODEL), lambda w, rt, *_: (rt[w], 0)),
                pl.BlockSpec((ts, N_EXPERTS), lambda w, rt, e, s, *_: (s[w], 0)),
                pl.BlockSpec((ts, D_MODEL), lambda w, rt, e, s, *_: (s[w], 0)),
                pl.BlockSpec((ts // ROW_TILE, 1, D_MODEL), lambda w, rt, e, s, *_: (s[w], 0, 0)),
            ],
            out_specs=pl.BlockSpec((ts, D_MODEL), lambda w, rt, e, s, *_: (s[w], 0)),
            scratch_shapes=[pltpu.VMEM((ts, D_MODEL), F32)],
        ),
        out_shape=jax.ShapeDtypeStruct((n, D_MODEL), F32),
        compiler_params=_cparams(("arbitrary",)),
        name="moe_combine",
    )(*c_items, used_rows, ys, dest_cols, x, mod)


def _rope_tables(n_ctx, n_lat):
    pos_r = jnp.arange(n_lat, dtype=jnp.int32) // GRID_W
    pos_c = jnp.arange(n_lat, dtype=jnp.int32) % GRID_W

    def per_head(head_dim):
        nf = head_dim // 4
        inv = ROPE_BASE ** (-jnp.arange(nf, dtype=F32) / nf)
        ang_r = pos_r.astype(F32)[:, None] * inv[None, :]
        ang_c = pos_c.astype(F32)[:, None] * inv[None, :]
        z = jnp.zeros_like(ang_r)
        cos = jnp.concatenate([jnp.cos(ang_r)] * 2 + [jnp.cos(ang_c)] * 2, axis=-1)
        sa = jnp.concatenate([-jnp.sin(ang_r), z, -jnp.sin(ang_c), z], axis=-1)
        sb = jnp.concatenate([z, jnp.sin(ang_r), z, jnp.sin(ang_c)], axis=-1)
        reps = 256 // head_dim
        tabs = [jnp.tile(t, (1, reps)) for t in (cos, sa, sb)]
        ctx = [jnp.ones((n_ctx, 256), F32), jnp.zeros((n_ctx, 256), F32), jnp.zeros((n_ctx, 256), F32)]
        return [jnp.concatenate([c, t], axis=0) for c, t in zip(ctx, tabs)]

    return jnp.concatenate(per_head(DIFF_HEAD_DIM) + per_head(SWA_HEAD_DIM), axis=-1)


def _block_diag_mean(group):
    idx = jnp.arange(256) // group
    return jnp.where(idx[:, None] == idx[None, :], 1.0 / group, 0.0).astype(BF16)


def kernel(x, c, ctx, c_ctx, ada_w, ada_b, norm_mix_g, norm_ffn_g, w_in, w_out, diff_qk_norm_g, diff_lambda,
           diff_subln_g, hgrn_lb_logits, hgrn_norm_g, swa_qk_norm_g, swa_sink, ffn_w1, ffn_w3, ffn_w2,
           moe_router, moe_w1, moe_w3, moe_w2):
    n_batch, n_lat, _ = x.shape
    n_ctx = ctx.shape[1]
    depth = ada_w.shape[0]
    t_len = n_ctx + n_lat
    tps = t_len // ROW_TILE
    assert n_ctx == ROW_TILE and n_lat % ROW_TILE == 0 and n_lat >= ROW_TILE + 2 * WINDOW
    assert (n_batch * t_len) % MOE_GATHER_TS == 0 and (n_batch * t_len) % MOE_COMBINE_TS == 0

    xc = jnp.concatenate([ctx, x], axis=1).reshape(n_batch * t_len, D_MODEL)

    n_rows = -(-(n_batch + 1) // 8) * 8
    cond = jnp.concatenate([c, c_ctx[None, :], jnp.zeros((n_rows - n_batch - 1, D_MODEL), F32)], axis=0)
    mods = _ada_modulation(cond, ada_w, ada_b).reshape(depth, n_rows, 6, D_MODEL)
    m_lat = jnp.broadcast_to(mods[:, :n_batch, None], (depth, n_batch, tps - 1, 6, D_MODEL))
    m_ctx = jnp.broadcast_to(mods[:, n_batch, None, None], (depth, n_batch, 1, 6, D_MODEL))
    mods = jnp.concatenate([m_ctx, m_lat], axis=2).reshape(depth, n_batch * tps, 6, D_MODEL)

    lb = jnp.cumsum(jax.nn.softmax(hgrn_lb_logits.astype(F32), axis=1), axis=1)
    lb = lb - lb[:, :1]
    rope = _rope_tables(n_ctx, n_lat)
    g32 = _block_diag_mean(DIFF_HEAD_DIM)
    g64 = _block_diag_mean(SWA_HEAD_DIM)

    perm_q = jnp.arange(SWA_WIDTH).reshape(SWA_KV_HEADS, SWA_GROUP, SWA_HEAD_DIM).transpose(1, 0, 2).reshape(-1)
    qc0 = 3 * DIFF_WIDTH + 5 * HGRN_WIDTH
    col_perm = jnp.concatenate([jnp.arange(qc0), qc0 + perm_q, jnp.arange(qc0 + SWA_WIDTH, D_IN)])
    oc0 = DIFF_WIDTH + HGRN_WIDTH
    row_perm = jnp.concatenate([jnp.arange(oc0), oc0 + perm_q])

    for layer in range(depth):
        lam_init = 0.8 - 0.6 * math.exp(-0.3 * layer)
        mod = mods[layer]
        w_in_l = w_in[layer][:, col_perm].astype(BF16)
        w_out_l = w_out[layer][row_perm, :].astype(BF16)
        gains = jnp.stack([
            jnp.tile(diff_qk_norm_g[layer, 0], 8) * (DIFF_HEAD_DIM ** -0.5 * LOG2_E),
            jnp.tile(diff_qk_norm_g[layer, 1], 8),
            jnp.tile(swa_qk_norm_g[layer, 0], 4) * (SWA_HEAD_DIM ** -0.5 * LOG2_E),
            jnp.tile(swa_qk_norm_g[layer, 1], 4),
        ]).astype(F32)
        lbt = jnp.stack([
            jnp.log(lb[:, layer]).reshape(-1),
            jnp.log1p(-lb[:, layer]).reshape(-1),
        ]).astype(F32)
        lv = diff_lambda[layer].astype(F32)
        lam = jnp.exp(jnp.sum(lv[0] * lv[1])) - jnp.exp(jnp.sum(lv[2] * lv[3])) + lam_init
        lam_row = jnp.full((1, 128), lam, F32)
        post_row = (jnp.tile(diff_subln_g[layer], DIFF_HEADS) * (1.0 - lam_init)).reshape(1, DIFF_WIDTH).astype(F32)
        sink_tab = jnp.broadcast_to(swa_sink[layer].astype(F32)[:, None] * LOG2_E, (SWA_Q_HEADS, 128))

        (qa, ka, va, hq, hv, lff, lfb, hg, qc, kc, vc) = _in_projection(
            xc, mod[:, 0:2], norm_mix_g[layer].reshape(1, D_MODEL), w_in_l, rope, gains, lbt, g32, g64, tps)
        a = _diff_attention(qa, ka, va, lam_row, post_row, g64, n_batch, t_len, n_ctx)
        b = _hgrn(hq, hv, lff, lfb, hg, hgrn_norm_g[layer].reshape(1, HGRN_K), n_batch, t_len, n_ctx)
        cc = _swa_attention(qc, kc, vc, sink_tab, n_batch, t_len, n_ctx)

        jj = layer // 2
        g2 = norm_ffn_g[layer].reshape(1, D_MODEL)
        if layer % 2 == 1:
            router = jnp.pad(moe_router[jj].astype(F32), ((0, 0), (0, 128 - N_EXPERTS)))
            x_mid, h2, gates = _out_projection_router(xc, a, b, cc, w_out_l, mod[:, 2:5], g2, router)
            if layer == depth - 1:
                is_lat = (jnp.arange(n_batch * t_len, dtype=jnp.int32) % t_len) >= n_ctx
                gates = jnp.where(is_lat[:, None], gates, 0.0)
            xc = _moe_ffn(x_mid, h2, gates, moe_w1[jj].astype(BF16), moe_w3[jj].astype(BF16),
                          moe_w2[jj].astype(BF16), mod[:, 5:6])
        else:
            xc = _outproj_dense_ffn(xc, a, b, cc, w_out_l, mod[:, 2:6], g2, ffn_w1[jj].astype(BF16),
                                    ffn_w3[jj].astype(BF16), ffn_w2[jj].astype(BF16))

    return xc.reshape(n_batch, t_len, D_MODEL)[:, n_ctx:, :]
```

```python
import functools
import math

import jax
import jax.numpy as jnp
from jax import lax
from jax.experimental import pallas as pl
from jax.experimental.pallas import tpu as pltpu
from jax.experimental.pallas import tpu_sc as plsc

D_MODEL = 1024
GRID_W = 64
DIFF_HEADS = 4
DIFF_HEAD_DIM = 32
DIFF_V_DIM = 64
DIFF_WIDTH = 256
HGRN_HEADS = 4
HGRN_K = 128
HGRN_WIDTH = 512
SWA_Q_HEADS = 4
SWA_KV_HEADS = 2
SWA_GROUP = 2
SWA_HEAD_DIM = 64
SWA_WIDTH = 256
SWA_KV_WIDTH = 128
WINDOW = 128
D_FF = 2816
N_EXPERTS = 8
ROPE_BASE = 10000.0
NORM_EPS = 1e-6
D_IN = 3840

F32 = jnp.float32
BF16 = jnp.bfloat16

ROW_TILE = 256
HGRN_CHUNK = 64
HGRN_UNROLL = 4
HGRN_SAFE_DECAY = 80.0
NEG_BIG = -1e30
LOG2_E = 1.4426950408889634
VMEM_LIMIT = 56 * 1024 * 1024


def _cparams(sem):
    return pltpu.CompilerParams(dimension_semantics=sem, vmem_limit_bytes=VMEM_LIMIT)


def _split_bf16(v):
    hi = v.astype(BF16)
    lo = (v - hi.astype(F32)).astype(BF16)
    return hi, lo


def _dot(a, b):
    return jnp.dot(a, b, preferred_element_type=F32)


def _dot_nt(a, b):
    return lax.dot_general(a, b, (((1,), (1,)), ((), ())), preferred_element_type=F32)


def _dot_tn(a, b):
    return lax.dot_general(a, b, (((0,), (0,)), ((), ())), preferred_element_type=F32)


def _group_mean_sq(y, gmat):
    hi, lo = _split_bf16(y * y)
    return _dot(hi, gmat) + _dot(lo, gmat)


def _silu(v):
    return v * (1.0 / (1.0 + jnp.exp(-v)))


def _ada_kernel(s_ref, w_ref, b_ref, o_ref):
    s = s_ref[...]
    s = _silu(s)
    s_hi, s_lo = _split_bf16(s)
    w_hi, w_lo = _split_bf16(w_ref[...])
    o_ref[...] = _dot(s_hi, w_hi) + _dot(s_lo, w_hi) + _dot(s_hi, w_lo) + b_ref[...]


def _ada_modulation(cond, ada_w, ada_b):
    depth = ada_w.shape[0]
    r = cond.shape[0]
    nblk = 6 * D_MODEL // 1024
    return pl.pallas_call(
        _ada_kernel,
        grid=(depth, nblk),
        in_specs=[
            pl.BlockSpec((r, D_MODEL), lambda l, n: (0, 0)),
            pl.BlockSpec((None, D_MODEL, 1024), lambda l, n: (l, 0, n)),
            pl.BlockSpec((None, 1, 1024), lambda l, n: (l, 0, n)),
        ],
        out_specs=pl.BlockSpec((None, r, 1024), lambda l, n: (l, 0, n)),
        out_shape=jax.ShapeDtypeStruct((depth, r, 6 * D_MODEL), F32),
        compiler_params=_cparams(("parallel", "parallel")),
        name="ada_modulation",
    )(cond, ada_w, ada_b.reshape(depth, 1, 6 * D_MODEL))


def _rope(y, cos, sa, sb, half):
    w = y.shape[-1]
    fwd = pltpu.roll(y, w - half, 1)
    bwd = pltpu.roll(y, half, 1)
    return y * cos + fwd * sa + bwd * sb


def _inproj_kernel(x_ref, mod_ref, g_ref, w_ref, rope_ref, gains_ref, lbt_ref, ga_ref, gc_ref,
                   qa_ref, ka_ref, va_ref, hq_ref, hv_ref, lff_ref, lfb_ref, hg_ref,
                   qc_ref, kc_ref, vc_ref):
    x = x_ref[...]
    shift = mod_ref[0:1, :]
    scale = mod_ref[1:2, :]
    ms = jnp.mean(x * x, axis=-1, keepdims=True)
    h = (x * lax.rsqrt(ms + NORM_EPS) * g_ref[...]) * (1.0 + scale) + shift
    hb = h.astype(BF16)

    def proj(a, b):
        return _dot(hb, w_ref[:, a:b])

    def qk_prep(y, gmat, gain, cos, sa, sb, half):
        msq = _group_mean_sq(y, gmat)
        y = y * lax.rsqrt(msq + NORM_EPS) * gain
        return _rope(y, cos, sa, sb, half)

    ga = ga_ref[...]
    gc = gc_ref[...]
    ra = [rope_ref[:, 256 * i:256 * (i + 1)] for i in range(6)]
    qa = qk_prep(proj(0, 256), ga, gains_ref[0:1, :], ra[0], ra[1], ra[2], 8)
    qa_ref[...] = qa.astype(BF16)
    ka = qk_prep(proj(256, 512), ga, gains_ref[1:2, :], ra[0], ra[1], ra[2], 8)
    ka_ref[...] = ka.astype(BF16)
    va_ref[...] = proj(512, 768).astype(BF16)
    hq_ref[...] = _silu(proj(768, 1280)).astype(BF16)
    hv_ref[...] = proj(1280, 1792).astype(BF16)
    for d, lf_ref in enumerate((lff_ref, lfb_ref)):
        z = proj(1792 + 512 * d, 2304 + 512 * d)
        log_lb = lbt_ref[0:1, 512 * d:512 * (d + 1)]
        log1m_lb = lbt_ref[1:2, 512 * d:512 * (d + 1)]
        sp = jnp.maximum(-z, 0.0) + jnp.log(1.0 + jnp.exp(-jnp.abs(z)))
        b2 = log1m_lb - sp
        mx = jnp.maximum(log_lb, b2)
        lf_ref[...] = mx + jnp.log(1.0 + jnp.exp(-jnp.abs(log_lb - b2)))
    hg_ref[...] = _silu(proj(2816, 3328)).astype(BF16)
    qc = qk_prep(proj(3328, 3584), gc, gains_ref[2:3, :], ra[3], ra[4], ra[5], 16)
    qc_ref[...] = qc.astype(BF16)
    kc = qk_prep(proj(3584, 3712), gc[0:128, 0:128], gains_ref[3:4, 0:128],
                 ra[3][:, 0:128], ra[4][:, 0:128], ra[5][:, 0:128], 16)
    kc_ref[...] = kc.astype(BF16)
    vc_ref[...] = proj(3712, 3840).astype(BF16)


def _in_projection(xc, mod, g, w, rope, gains, lbt, ga, gc, tiles_per_seq):
    n = xc.shape[0]
    nt = n // ROW_TILE
    tps = tiles_per_seq

    def row(i):
        return (i, 0)

    def mod_idx(i):
        return (i, 0, 0)

    const = lambda i: (0, 0)
    widths = [(256, BF16), (256, BF16), (256, BF16), (512, BF16), (512, BF16),
              (512, F32), (512, F32), (512, BF16), (256, BF16), (128, BF16), (128, BF16)]
    return pl.pallas_call(
        _inproj_kernel,
        grid=(nt,),
        in_specs=[
            pl.BlockSpec((ROW_TILE, D_MODEL), row),
            pl.BlockSpec((None, 2, D_MODEL), mod_idx),
            pl.BlockSpec((1, D_MODEL), const),
            pl.BlockSpec((D_MODEL, D_IN), const),
            pl.BlockSpec((ROW_TILE, 6 * 256), lambda i: (i % tps, 0)),
            pl.BlockSpec((4, 256), const),
            pl.BlockSpec((2, 1024), const),
            pl.BlockSpec((256, 256), const),
            pl.BlockSpec((256, 256), const),
        ],
        out_specs=[pl.BlockSpec((ROW_TILE, wd), row) for wd, _ in widths],
        out_shape=[jax.ShapeDtypeStruct((n, wd), dt) for wd, dt in widths],
        compiler_params=_cparams(("parallel",)),
        name="in_projection",
    )(xc, mod, g, w, rope, gains, lbt, ga, gc)


def _diff_attend(q, k, v, lam):
    lane = lax.broadcasted_iota(jnp.int32, q.shape, 1)
    n_maps = 2 * DIFF_HEADS

    def scores(g):
        return _dot_nt(jnp.where(lane // DIFF_HEAD_DIM == g, q, jnp.zeros_like(q)), k)

    out = jnp.zeros(q.shape, F32)
    parts = []
    s_next = scores(0)
    for g in range(n_maps):
        s = s_next
        if g + 1 < n_maps:
            s_next = scores(g + 1)
        e = jnp.exp2(s - jnp.max(s, axis=-1, keepdims=True)).astype(BF16)
        o = _dot(e, v[g // 2])
        parts.append(o * (1.0 / pltpu.roll(o, DIFF_WIDTH - DIFF_V_DIM, 1)))
        if g % 2 == 1:
            out = out + jnp.where(lane // DIFF_V_DIM == g // 2, parts[g - 1] - lam * parts[g], 0.0)
    return out


def _diff_kernel(q_ref, k_ref, v_ref, lam_ref, post_ref, g64_ref, o_ref, vaug_ref, *, n_ctx):
    j = pl.program_id(1)
    lam = lam_ref[0:1, 0:1]

    @pl.when(j == 0)
    def _():
        v = v_ref[...]
        lane = lax.broadcasted_iota(jnp.int32, v.shape, 1)
        for hd in range(DIFF_HEADS):
            vaug_ref[hd] = jnp.where(lane // DIFF_V_DIM == (hd + 1) % DIFF_HEADS, jnp.ones_like(v), v)

    def finish(o):
        msq = _group_mean_sq(o, g64_ref[...])
        o_ref[...] = (o * lax.rsqrt(msq + NORM_EPS) * post_ref[...]).astype(BF16)

    @pl.when(j == 0)
    def _():
        finish(_diff_attend(q_ref[...], k_ref[0:n_ctx, :], [vaug_ref[hd, 0:n_ctx, :] for hd in range(DIFF_HEADS)],
                            lam))

    @pl.when(j > 0)
    def _():
        finish(_diff_attend(q_ref[...], k_ref[...], [vaug_ref[hd] for hd in range(DIFF_HEADS)], lam))


def _diff_attention(qa, ka, va, lam_row, post_row, g64, n_batch, t_len, n_ctx):
    tps = t_len // ROW_TILE
    q3 = qa.reshape(n_batch, t_len, DIFF_WIDTH)
    k3 = ka.reshape(n_batch, t_len, DIFF_WIDTH)
    v3 = va.reshape(n_batch, t_len, DIFF_WIDTH)
    const = lambda b, j: (0, 0)
    out = pl.pallas_call(
        functools.partial(_diff_kernel, n_ctx=n_ctx),
        grid=(n_batch, tps),
        in_specs=[
            pl.BlockSpec((None, ROW_TILE, DIFF_WIDTH), lambda b, j: (b, j, 0)),
            pl.BlockSpec((None, t_len, DIFF_WIDTH), lambda b, j: (b, 0, 0)),
            pl.BlockSpec((None, t_len, DIFF_WIDTH), lambda b, j: (b, 0, 0)),
            pl.BlockSpec((1, 128), const),
            pl.BlockSpec((1, DIFF_WIDTH), const),
            pl.BlockSpec((256, 256), const),
        ],
        out_specs=pl.BlockSpec((None, ROW_TILE, DIFF_WIDTH), lambda b, j: (b, j, 0)),
        out_shape=jax.ShapeDtypeStruct((n_batch, t_len, DIFF_WIDTH), BF16),
        scratch_shapes=[pltpu.VMEM((DIFF_HEADS, t_len, DIFF_WIDTH), BF16)],
        compiler_params=_cparams(("parallel", "arbitrary")),
        name="diff_attention",
    )(q3, k3, v3, lam_row, post_row, g64)
    return out.reshape(n_batch * t_len, DIFF_WIDTH)


def _swa_kernel(q_ref, k_ref, v_ref, sink_ref, o_ref, *, n_ctx, n_lat):
    j = pl.program_id(1)
    q = q_ref[...]
    tq = q.shape[0]
    lane = lax.broadcasted_iota(jnp.int32, (tq, SWA_KV_WIDTH), 1)
    span = ROW_TILE + 2 * WINDOW
    heads = [(g, hd) for g in range(SWA_GROUP) for hd in range(SWA_KV_HEADS)]

    def run(k, v, valid):
        vlane = lax.broadcasted_iota(jnp.int32, v.shape, 1)
        vaug = [jnp.where(vlane // SWA_HEAD_DIM == hd, v, jnp.ones_like(v)) for hd in range(SWA_KV_HEADS)]
        sinks = [sink_ref[SWA_GROUP * hd + g:SWA_GROUP * hd + g + 1, 0:1] for g, hd in heads]
        scores = []
        for g, hd in heads:
            qg = q[:, SWA_KV_WIDTH * g:SWA_KV_WIDTH * (g + 1)]
            s = _dot_nt(jnp.where(lane // SWA_HEAD_DIM == hd, qg, jnp.zeros_like(qg)), k)
            scores.append(s if valid is None else jnp.where(valid, s, NEG_BIG))
        maxes = [jnp.maximum(jnp.max(s, axis=-1, keepdims=True), sink) for s, sink in zip(scores, sinks)]
        outs = [_dot(jnp.exp2(s - mx).astype(BF16), vaug[hd]) for s, mx, (g, hd) in zip(scores, maxes, heads)]
        for g in range(SWA_GROUP):
            out = jnp.zeros((tq, SWA_KV_WIDTH), F32)
            for i, (gi, hd) in enumerate(heads):
                if gi == g:
                    den = pltpu.roll(outs[i], SWA_HEAD_DIM, 1) + jnp.exp2(sinks[i] - maxes[i])
                    out = out + jnp.where(lane // SWA_HEAD_DIM == hd, outs[i] * (1.0 / den), 0.0)
            o_ref[:, SWA_KV_WIDTH * g:SWA_KV_WIDTH * (g + 1)] = out.astype(BF16)

    @pl.when(j == 0)
    def _():
        run(k_ref[0:n_ctx, :], v_ref[0:n_ctx, :], None)

    @pl.when(j > 0)
    def _():
        q0 = (j - 1) * ROW_TILE
        ks = jnp.clip(q0 - WINDOW, 0, n_lat - span)
        ks = pl.multiple_of(ks, WINDOW)
        k = jnp.concatenate([k_ref[0:n_ctx, :], k_ref[pl.ds(n_ctx + ks, span), :]], axis=0)
        v = jnp.concatenate([v_ref[0:n_ctx, :], v_ref[pl.ds(n_ctx + ks, span), :]], axis=0)
        qpos = q0 + lax.broadcasted_iota(jnp.int32, (tq, n_ctx + span), 0)
        kpos = ks - n_ctx + lax.broadcasted_iota(jnp.int32, (tq, n_ctx + span), 1)
        run(k, v, (kpos < ks) | (jnp.abs(qpos - kpos) <= WINDOW))


def _swa_attention(qc, kc, vc, sink_tab, n_batch, t_len, n_ctx):
    tps = t_len // ROW_TILE
    q3 = qc.reshape(n_batch, t_len, SWA_WIDTH)
    k3 = kc.reshape(n_batch, t_len, SWA_KV_WIDTH)
    v3 = vc.reshape(n_batch, t_len, SWA_KV_WIDTH)
    out = pl.pallas_call(
        functools.partial(_swa_kernel, n_ctx=n_ctx, n_lat=t_len - n_ctx),
        grid=(n_batch, tps),
        in_specs=[
            pl.BlockSpec((None, ROW_TILE, SWA_WIDTH), lambda b, j: (b, j, 0)),
            pl.BlockSpec((None, t_len, SWA_KV_WIDTH), lambda b, j: (b, 0, 0)),
            pl.BlockSpec((None, t_len, SWA_KV_WIDTH), lambda b, j: (b, 0, 0)),
            pl.BlockSpec((SWA_Q_HEADS, 128), lambda b, j: (0, 0)),
        ],
        out_specs=pl.BlockSpec((None, ROW_TILE, SWA_WIDTH), lambda b, j: (b, j, 0)),
        out_shape=jax.ShapeDtypeStruct((n_batch, t_len, SWA_WIDTH), BF16),
        compiler_params=_cparams(("parallel", "arbitrary")),
        name="swa_attention",
    )(q3, k3, v3, sink_tab)
    return out.reshape(n_batch * t_len, SWA_WIDTH)


def _hgrn_kernel(q_ref, v_ref, lff_ref, lfb_ref, g_ref, ng_ref, o_ref, of_ref, ob_ref, st_ref, cum_ref, qrow_ref,
                 strong_ref, *, n_ctx, t_len):
    c = HGRN_CHUNK
    n_chunks = t_len // c
    n_ctx_chunks = n_ctx // c
    mid = c // 2
    ri = lax.broadcasted_iota(jnp.int32, (c, c), 0)
    ci = lax.broadcasted_iota(jnp.int32, (c, c), 1)
    causal = ri >= ci
    anti = ri <= ci
    tri_f = jnp.where(causal, 1.0, 0.0).astype(BF16)
    tri_b = jnp.where(anti, 1.0, 0.0).astype(BF16)

    st_ref[...] = jnp.zeros_like(st_ref)
    n_iter = n_chunks // HGRN_UNROLL

    def chains_of(i):
        chains = []
        for u in range(HGRN_UNROLL):
            step = i * HGRN_UNROLL + u
            rf = pl.multiple_of(step * c, c)
            cb = jnp.where(step < n_ctx_chunks, n_ctx_chunks - 1 - step, n_chunks - 1 + n_ctx_chunks - step)
            rb = pl.multiple_of(cb * c, c)
            for hd in range(HGRN_HEADS):
                cols = slice(HGRN_K * hd, HGRN_K * (hd + 1))
                chains.append((2 * hd, pl.ds(rf, c), cols, lff_ref, of_ref, tri_f, causal, mid - 1, c - 1))
                chains.append((2 * hd + 1, pl.ds(rb, c), cols, lfb_ref, ob_ref, tri_b, anti, mid, 0))
        return chains

    def max_half_decay(i):
        worst = jnp.zeros((1, HGRN_K), F32)
        for _, rows, cols, lf_ref, _, _, _, _, _ in chains_of(i):
            lf = lf_ref[rows, cols]
            worst = jnp.maximum(worst, jnp.maximum(jnp.abs(jnp.sum(lf[0:mid, :], axis=0, keepdims=True)),
                                                   jnp.abs(jnp.sum(lf[mid:c, :], axis=0, keepdims=True))))
        return jnp.max(worst)

    def exact_step(i):
        for slot, rows, cols, lf_ref, o_ref_d, tri, mask, _, end_row in chains_of(i):
            lf = lf_ref[rows, cols]
            lf_hi, lf_lo = _split_bf16(lf)
            cum = _dot(tri, lf_hi) + _dot(tri, lf_lo)
            tot = cum[end_row:end_row + 1, :]
            v = v_ref[rows, cols]
            k = 1.0 - jnp.exp(lf)
            q = q_ref[rows, cols].astype(F32)
            cum_ref[...] = cum
            qrow_ref[...] = q

            def row(t, sct):
                w = jnp.exp(jnp.minimum(cum_ref[pl.ds(t, 1), :] - cum, 0.0))
                col = jnp.sum(qrow_ref[pl.ds(t, 1), :] * k * w, axis=-1, keepdims=True)
                return jnp.where(ci == t, col, sct)

            sct = lax.fori_loop(0, c, row, jnp.zeros((c, c), F32))
            valid_t = anti if mask is causal else causal
            st = st_ref[slot]
            o = (_dot_tn(jnp.where(valid_t, sct, 0.0).astype(BF16), v)
                 + _dot_nt((q * jnp.exp(cum)).astype(BF16), st.astype(BF16)))
            o_ref_d[rows, cols] = o.astype(BF16)
            st_ref[slot] = st * jnp.exp(tot) + _dot_tn(v, (k * jnp.exp(tot - cum)).astype(BF16))

    def fast_step(i):
        chains = chains_of(i)
        cums = []
        for _, rows, cols, lf_ref, _, tri, _, _, _ in chains:
            lf_hi, lf_lo = _split_bf16(lf_ref[rows, cols])
            cums.append(_dot(tri, lf_hi) + _dot(tri, lf_lo))
        prods = []
        for (slot, rows, cols, lf_ref, _, _, mask, ref_row, end_row), cum in zip(chains, cums):
            ref = cum[ref_row:ref_row + 1, :]
            tot = cum[end_row:end_row + 1, :]
            v = v_ref[rows, cols]
            k = 1.0 - jnp.exp(lf_ref[rows, cols])
            qt = q_ref[rows, cols].astype(F32) * jnp.exp(cum - ref)
            kt = k * jnp.exp(ref - cum)
            sc = _dot_nt(qt.astype(BF16), kt.astype(BF16))
            upd = _dot_tn(v, (kt * jnp.exp(tot - ref)).astype(BF16))
            prods.append((sc, (qt * jnp.exp(ref)).astype(BF16), jnp.exp(tot), upd, v))
        states = {}
        for (slot, rows, cols, _, o_ref_d, _, mask, _, _), (sc, q_in, decay, upd, v) in zip(chains, prods):
            st = states[slot] if slot in states else st_ref[slot]
            o = _dot(jnp.where(mask, sc, 0.0).astype(BF16), v) + _dot_nt(q_in, st.astype(BF16))
            o_ref_d[rows, cols] = o.astype(BF16)
            states[slot] = st * decay + upd
        for slot, st in states.items():
            st_ref[slot] = st

    def test_decay(i):
        strong_ref[0] = (max_half_decay(i) > HGRN_SAFE_DECAY).astype(jnp.int32)

    def body(i, _):
        strong = strong_ref[0]

        @pl.when(strong == 0)
        def _():
            test_decay(jnp.minimum(i + 1, n_iter - 1))
            fast_step(i)

        @pl.when(strong != 0)
        def _():
            test_decay(jnp.minimum(i + 1, n_iter - 1))
            exact_step(i)

        return 0

    test_decay(0)
    lax.fori_loop(0, n_iter, body, 0)

    def fin(t, _):
        r = pl.multiple_of(t * ROW_TILE, ROW_TILE)
        for hd in range(HGRN_HEADS):
            cols = slice(HGRN_K * hd, HGRN_K * (hd + 1))
            o = of_ref[pl.ds(r, ROW_TILE), cols].astype(F32) + ob_ref[pl.ds(r, ROW_TILE), cols].astype(F32)
            ms = jnp.mean(o * o, axis=-1, keepdims=True)
            o = o * lax.rsqrt(ms + NORM_EPS) * ng_ref[...]
            o_ref[pl.ds(r, ROW_TILE), cols] = (o * g_ref[pl.ds(r, ROW_TILE), cols].astype(F32)).astype(BF16)
        return 0

    lax.fori_loop(0, t_len // ROW_TILE, fin, 0)


def _hgrn(hq, hv, lff, lfb, hg, norm_g, n_batch, t_len, n_ctx):
    def view(a):
        return a.reshape(n_batch, t_len, HGRN_WIDTH)

    blk = pl.BlockSpec((None, t_len, HGRN_WIDTH), lambda b: (b, 0, 0))
    out = pl.pallas_call(
        functools.partial(_hgrn_kernel, n_ctx=n_ctx, t_len=t_len),
        grid=(n_batch,),
        in_specs=[blk] * 5 + [pl.BlockSpec((1, HGRN_K), lambda b: (0, 0))],
        out_specs=blk,
        out_shape=jax.ShapeDtypeStruct((n_batch, t_len, HGRN_WIDTH), BF16),
        scratch_shapes=[pltpu.VMEM((t_len, HGRN_WIDTH), BF16), pltpu.VMEM((t_len, HGRN_WIDTH), BF16),
                        pltpu.VMEM((2 * HGRN_HEADS, HGRN_K, HGRN_K), F32),
                        pltpu.VMEM((HGRN_CHUNK, HGRN_K), F32), pltpu.VMEM((HGRN_CHUNK, HGRN_K), F32),
                        pltpu.SMEM((1,), jnp.int32)],
        compiler_params=_cparams(("parallel",)),
        name="hgrn2",
    )(view(hq), view(hv), view(lff), view(lfb), view(hg), norm_g)
    return out.reshape(n_batch * t_len, HGRN_WIDTH)


def _mix_residual_norm(x_ref, a_ref, b_ref, c_ref, w_ref, mod_ref, g_ref):
    y = (_dot(a_ref[...], w_ref[0:256, :]) + _dot(b_ref[...], w_ref[256:768, :])
         + _dot(c_ref[...], w_ref[768:1024, :]))
    x = x_ref[...] + mod_ref[0:1, :] * y
    ms = jnp.mean(x * x, axis=-1, keepdims=True)
    h = (x * lax.rsqrt(ms + NORM_EPS) * g_ref[...]) * (1.0 + mod_ref[2:3, :]) + mod_ref[1:2, :]
    return x, h


def _outproj_router_kernel(x_ref, a_ref, b_ref, c_ref, w_ref, mod_ref, g_ref, r_ref, xo_ref, h_ref, gate_ref):
    x, h = _mix_residual_norm(x_ref, a_ref, b_ref, c_ref, w_ref, mod_ref, g_ref)
    xo_ref[...] = x
    h_ref[...] = _pack_bf16_pairs(h)
    h_hi, h_lo = _split_bf16(h)
    r_hi, r_lo = _split_bf16(r_ref[...])
    logits = _dot(h_hi, r_hi) + _dot(h_lo, r_hi) + _dot(h_hi, r_lo)
    lane = lax.broadcasted_iota(jnp.int32, logits.shape, 1).astype(F32)
    logits = jnp.where(lane < N_EXPERTS, logits, NEG_BIG)
    v1 = jnp.max(logits, axis=-1, keepdims=True)
    i1 = jnp.min(jnp.where(logits == v1, lane, 128.0), axis=-1, keepdims=True)
    rest_l = jnp.where(lane == i1, NEG_BIG, logits)
    v2 = jnp.max(rest_l, axis=-1, keepdims=True)
    i2 = jnp.min(jnp.where(rest_l == v2, lane, 128.0), axis=-1, keepdims=True)
    e2 = jnp.exp(v2 - v1)
    w1 = 1.0 / (1.0 + e2)
    w2 = e2 * w1
    gate_ref[...] = (jnp.where(lane == i1, w1, 0.0) + jnp.where(lane == i2, w2, 0.0)
                     + jnp.where(lane == 8.0, i1, 0.0) + jnp.where(lane == 9.0, i2, 0.0)
                     + jnp.where(lane == 10.0, w1, 0.0) + jnp.where(lane == 11.0, w2, 0.0))


def _out_projection_router(xc, a, b, c, w, mod, g, router):
    n = xc.shape[0]
    nt = n // ROW_TILE
    row = lambda i: (i, 0)
    const = lambda i: (0, 0)
    return pl.pallas_call(
        _outproj_router_kernel,
        grid=(nt,),
        in_specs=[
            pl.BlockSpec((ROW_TILE, D_MODEL), row),
            pl.BlockSpec((ROW_TILE, DIFF_WIDTH), row),
            pl.BlockSpec((ROW_TILE, HGRN_WIDTH), row),
            pl.BlockSpec((ROW_TILE, SWA_WIDTH), row),
            pl.BlockSpec((D_MODEL, D_MODEL), const),
            pl.BlockSpec((None, 3, D_MODEL), lambda i: (i, 0, 0)),
            pl.BlockSpec((1, D_MODEL), const),
            pl.BlockSpec((D_MODEL, 128), const),
        ],
        out_specs=[pl.BlockSpec((ROW_TILE, D_MODEL), row), pl.BlockSpec((ROW_TILE, D_MODEL // 2), row),
                   pl.BlockSpec((ROW_TILE, 128), row)],
        out_shape=[jax.ShapeDtypeStruct((n, D_MODEL), F32), jax.ShapeDtypeStruct((n, D_MODEL // 2), jnp.int32),
                   jax.ShapeDtypeStruct((n, 128), F32)],
        compiler_params=_cparams(("parallel",)),
        name="out_projection",
    )(xc, a, b, c, w, mod, g, router)


def _outproj_ffn_kernel(x_ref, a_ref, b_ref, c_ref, wo_ref, mod_ref, g_ref, w1_ref, w3_ref, w2_ref, o_ref):
    x, h = _mix_residual_norm(x_ref, a_ref, b_ref, c_ref, wo_ref, mod_ref, g_ref)
    h = h.astype(BF16)
    u = _dot(h, w1_ref[...])
    act = (_silu(u) * _dot(h, w3_ref[...])).astype(BF16)
    o_ref[...] = x + mod_ref[3:4, :] * _dot(act, w2_ref[...])


def _outproj_dense_ffn(xc, a, b, c, wo, mod, g, w1, w3, w2):
    n = xc.shape[0]
    nt = n // ROW_TILE
    row = lambda i: (i, 0)
    const = lambda i: (0, 0)
    return pl.pallas_call(
        _outproj_ffn_kernel,
        grid=(nt,),
        in_specs=[
            pl.BlockSpec((ROW_TILE, D_MODEL), row),
            pl.BlockSpec((ROW_TILE, DIFF_WIDTH), row),
            pl.BlockSpec((ROW_TILE, HGRN_WIDTH), row),
            pl.BlockSpec((ROW_TILE, SWA_WIDTH), row),
            pl.BlockSpec((D_MODEL, D_MODEL), const),
            pl.BlockSpec((None, 4, D_MODEL), lambda i: (i, 0, 0)),
            pl.BlockSpec((1, D_MODEL), const),
            pl.BlockSpec((D_MODEL, D_FF), const),
            pl.BlockSpec((D_MODEL, D_FF), const),
            pl.BlockSpec((D_FF, D_MODEL), const),
        ],
        out_specs=pl.BlockSpec((ROW_TILE, D_MODEL), row),
        out_shape=jax.ShapeDtypeStruct((n, D_MODEL), F32),
        compiler_params=_cparams(("parallel",)),
        name="outproj_dense_ffn",
    )(xc, a, b, c, wo, mod, g, w1, w3, w2)


MOE_TM = 256
SC_CORES = 2
SC_SUBCORES = 16
SC_GATHER_ROWS = 64


def _pack_bf16_pairs(h):
    half = h.shape[1] // 2
    bits = pltpu.bitcast(h.astype(BF16).astype(F32), jnp.uint32)
    packed = (bits[:, :half] >> 16) | (bits[:, half:] & jnp.uint32(0xFFFF0000))
    return pltpu.bitcast(packed, jnp.int32)


def _unpack_bf16_pairs(u):
    bits = pltpu.bitcast(u, jnp.uint32)
    lo = pltpu.bitcast(bits << 16, F32)
    hi = pltpu.bitcast(bits & jnp.uint32(0xFFFF0000), F32)
    return jnp.concatenate([lo, hi], axis=1)


def _sc_gather_rows(table, idx):
    n_workers = SC_CORES * SC_SUBCORES
    n_rows, width = idx.shape[0], table.shape[1]
    assert n_rows % (n_workers * SC_GATHER_ROWS) == 0
    per_worker = n_rows // n_workers
    mesh = plsc.VectorSubcoreMesh(core_axis_name="c", subcore_axis_name="s",
                                  num_cores=SC_CORES, num_subcores=SC_SUBCORES)

    @functools.partial(
        pl.kernel, mesh=mesh,
        out_type=jax.ShapeDtypeStruct((n_rows, width), table.dtype),
        scratch_types=[pltpu.VMEM((SC_GATHER_ROWS,), jnp.int32),
                       pltpu.VMEM((SC_GATHER_ROWS, width), table.dtype),
                       pltpu.SemaphoreType.DMA],
        name="sc_gather_rows",
    )
    def gather(table_hbm, idx_hbm, out_hbm, idx_v, rows_v, sem):
        worker = lax.axis_index("s") * SC_CORES + lax.axis_index("c")
        base = worker * per_worker

        @pl.loop(0, per_worker // SC_GATHER_ROWS)
        def _(step):
            off = pl.multiple_of(base + step * SC_GATHER_ROWS, SC_GATHER_ROWS)
            pltpu.sync_copy(idx_hbm.at[pl.ds(off, SC_GATHER_ROWS)], idx_v)
            pltpu.async_copy(table_hbm.at[idx_v], rows_v, sem).wait()
            pltpu.sync_copy(rows_v, out_hbm.at[pl.ds(off, SC_GATHER_ROWS)])

    return gather(table, idx)


def _moe_expert_kernel(te_ref, tv_ref, xs_ref, w1_ref, w3_ref, w2_ref, ys_ref):
    r = pl.program_id(0)

    @pl.when(tv_ref[r] == 1)
    def _():
        h = _unpack_bf16_pairs(xs_ref[...]).astype(BF16)
        u = _dot(h, w1_ref[...])
        act = (_silu(u) * _dot(h, w3_ref[...])).astype(BF16)
        ys_ref[...] = _pack_bf16_pairs(_dot(act, w2_ref[...]))

    @pl.when(tv_ref[r] == 0)
    def _():
        ys_ref[...] = jnp.zeros_like(ys_ref)


def _moe_combine_kernel(x_ref, y0_ref, y1_ref, route_ref, mod_ref, o_ref):
    y = (route_ref[:, 10:11] * _unpack_bf16_pairs(y0_ref[...])
         + route_ref[:, 11:12] * _unpack_bf16_pairs(y1_ref[...]))
    o_ref[...] = x_ref[...] + mod_ref[...] * y


def _moe_plan(route, n_rt):
    n = route.shape[0]
    tm = MOE_TM
    gates_t = route[:, :N_EXPERTS].T
    sel = gates_t != 0.0
    si = sel.astype(jnp.int32)
    rank = jnp.cumsum(si, axis=1) - si
    ntile = (jnp.sum(si, axis=1) + tm - 1) // tm
    tile_end = jnp.cumsum(ntile)
    tile_off = tile_end - ntile
    used = tile_end[-1]
    n_rows = n_rt * tm
    dest = jnp.where(sel, tile_off[:, None] * tm + rank, n_rows)
    token = jnp.broadcast_to(jnp.arange(n, dtype=jnp.int32)[None, :], dest.shape)
    src = jnp.zeros((n_rows,), jnp.int32).at[dest.reshape(-1)].set(token.reshape(-1), mode="drop")
    expert = jnp.arange(N_EXPERTS, dtype=jnp.int32)[:, None]

    def row_of(lane):
        pick = route[:, lane].astype(jnp.int32)[None, :]
        row = jnp.sum(jnp.where(expert == pick, dest, 0), axis=0)
        return jnp.where(row >= n_rows, 0, row)

    pos = jnp.concatenate([row_of(8), row_of(9)]).astype(jnp.int32)
    r = jnp.arange(n_rt, dtype=jnp.int32)
    rc = jnp.minimum(r, used - 1)
    tile_expert = jnp.sum((tile_end[None, :] <= rc[:, None]).astype(jnp.int32), axis=1)
    tile_valid = (r < used).astype(jnp.int32)
    return src, pos, tile_expert, tile_valid


def _moe_ffn(x, h_packed, route, w1, w3, w2, layer_idx, mod):
    n = x.shape[0]
    nt = n // ROW_TILE
    tm = MOE_TM
    half = D_MODEL // 2
    n_rt = 2 * n // tm + N_EXPERTS
    src, pos, tile_expert, tile_valid = _moe_plan(route, n_rt)

    xs = _sc_gather_rows(h_packed, src)
    expert_w = lambda r, te, tv: (layer_idx, te[r], 0, 0)
    ys = pl.pallas_call(
        _moe_expert_kernel,
        grid_spec=pltpu.PrefetchScalarGridSpec(
            num_scalar_prefetch=2,
            grid=(n_rt,),
            in_specs=[
                pl.BlockSpec((tm, half), lambda r, te, tv: (r, 0)),
                pl.BlockSpec((None, None, D_MODEL, D_FF), expert_w),
                pl.BlockSpec((None, None, D_MODEL, D_FF), expert_w),
                pl.BlockSpec((None, None, D_FF, D_MODEL), expert_w),
            ],
            out_specs=pl.BlockSpec((tm, half), lambda r, te, tv: (r, 0)),
        ),
        out_shape=jax.ShapeDtypeStruct((n_rt * tm, half), jnp.int32),
        compiler_params=_cparams(("arbitrary",)),
        name="moe_experts",
    )(tile_expert, tile_valid, xs, w1, w3, w2)

    y = _sc_gather_rows(ys, pos)
    row = lambda i: (i, 0)
    return pl.pallas_call(
        _moe_combine_kernel,
        grid=(nt,),
        in_specs=[
            pl.BlockSpec((ROW_TILE, D_MODEL), row),
            pl.BlockSpec((ROW_TILE, half), row),
            pl.BlockSpec((ROW_TILE, half), lambda i: (nt + i, 0)),
            pl.BlockSpec((ROW_TILE, 128), row),
            pl.BlockSpec((None, 1, D_MODEL), lambda i: (i, 0, 0)),
        ],
        out_specs=pl.BlockSpec((ROW_TILE, D_MODEL), row),
        out_shape=jax.ShapeDtypeStruct((n, D_MODEL), F32),
        compiler_params=_cparams(("parallel",)),
        name="moe_combine",
    )(x, y, y, route, mod)


def _rope_tables(n_ctx, n_lat):
    pos_r = jnp.arange(n_lat, dtype=jnp.int32) // GRID_W
    pos_c = jnp.arange(n_lat, dtype=jnp.int32) % GRID_W

    def per_head(head_dim):
        nf = head_dim // 4
        inv = ROPE_BASE ** (-jnp.arange(nf, dtype=F32) / nf)
        ang_r = pos_r.astype(F32)[:, None] * inv[None, :]
        ang_c = pos_c.astype(F32)[:, None] * inv[None, :]
        z = jnp.zeros_like(ang_r)
        cos = jnp.concatenate([jnp.cos(ang_r)] * 2 + [jnp.cos(ang_c)] * 2, axis=-1)
        sa = jnp.concatenate([-jnp.sin(ang_r), z, -jnp.sin(ang_c), z], axis=-1)
        sb = jnp.concatenate([z, jnp.sin(ang_r), z, jnp.sin(ang_c)], axis=-1)
        reps = 256 // head_dim
        tabs = [jnp.tile(t, (1, reps)) for t in (cos, sa, sb)]
        ctx = [jnp.ones((n_ctx, 256), F32), jnp.zeros((n_ctx, 256), F32), jnp.zeros((n_ctx, 256), F32)]
        return [jnp.concatenate([c, t], axis=0) for c, t in zip(ctx, tabs)]

    return jnp.concatenate(per_head(DIFF_HEAD_DIM) + per_head(SWA_HEAD_DIM), axis=-1)


def _block_diag_mean(group):
    idx = jnp.arange(256) // group
    return jnp.where(idx[:, None] == idx[None, :], 1.0 / group, 0.0).astype(BF16)


def kernel(x, c, ctx, c_ctx, ada_w, ada_b, norm_mix_g, norm_ffn_g, w_in, w_out, diff_qk_norm_g, diff_lambda,
           diff_subln_g, hgrn_lb_logits, hgrn_norm_g, swa_qk_norm_g, swa_sink, ffn_w1, ffn_w3, ffn_w2,
           moe_router, moe_w1, moe_w3, moe_w2):
    n_batch, n_lat, _ = x.shape
    n_ctx = ctx.shape[1]
    depth = ada_w.shape[0]
    t_len = n_ctx + n_lat
    tps = t_len // ROW_TILE
    assert n_ctx == ROW_TILE and n_lat % ROW_TILE == 0 and n_lat >= ROW_TILE + 2 * WINDOW

    xc = jnp.concatenate([ctx, x], axis=1).reshape(n_batch * t_len, D_MODEL)

    n_rows = -(-(n_batch + 1) // 8) * 8
    cond = jnp.concatenate([c, c_ctx[None, :], jnp.zeros((n_rows - n_batch - 1, D_MODEL), F32)], axis=0)
    mods = _ada_modulation(cond, ada_w, ada_b).reshape(depth, n_rows, 6, D_MODEL)
    m_lat = jnp.broadcast_to(mods[:, :n_batch, None], (depth, n_batch, tps - 1, 6, D_MODEL))
    m_ctx = jnp.broadcast_to(mods[:, n_batch, None, None], (depth, n_batch, 1, 6, D_MODEL))
    mods = jnp.concatenate([m_ctx, m_lat], axis=2).reshape(depth, n_batch * tps, 6, D_MODEL)

    lb = jnp.cumsum(jax.nn.softmax(hgrn_lb_logits.astype(F32), axis=1), axis=1)
    lb = lb - lb[:, :1]
    rope = _rope_tables(n_ctx, n_lat)
    g32 = _block_diag_mean(DIFF_HEAD_DIM)
    g64 = _block_diag_mean(SWA_HEAD_DIM)

    perm_q = jnp.arange(SWA_WIDTH).reshape(SWA_KV_HEADS, SWA_GROUP, SWA_HEAD_DIM).transpose(1, 0, 2).reshape(-1)
    qc0 = 3 * DIFF_WIDTH + 5 * HGRN_WIDTH
    col_perm = jnp.concatenate([jnp.arange(qc0), qc0 + perm_q, jnp.arange(qc0 + SWA_WIDTH, D_IN)])
    oc0 = DIFF_WIDTH + HGRN_WIDTH
    row_perm = jnp.concatenate([jnp.arange(oc0), oc0 + perm_q])
    moe_w1_b, moe_w3_b, moe_w2_b = moe_w1.astype(BF16), moe_w3.astype(BF16), moe_w2.astype(BF16)

    for layer in range(depth):
        lam_init = 0.8 - 0.6 * math.exp(-0.3 * layer)
        mod = mods[layer]
        w_in_l = w_in[layer][:, col_perm].astype(BF16)
        w_out_l = w_out[layer][row_perm, :].astype(BF16)
        gains = jnp.stack([
            jnp.tile(diff_qk_norm_g[layer, 0], 8) * (DIFF_HEAD_DIM ** -0.5 * LOG2_E),
            jnp.tile(diff_qk_norm_g[layer, 1], 8),
            jnp.tile(swa_qk_norm_g[layer, 0], 4) * (SWA_HEAD_DIM ** -0.5 * LOG2_E),
            jnp.tile(swa_qk_norm_g[layer, 1], 4),
        ]).astype(F32)
        lbt = jnp.stack([
            jnp.log(lb[:, layer]).reshape(-1),
            jnp.log1p(-lb[:, layer]).reshape(-1),
        ]).astype(F32)
        lv = diff_lambda[layer].astype(F32)
        lam = jnp.exp(jnp.sum(lv[0] * lv[1])) - jnp.exp(jnp.sum(lv[2] * lv[3])) + lam_init
        lam_row = jnp.full((1, 128), lam, F32)
        post_row = (jnp.tile(diff_subln_g[layer], DIFF_HEADS) * (1.0 - lam_init)).reshape(1, DIFF_WIDTH).astype(F32)
        sink_tab = jnp.broadcast_to(swa_sink[layer].astype(F32)[:, None] * LOG2_E, (SWA_Q_HEADS, 128))

        (qa, ka, va, hq, hv, lff, lfb, hg, qc, kc, vc) = _in_projection(
            xc, mod[:, 0:2], norm_mix_g[layer].reshape(1, D_MODEL), w_in_l, rope, gains, lbt, g32, g64, tps)
        a = _diff_attention(qa, ka, va, lam_row, post_row, g64, n_batch, t_len, n_ctx)
        b = _hgrn(hq, hv, lff, lfb, hg, hgrn_norm_g[layer].reshape(1, HGRN_K), n_batch, t_len, n_ctx)
        cc = _swa_attention(qc, kc, vc, sink_tab, n_batch, t_len, n_ctx)

        jj = layer // 2
        g2 = norm_ffn_g[layer].reshape(1, D_MODEL)
        if layer % 2 == 1:
            router = jnp.pad(moe_router[jj].astype(F32), ((0, 0), (0, 128 - N_EXPERTS)))
            x_mid, h2, route = _out_projection_router(xc, a, b, cc, w_out_l, mod[:, 2:5], g2, router)
            if layer == depth - 1:
                is_lat = (jnp.arange(n_batch * t_len, dtype=jnp.int32) % t_len) >= n_ctx
                route = jnp.where(is_lat[:, None], route, 0.0)
            xc = _moe_ffn(x_mid, h2, route, moe_w1_b, moe_w3_b, moe_w2_b, jj, mod[:, 5:6])
        else:
            xc = _outproj_dense_ffn(xc, a, b, cc, w_out_l, mod[:, 2:6], g2, ffn_w1[jj].astype(BF16),
                                    ffn_w3[jj].astype(BF16), ffn_w2[jj].astype(BF16))

    return xc.reshape(n_batch, t_len, D_MODEL)[:, n_ctx:, :]
```

```python
import functools
import math

import jax
import jax.numpy as jnp
from jax import lax
from jax.experimental import pallas as pl
from jax.experimental.pallas import tpu as pltpu
from jax.experimental.pallas import tpu_sc as plsc

D_MODEL = 1024
GRID_W = 64
DIFF_HEADS = 4
DIFF_HEAD_DIM = 32
DIFF_V_DIM = 64
DIFF_WIDTH = 256
HGRN_HEADS = 4
HGRN_K = 128
HGRN_WIDTH = 512
SWA_Q_HEADS = 4
SWA_KV_HEADS = 2
SWA_GROUP = 2
SWA_HEAD_DIM = 64
SWA_WIDTH = 256
SWA_KV_WIDTH = 128
WINDOW = 128
D_FF = 2816
N_EXPERTS = 8
ROPE_BASE = 10000.0
NORM_EPS = 1e-6
D_IN = 3840

F32 = jnp.float32
BF16 = jnp.bfloat16

ROW_TILE = 256
HGRN_CHUNK = 64
HGRN_UNROLL = 4
HGRN_SAFE_DECAY = 80.0
NEG_BIG = -1e30
LOG2_E = 1.4426950408889634
VMEM_LIMIT = 56 * 1024 * 1024


def _cparams(sem):
    return pltpu.CompilerParams(dimension_semantics=sem, vmem_limit_bytes=VMEM_LIMIT)


def _split_bf16(v):
    hi = v.astype(BF16)
    lo = (v - hi.astype(F32)).astype(BF16)
    return hi, lo


def _dot(a, b):
    return jnp.dot(a, b, preferred_element_type=F32)


def _dot_nt(a, b):
    return lax.dot_general(a, b, (((1,), (1,)), ((), ())), preferred_element_type=F32)


def _dot_tn(a, b):
    return lax.dot_general(a, b, (((0,), (0,)), ((), ())), preferred_element_type=F32)


def _group_mean_sq(y, gmat):
    hi, lo = _split_bf16(y * y)
    return _dot(hi, gmat) + _dot(lo, gmat)


def _silu(v):
    return v * (1.0 / (1.0 + jnp.exp(-v)))


def _ada_kernel(s_ref, w_ref, b_ref, o_ref):
    s = s_ref[...]
    s = _silu(s)
    s_hi, s_lo = _split_bf16(s)
    w_hi, w_lo = _split_bf16(w_ref[...])
    o_ref[...] = _dot(s_hi, w_hi) + _dot(s_lo, w_hi) + _dot(s_hi, w_lo) + b_ref[...]


def _ada_modulation(cond, ada_w, ada_b):
    depth = ada_w.shape[0]
    r = cond.shape[0]
    nblk = 6 * D_MODEL // 1024
    return pl.pallas_call(
        _ada_kernel,
        grid=(depth, nblk),
        in_specs=[
            pl.BlockSpec((r, D_MODEL), lambda l, n: (0, 0)),
            pl.BlockSpec((None, D_MODEL, 1024), lambda l, n: (l, 0, n)),
            pl.BlockSpec((None, 1, 1024), lambda l, n: (l, 0, n)),
        ],
        out_specs=pl.BlockSpec((None, r, 1024), lambda l, n: (l, 0, n)),
        out_shape=jax.ShapeDtypeStruct((depth, r, 6 * D_MODEL), F32),
        compiler_params=_cparams(("parallel", "parallel")),
        name="ada_modulation",
    )(cond, ada_w, ada_b.reshape(depth, 1, 6 * D_MODEL))


def _rope(y, cos, sa, sb, half):
    w = y.shape[-1]
    fwd = pltpu.roll(y, w - half, 1)
    bwd = pltpu.roll(y, half, 1)
    return y * cos + fwd * sa + bwd * sb


def _inproj_kernel(x_ref, mod_ref, g_ref, w_ref, rope_ref, gains_ref, lbt_ref, ga_ref, gc_ref,
                   qa_ref, ka_ref, va_ref, hq_ref, hv_ref, lff_ref, lfb_ref, hg_ref,
                   qc_ref, kc_ref, vc_ref):
    x = x_ref[...]
    shift = mod_ref[0:1, :]
    scale = mod_ref[1:2, :]
    ms = jnp.mean(x * x, axis=-1, keepdims=True)
    h = (x * lax.rsqrt(ms + NORM_EPS) * g_ref[...]) * (1.0 + scale) + shift
    hb = h.astype(BF16)

    def proj(a, b):
        return _dot(hb, w_ref[:, a:b])

    def qk_prep(y, gmat, gain, cos, sa, sb, half):
        msq = _group_mean_sq(y, gmat)
        y = y * lax.rsqrt(msq + NORM_EPS) * gain
        return _rope(y, cos, sa, sb, half)

    ga = ga_ref[...]
    gc = gc_ref[...]
    ra = [rope_ref[:, 256 * i:256 * (i + 1)] for i in range(6)]
    qa = qk_prep(proj(0, 256), ga, gains_ref[0:1, :], ra[0], ra[1], ra[2], 8)
    qa_ref[...] = qa.astype(BF16)
    ka = qk_prep(proj(256, 512), ga, gains_ref[1:2, :], ra[0], ra[1], ra[2], 8)
    ka_ref[...] = ka.astype(BF16)
    va_ref[...] = proj(512, 768).astype(BF16)
    hq_ref[...] = _silu(proj(768, 1280)).astype(BF16)
    hv_ref[...] = proj(1280, 1792).astype(BF16)
    for d, lf_ref in enumerate((lff_ref, lfb_ref)):
        z = proj(1792 + 512 * d, 2304 + 512 * d)
        log_lb = lbt_ref[0:1, 512 * d:512 * (d + 1)]
        log1m_lb = lbt_ref[1:2, 512 * d:512 * (d + 1)]
        sp = jnp.maximum(-z, 0.0) + jnp.log(1.0 + jnp.exp(-jnp.abs(z)))
        b2 = log1m_lb - sp
        mx = jnp.maximum(log_lb, b2)
        lf_ref[...] = mx + jnp.log(1.0 + jnp.exp(-jnp.abs(log_lb - b2)))
    hg_ref[...] = _silu(proj(2816, 3328)).astype(BF16)
    qc = qk_prep(proj(3328, 3584), gc, gains_ref[2:3, :], ra[3], ra[4], ra[5], 16)
    qc_ref[...] = qc.astype(BF16)
    kc = qk_prep(proj(3584, 3712), gc[0:128, 0:128], gains_ref[3:4, 0:128],
                 ra[3][:, 0:128], ra[4][:, 0:128], ra[5][:, 0:128], 16)
    kc_ref[...] = kc.astype(BF16)
    vc_ref[...] = proj(3712, 3840).astype(BF16)


def _in_projection(xc, mod, g, w, rope, gains, lbt, ga, gc, tiles_per_seq):
    n = xc.shape[0]
    nt = n // ROW_TILE
    tps = tiles_per_seq

    def row(i):
        return (i, 0)

    def mod_idx(i):
        return (i, 0, 0)

    const = lambda i: (0, 0)
    widths = [(256, BF16), (256, BF16), (256, BF16), (512, BF16), (512, BF16),
              (512, F32), (512, F32), (512, BF16), (256, BF16), (128, BF16), (128, BF16)]
    return pl.pallas_call(
        _inproj_kernel,
        grid=(nt,),
        in_specs=[
            pl.BlockSpec((ROW_TILE, D_MODEL), row),
            pl.BlockSpec((None, 2, D_MODEL), mod_idx),
            pl.BlockSpec((1, D_MODEL), const),
            pl.BlockSpec((D_MODEL, D_IN), const),
            pl.BlockSpec((ROW_TILE, 6 * 256), lambda i: (i % tps, 0)),
            pl.BlockSpec((4, 256), const),
            pl.BlockSpec((2, 1024), const),
            pl.BlockSpec((256, 256), const),
            pl.BlockSpec((256, 256), const),
        ],
        out_specs=[pl.BlockSpec((ROW_TILE, wd), row) for wd, _ in widths],
        out_shape=[jax.ShapeDtypeStruct((n, wd), dt) for wd, dt in widths],
        compiler_params=_cparams(("parallel",)),
        name="in_projection",
    )(xc, mod, g, w, rope, gains, lbt, ga, gc)


def _diff_attend(q, k, v, lam):
    lane = lax.broadcasted_iota(jnp.int32, q.shape, 1)
    n_maps = 2 * DIFF_HEADS

    def scores(g):
        return _dot_nt(jnp.where(lane // DIFF_HEAD_DIM == g, q, jnp.zeros_like(q)), k)

    out = jnp.zeros(q.shape, F32)
    parts = []
    s_next = scores(0)
    for g in range(n_maps):
        s = s_next
        if g + 1 < n_maps:
            s_next = scores(g + 1)
        e = jnp.exp2(s - jnp.max(s, axis=-1, keepdims=True)).astype(BF16)
        o = _dot(e, v[g // 2])
        parts.append(o * (1.0 / pltpu.roll(o, DIFF_WIDTH - DIFF_V_DIM, 1)))
        if g % 2 == 1:
            out = out + jnp.where(lane // DIFF_V_DIM == g // 2, parts[g - 1] - lam * parts[g], 0.0)
    return out


def _diff_kernel(q_ref, k_ref, v_ref, lam_ref, post_ref, g64_ref, o_ref, vaug_ref, *, n_ctx):
    j = pl.program_id(1)
    lam = lam_ref[0:1, 0:1]

    @pl.when(j == 0)
    def _():
        v = v_ref[...]
        lane = lax.broadcasted_iota(jnp.int32, v.shape, 1)
        for hd in range(DIFF_HEADS):
            vaug_ref[hd] = jnp.where(lane // DIFF_V_DIM == (hd + 1) % DIFF_HEADS, jnp.ones_like(v), v)

    def finish(o):
        msq = _group_mean_sq(o, g64_ref[...])
        o_ref[...] = (o * lax.rsqrt(msq + NORM_EPS) * post_ref[...]).astype(BF16)

    @pl.when(j == 0)
    def _():
        finish(_diff_attend(q_ref[...], k_ref[0:n_ctx, :], [vaug_ref[hd, 0:n_ctx, :] for hd in range(DIFF_HEADS)],
                            lam))

    @pl.when(j > 0)
    def _():
        finish(_diff_attend(q_ref[...], k_ref[...], [vaug_ref[hd] for hd in range(DIFF_HEADS)], lam))


def _diff_attention(qa, ka, va, lam_row, post_row, g64, n_batch, t_len, n_ctx):
    tps = t_len // ROW_TILE
    q3 = qa.reshape(n_batch, t_len, DIFF_WIDTH)
    k3 = ka.reshape(n_batch, t_len, DIFF_WIDTH)
    v3 = va.reshape(n_batch, t_len, DIFF_WIDTH)
    const = lambda b, j: (0, 0)
    out = pl.pallas_call(
        functools.partial(_diff_kernel, n_ctx=n_ctx),
        grid=(n_batch, tps),
        in_specs=[
            pl.BlockSpec((None, ROW_TILE, DIFF_WIDTH), lambda b, j: (b, j, 0)),
            pl.BlockSpec((None, t_len, DIFF_WIDTH), lambda b, j: (b, 0, 0)),
            pl.BlockSpec((None, t_len, DIFF_WIDTH), lambda b, j: (b, 0, 0)),
            pl.BlockSpec((1, 128), const),
            pl.BlockSpec((1, DIFF_WIDTH), const),
            pl.BlockSpec((256, 256), const),
        ],
        out_specs=pl.BlockSpec((None, ROW_TILE, DIFF_WIDTH), lambda b, j: (b, j, 0)),
        out_shape=jax.ShapeDtypeStruct((n_batch, t_len, DIFF_WIDTH), BF16),
        scratch_shapes=[pltpu.VMEM((DIFF_HEADS, t_len, DIFF_WIDTH), BF16)],
        compiler_params=_cparams(("parallel", "arbitrary")),
        name="diff_attention",
    )(q3, k3, v3, lam_row, post_row, g64)
    return out.reshape(n_batch * t_len, DIFF_WIDTH)


def _swa_kernel(q_ref, k_ref, v_ref, sink_ref, o_ref, *, n_ctx, n_lat):
    j = pl.program_id(1)
    q = q_ref[...]
    tq = q.shape[0]
    lane = lax.broadcasted_iota(jnp.int32, (tq, SWA_KV_WIDTH), 1)
    span = ROW_TILE + 2 * WINDOW
    heads = [(g, hd) for g in range(SWA_GROUP) for hd in range(SWA_KV_HEADS)]

    def run(k, v, valid):
        vlane = lax.broadcasted_iota(jnp.int32, v.shape, 1)
        vaug = [jnp.where(vlane // SWA_HEAD_DIM == hd, v, jnp.ones_like(v)) for hd in range(SWA_KV_HEADS)]
        sinks = [sink_ref[SWA_GROUP * hd + g:SWA_GROUP * hd + g + 1, 0:1] for g, hd in heads]
        scores = []
        for g, hd in heads:
            qg = q[:, SWA_KV_WIDTH * g:SWA_KV_WIDTH * (g + 1)]
            s = _dot_nt(jnp.where(lane // SWA_HEAD_DIM == hd, qg, jnp.zeros_like(qg)), k)
            scores.append(s if valid is None else jnp.where(valid, s, NEG_BIG))
        maxes = [jnp.maximum(jnp.max(s, axis=-1, keepdims=True), sink) for s, sink in zip(scores, sinks)]
        outs = [_dot(jnp.exp2(s - mx).astype(BF16), vaug[hd]) for s, mx, (g, hd) in zip(scores, maxes, heads)]
        for g in range(SWA_GROUP):
            out = jnp.zeros((tq, SWA_KV_WIDTH), F32)
            for i, (gi, hd) in enumerate(heads):
                if gi == g:
                    den = pltpu.roll(outs[i], SWA_HEAD_DIM, 1) + jnp.exp2(sinks[i] - maxes[i])
                    out = out + jnp.where(lane // SWA_HEAD_DIM == hd, outs[i] * (1.0 / den), 0.0)
            o_ref[:, SWA_KV_WIDTH * g:SWA_KV_WIDTH * (g + 1)] = out.astype(BF16)

    @pl.when(j == 0)
    def _():
        run(k_ref[0:n_ctx, :], v_ref[0:n_ctx, :], None)

    @pl.when(j > 0)
    def _():
        q0 = (j - 1) * ROW_TILE
        ks = jnp.clip(q0 - WINDOW, 0, n_lat - span)
        ks = pl.multiple_of(ks, WINDOW)
        k = jnp.concatenate([k_ref[0:n_ctx, :], k_ref[pl.ds(n_ctx + ks, span), :]], axis=0)
        v = jnp.concatenate([v_ref[0:n_ctx, :], v_ref[pl.ds(n_ctx + ks, span), :]], axis=0)
        qpos = q0 + lax.broadcasted_iota(jnp.int32, (tq, n_ctx + span), 0)
        kpos = ks - n_ctx + lax.broadcasted_iota(jnp.int32, (tq, n_ctx + span), 1)
        run(k, v, (kpos < ks) | (jnp.abs(qpos - kpos) <= WINDOW))


def _swa_attention(qc, kc, vc, sink_tab, n_batch, t_len, n_ctx):
    tps = t_len // ROW_TILE
    q3 = qc.reshape(n_batch, t_len, SWA_WIDTH)
    k3 = kc.reshape(n_batch, t_len, SWA_KV_WIDTH)
    v3 = vc.reshape(n_batch, t_len, SWA_KV_WIDTH)
    out = pl.pallas_call(
        functools.partial(_swa_kernel, n_ctx=n_ctx, n_lat=t_len - n_ctx),
        grid=(n_batch, tps),
        in_specs=[
            pl.BlockSpec((None, ROW_TILE, SWA_WIDTH), lambda b, j: (b, j, 0)),
            pl.BlockSpec((None, t_len, SWA_KV_WIDTH), lambda b, j: (b, 0, 0)),
            pl.BlockSpec((None, t_len, SWA_KV_WIDTH), lambda b, j: (b, 0, 0)),
            pl.BlockSpec((SWA_Q_HEADS, 128), lambda b, j: (0, 0)),
        ],
        out_specs=pl.BlockSpec((None, ROW_TILE, SWA_WIDTH), lambda b, j: (b, j, 0)),
        out_shape=jax.ShapeDtypeStruct((n_batch, t_len, SWA_WIDTH), BF16),
        compiler_params=_cparams(("parallel", "arbitrary")),
        name="swa_attention",
    )(q3, k3, v3, sink_tab)
    return out.reshape(n_batch * t_len, SWA_WIDTH)


def _hgrn_kernel(q_ref, v_ref, lff_ref, lfb_ref, g_ref, ng_ref, o_ref, of_ref, ob_ref, st_ref, cum_ref, qrow_ref,
                 strong_ref, *, n_ctx, t_len):
    c = HGRN_CHUNK
    n_chunks = t_len // c
    n_ctx_chunks = n_ctx // c
    mid = c // 2
    ri = lax.broadcasted_iota(jnp.int32, (c, c), 0)
    ci = lax.broadcasted_iota(jnp.int32, (c, c), 1)
    causal = ri >= ci
    anti = ri <= ci
    tri_f = jnp.where(causal, 1.0, 0.0).astype(BF16)
    tri_b = jnp.where(anti, 1.0, 0.0).astype(BF16)

    st_ref[...] = jnp.zeros_like(st_ref)
    n_iter = n_chunks // HGRN_UNROLL

    def chains_of(i):
        chains = []
        for u in range(HGRN_UNROLL):
            step = i * HGRN_UNROLL + u
            rf = pl.multiple_of(step * c, c)
            cb = jnp.where(step < n_ctx_chunks, n_ctx_chunks - 1 - step, n_chunks - 1 + n_ctx_chunks - step)
            rb = pl.multiple_of(cb * c, c)
            for hd in range(HGRN_HEADS):
                cols = slice(HGRN_K * hd, HGRN_K * (hd + 1))
                chains.append((2 * hd, pl.ds(rf, c), cols, lff_ref, of_ref, tri_f, causal, mid - 1, c - 1))
                chains.append((2 * hd + 1, pl.ds(rb, c), cols, lfb_ref, ob_ref, tri_b, anti, mid, 0))
        return chains

    def max_half_decay(i):
        worst = jnp.zeros((1, HGRN_K), F32)
        for _, rows, cols, lf_ref, _, _, _, _, _ in chains_of(i):
            lf = lf_ref[rows, cols]
            worst = jnp.maximum(worst, jnp.maximum(jnp.abs(jnp.sum(lf[0:mid, :], axis=0, keepdims=True)),
                                                   jnp.abs(jnp.sum(lf[mid:c, :], axis=0, keepdims=True))))
        return jnp.max(worst)

    def exact_step(i):
        for slot, rows, cols, lf_ref, o_ref_d, tri, mask, _, end_row in chains_of(i):
            lf = lf_ref[rows, cols]
            lf_hi, lf_lo = _split_bf16(lf)
            cum = _dot(tri, lf_hi) + _dot(tri, lf_lo)
            tot = cum[end_row:end_row + 1, :]
            v = v_ref[rows, cols]
            k = 1.0 - jnp.exp(lf)
            q = q_ref[rows, cols].astype(F32)
            cum_ref[...] = cum
            qrow_ref[...] = q

            def row(t, sct):
                w = jnp.exp(jnp.minimum(cum_ref[pl.ds(t, 1), :] - cum, 0.0))
                col = jnp.sum(qrow_ref[pl.ds(t, 1), :] * k * w, axis=-1, keepdims=True)
                return jnp.where(ci == t, col, sct)

            sct = lax.fori_loop(0, c, row, jnp.zeros((c, c), F32))
            valid_t = anti if mask is causal else causal
            st = st_ref[slot]
            o = (_dot_tn(jnp.where(valid_t, sct, 0.0).astype(BF16), v)
                 + _dot_nt((q * jnp.exp(cum)).astype(BF16), st.astype(BF16)))
            o_ref_d[rows, cols] = o.astype(BF16)
            st_ref[slot] = st * jnp.exp(tot) + _dot_tn(v, (k * jnp.exp(tot - cum)).astype(BF16))

    def fast_step(i):
        chains = chains_of(i)
        cums = []
        for _, rows, cols, lf_ref, _, tri, _, _, _ in chains:
            lf_hi, lf_lo = _split_bf16(lf_ref[rows, cols])
            cums.append(_dot(tri, lf_hi) + _dot(tri, lf_lo))
        prods = []
        for (slot, rows, cols, lf_ref, _, _, mask, ref_row, end_row), cum in zip(chains, cums):
            ref = cum[ref_row:ref_row + 1, :]
            tot = cum[end_row:end_row + 1, :]
            v = v_ref[rows, cols]
            k = 1.0 - jnp.exp(lf_ref[rows, cols])
            qt = q_ref[rows, cols].astype(F32) * jnp.exp(cum - ref)
            kt = k * jnp.exp(ref - cum)
            sc = _dot_nt(qt.astype(BF16), kt.astype(BF16))
            upd = _dot_tn(v, (kt * jnp.exp(tot - ref)).astype(BF16))
            prods.append((sc, (qt * jnp.exp(ref)).astype(BF16), jnp.exp(tot), upd, v))
        states = {}
        for (slot, rows, cols, _, o_ref_d, _, mask, _, _), (sc, q_in, decay, upd, v) in zip(chains, prods):
            st = states[slot] if slot in states else st_ref[slot]
            o = _dot(jnp.where(mask, sc, 0.0).astype(BF16), v) + _dot_nt(q_in, st.astype(BF16))
            o_ref_d[rows, cols] = o.astype(BF16)
            states[slot] = st * decay + upd
        for slot, st in states.items():
            st_ref[slot] = st

    def test_decay(i):
        strong_ref[0] = (max_half_decay(i) > HGRN_SAFE_DECAY).astype(jnp.int32)

    def body(i, _):
        strong = strong_ref[0]

        @pl.when(strong == 0)
        def _():
            test_decay(jnp.minimum(i + 1, n_iter - 1))
            fast_step(i)

        @pl.when(strong != 0)
        def _():
            test_decay(jnp.minimum(i + 1, n_iter - 1))
            exact_step(i)

        return 0

    test_decay(0)
    lax.fori_loop(0, n_iter, body, 0)

    def fin(t, _):
        r = pl.multiple_of(t * ROW_TILE, ROW_TILE)
        for hd in range(HGRN_HEADS):
            cols = slice(HGRN_K * hd, HGRN_K * (hd + 1))
            o = of_ref[pl.ds(r, ROW_TILE), cols].astype(F32) + ob_ref[pl.ds(r, ROW_TILE), cols].astype(F32)
            ms = jnp.mean(o * o, axis=-1, keepdims=True)
            o = o * lax.rsqrt(ms + NORM_EPS) * ng_ref[...]
            o_ref[pl.ds(r, ROW_TILE), cols] = (o * g_ref[pl.ds(r, ROW_TILE), cols].astype(F32)).astype(BF16)
        return 0

    lax.fori_loop(0, t_len // ROW_TILE, fin, 0)


def _hgrn(hq, hv, lff, lfb, hg, norm_g, n_batch, t_len, n_ctx):
    def view(a):
        return a.reshape(n_batch, t_len, HGRN_WIDTH)

    blk = pl.BlockSpec((None, t_len, HGRN_WIDTH), lambda b: (b, 0, 0))
    out = pl.pallas_call(
        functools.partial(_hgrn_kernel, n_ctx=n_ctx, t_len=t_len),
        grid=(n_batch,),
        in_specs=[blk] * 5 + [pl.BlockSpec((1, HGRN_K), lambda b: (0, 0))],
        out_specs=blk,
        out_shape=jax.ShapeDtypeStruct((n_batch, t_len, HGRN_WIDTH), BF16),
        scratch_shapes=[pltpu.VMEM((t_len, HGRN_WIDTH), BF16), pltpu.VMEM((t_len, HGRN_WIDTH), BF16),
                        pltpu.VMEM((2 * HGRN_HEADS, HGRN_K, HGRN_K), F32),
                        pltpu.VMEM((HGRN_CHUNK, HGRN_K), F32), pltpu.VMEM((HGRN_CHUNK, HGRN_K), F32),
                        pltpu.SMEM((1,), jnp.int32)],
        compiler_params=_cparams(("parallel",)),
        name="hgrn2",
    )(view(hq), view(hv), view(lff), view(lfb), view(hg), norm_g)
    return out.reshape(n_batch * t_len, HGRN_WIDTH)


def _mix_residual_norm(x_ref, a_ref, b_ref, c_ref, w_ref, mod_ref, g_ref):
    y = (_dot(a_ref[...], w_ref[0:256, :]) + _dot(b_ref[...], w_ref[256:768, :])
         + _dot(c_ref[...], w_ref[768:1024, :]))
    x = x_ref[...] + mod_ref[0:1, :] * y
    ms = jnp.mean(x * x, axis=-1, keepdims=True)
    h = (x * lax.rsqrt(ms + NORM_EPS) * g_ref[...]) * (1.0 + mod_ref[2:3, :]) + mod_ref[1:2, :]
    return x, h


def _outproj_router_kernel(x_ref, a_ref, b_ref, c_ref, w_ref, mod_ref, g_ref, r_ref, xo_ref, h_ref, gate_ref):
    x, h = _mix_residual_norm(x_ref, a_ref, b_ref, c_ref, w_ref, mod_ref, g_ref)
    xo_ref[...] = x
    h_ref[...] = _pack_bf16_pairs(h)
    h_hi, h_lo = _split_bf16(h)
    r_hi, r_lo = _split_bf16(r_ref[...])
    logits = _dot(h_hi, r_hi) + _dot(h_lo, r_hi) + _dot(h_hi, r_lo)
    lane = lax.broadcasted_iota(jnp.int32, logits.shape, 1).astype(F32)
    logits = jnp.where(lane < N_EXPERTS, logits, NEG_BIG)
    v1 = jnp.max(logits, axis=-1, keepdims=True)
    i1 = jnp.min(jnp.where(logits == v1, lane, 128.0), axis=-1, keepdims=True)
    rest_l = jnp.where(lane == i1, NEG_BIG, logits)
    v2 = jnp.max(rest_l, axis=-1, keepdims=True)
    i2 = jnp.min(jnp.where(rest_l == v2, lane, 128.0), axis=-1, keepdims=True)
    e2 = jnp.exp(v2 - v1)
    w1 = 1.0 / (1.0 + e2)
    w2 = e2 * w1
    gate_ref[...] = (jnp.where(lane == i1, w1, 0.0) + jnp.where(lane == i2, w2, 0.0)
                     + jnp.where(lane == 8.0, i1, 0.0) + jnp.where(lane == 9.0, i2, 0.0)
                     + jnp.where(lane == 10.0, w1, 0.0) + jnp.where(lane == 11.0, w2, 0.0))


def _out_projection_router(xc, a, b, c, w, mod, g, router):
    n = xc.shape[0]
    nt = n // ROW_TILE
    row = lambda i: (i, 0)
    const = lambda i: (0, 0)
    return pl.pallas_call(
        _outproj_router_kernel,
        grid=(nt,),
        in_specs=[
            pl.BlockSpec((ROW_TILE, D_MODEL), row),
            pl.BlockSpec((ROW_TILE, DIFF_WIDTH), row),
            pl.BlockSpec((ROW_TILE, HGRN_WIDTH), row),
            pl.BlockSpec((ROW_TILE, SWA_WIDTH), row),
            pl.BlockSpec((D_MODEL, D_MODEL), const),
            pl.BlockSpec((None, 3, D_MODEL), lambda i: (i, 0, 0)),
            pl.BlockSpec((1, D_MODEL), const),
            pl.BlockSpec((D_MODEL, 128), const),
        ],
        out_specs=[pl.BlockSpec((ROW_TILE, D_MODEL), row), pl.BlockSpec((ROW_TILE, D_MODEL // 2), row),
                   pl.BlockSpec((ROW_TILE, 128), row)],
        out_shape=[jax.ShapeDtypeStruct((n, D_MODEL), F32), jax.ShapeDtypeStruct((n, D_MODEL // 2), jnp.int32),
                   jax.ShapeDtypeStruct((n, 128), F32)],
        compiler_params=_cparams(("parallel",)),
        name="out_projection",
    )(xc, a, b, c, w, mod, g, router)


def _outproj_ffn_kernel(x_ref, a_ref, b_ref, c_ref, wo_ref, mod_ref, g_ref, w1_ref, w3_ref, w2_ref, o_ref):
    x, h = _mix_residual_norm(x_ref, a_ref, b_ref, c_ref, wo_ref, mod_ref, g_ref)
    h = h.astype(BF16)
    u = _dot(h, w1_ref[...])
    act = (_silu(u) * _dot(h, w3_ref[...])).astype(BF16)
    o_ref[...] = x + mod_ref[3:4, :] * _dot(act, w2_ref[...])


def _outproj_dense_ffn(xc, a, b, c, wo, mod, g, w1, w3, w2):
    n = xc.shape[0]
    nt = n // ROW_TILE
    row = lambda i: (i, 0)
    const = lambda i: (0, 0)
    return pl.pallas_call(
        _outproj_ffn_kernel,
        grid=(nt,),
        in_specs=[
            pl.BlockSpec((ROW_TILE, D_MODEL), row),
            pl.BlockSpec((ROW_TILE, DIFF_WIDTH), row),
            pl.BlockSpec((ROW_TILE, HGRN_WIDTH), row),
            pl.BlockSpec((ROW_TILE, SWA_WIDTH), row),
            pl.BlockSpec((D_MODEL, D_MODEL), const),
            pl.BlockSpec((None, 4, D_MODEL), lambda i: (i, 0, 0)),
            pl.BlockSpec((1, D_MODEL), const),
            pl.BlockSpec((D_MODEL, D_FF), const),
            pl.BlockSpec((D_MODEL, D_FF), const),
            pl.BlockSpec((D_FF, D_MODEL), const),
        ],
        out_specs=pl.BlockSpec((ROW_TILE, D_MODEL), row),
        out_shape=jax.ShapeDtypeStruct((n, D_MODEL), F32),
        compiler_params=_cparams(("parallel",)),
        name="outproj_dense_ffn",
    )(xc, a, b, c, wo, mod, g, w1, w3, w2)


MOE_TM = 256
SC_CORES = 2
SC_SUBCORES = 16
SC_GATHER_ROWS = 128


def _pack_bf16_pairs(h):
    half = h.shape[1] // 2
    bits = pltpu.bitcast(h.astype(BF16).astype(F32), jnp.uint32)
    packed = (bits[:, :half] >> 16) | (bits[:, half:] & jnp.uint32(0xFFFF0000))
    return pltpu.bitcast(packed, jnp.int32)


def _unpack_bf16_pairs(u):
    bits = pltpu.bitcast(u, jnp.uint32)
    lo = pltpu.bitcast(bits << 16, F32)
    hi = pltpu.bitcast(bits & jnp.uint32(0xFFFF0000), F32)
    return jnp.concatenate([lo, hi], axis=1)


def _sc_gather_rows(table, idx):
    n_workers = SC_CORES * SC_SUBCORES
    n_rows, width = idx.shape[0], table.shape[1]
    assert n_rows % (n_workers * SC_GATHER_ROWS) == 0
    per_worker = n_rows // n_workers
    mesh = plsc.VectorSubcoreMesh(core_axis_name="c", subcore_axis_name="s",
                                  num_cores=SC_CORES, num_subcores=SC_SUBCORES)

    @functools.partial(
        pl.kernel, mesh=mesh,
        out_type=jax.ShapeDtypeStruct((n_rows, width), table.dtype),
        scratch_types=[pltpu.VMEM((SC_GATHER_ROWS,), jnp.int32),
                       pltpu.VMEM((SC_GATHER_ROWS, width), table.dtype),
                       pltpu.SemaphoreType.DMA],
        name="sc_gather_rows",
    )
    def gather(table_hbm, idx_hbm, out_hbm, idx_v, rows_v, sem):
        worker = lax.axis_index("s") * SC_CORES + lax.axis_index("c")
        base = worker * per_worker

        @pl.loop(0, per_worker // SC_GATHER_ROWS)
        def _(step):
            off = pl.multiple_of(base + step * SC_GATHER_ROWS, SC_GATHER_ROWS)
            pltpu.sync_copy(idx_hbm.at[pl.ds(off, SC_GATHER_ROWS)], idx_v)
            pltpu.async_copy(table_hbm.at[idx_v], rows_v, sem).wait()
            pltpu.sync_copy(rows_v, out_hbm.at[pl.ds(off, SC_GATHER_ROWS)])

    return gather(table, idx)


def _sc_scatter_rows(rows, idx, n_out):
    n_workers = SC_CORES * SC_SUBCORES
    n_rows, width = rows.shape
    assert n_rows % (n_workers * SC_GATHER_ROWS) == 0
    chunks_per_worker = n_rows // (n_workers * SC_GATHER_ROWS)
    idx = idx.reshape(2, n_rows // SC_GATHER_ROWS, SC_GATHER_ROWS)
    mesh = plsc.VectorSubcoreMesh(core_axis_name="c", subcore_axis_name="s",
                                  num_cores=SC_CORES, num_subcores=SC_SUBCORES)

    @functools.partial(
        pl.kernel, mesh=mesh,
        out_type=jax.ShapeDtypeStruct((n_out, width), rows.dtype),
        scratch_types=[pltpu.VMEM((SC_GATHER_ROWS,), jnp.int32), pltpu.VMEM((SC_GATHER_ROWS,), jnp.int32),
                       pltpu.VMEM((SC_GATHER_ROWS, width), rows.dtype),
                       pltpu.SemaphoreType.DMA, pltpu.SemaphoreType.DMA],
        name="sc_scatter_rows",
    )
    def scatter(rows_hbm, idx_hbm, out_hbm, idx0_v, idx1_v, rows_v, sem0, sem1):
        worker = lax.axis_index("s") * SC_CORES + lax.axis_index("c")

        @pl.loop(0, chunks_per_worker)
        def _(step):
            chunk = worker * chunks_per_worker + step
            off = pl.multiple_of(chunk * SC_GATHER_ROWS, SC_GATHER_ROWS)
            pltpu.sync_copy(rows_hbm.at[pl.ds(off, SC_GATHER_ROWS)], rows_v)
            pltpu.sync_copy(idx_hbm.at[0, chunk], idx0_v)
            first = pltpu.async_copy(rows_v, out_hbm.at[idx0_v], sem0)
            pltpu.sync_copy(idx_hbm.at[1, chunk], idx1_v)
            second = pltpu.async_copy(rows_v, out_hbm.at[idx1_v], sem1)
            first.wait()
            second.wait()

    return scatter(rows, idx)


def _moe_expert_kernel(te_ref, tv_ref, xs_ref, w1_ref, w3_ref, w2_ref, ys_ref):
    r = pl.program_id(0)

    @pl.when(tv_ref[r] == 1)
    def _():
        h = _unpack_bf16_pairs(xs_ref[...]).astype(BF16)
        u = _dot(h, w1_ref[...])
        act = (_silu(u) * _dot(h, w3_ref[...])).astype(BF16)
        ys_ref[...] = _pack_bf16_pairs(_dot(act, w2_ref[...]))

    @pl.when(tv_ref[r] == 0)
    def _():
        ys_ref[...] = jnp.zeros_like(ys_ref)


def _moe_combine_kernel(x_ref, y0_ref, y1_ref, route_ref, mod_ref, o_ref):
    y = (route_ref[:, 10:11] * _unpack_bf16_pairs(y0_ref[...])
         + route_ref[:, 11:12] * _unpack_bf16_pairs(y1_ref[...]))
    o_ref[...] = x_ref[...] + mod_ref[...] * y


def _moe_plan(route, n_rt):
    n = route.shape[0]
    tm = MOE_TM
    gates_t = route[:, :N_EXPERTS].T
    sel = gates_t != 0.0
    si = sel.astype(jnp.int32)
    rank = jnp.cumsum(si, axis=1) - si
    ntile = (jnp.sum(si, axis=1) + tm - 1) // tm
    tile_end = jnp.cumsum(ntile)
    tile_off = tile_end - ntile
    used = tile_end[-1]
    n_rows = n_rt * tm
    dest = jnp.where(sel, tile_off[:, None] * tm + rank, n_rows)
    expert = jnp.arange(N_EXPERTS, dtype=jnp.int32)[:, None]

    def row_of(lane):
        pick = route[:, lane].astype(jnp.int32)[None, :]
        return jnp.sum(jnp.where(expert == pick, dest, 0), axis=0).astype(jnp.int32)

    pos = jnp.stack([row_of(8), row_of(9)])
    r = jnp.arange(n_rt, dtype=jnp.int32)
    rc = jnp.minimum(r, used - 1)
    tile_expert = jnp.sum((tile_end[None, :] <= rc[:, None]).astype(jnp.int32), axis=1)
    tile_valid = (r < used).astype(jnp.int32)
    return pos, tile_expert, tile_valid


def _moe_ffn(x, h_packed, route, w1, w3, w2, layer_idx, mod):
    n = x.shape[0]
    nt = n // ROW_TILE
    tm = MOE_TM
    half = D_MODEL // 2
    n_rt = 2 * n // tm + N_EXPERTS
    pos, tile_expert, tile_valid = _moe_plan(route, n_rt)

    xs = _sc_scatter_rows(h_packed, pos, n_rt * tm + SC_GATHER_ROWS)
    expert_w = lambda r, te, tv: (layer_idx, te[r], 0, 0)
    ys = pl.pallas_call(
        _moe_expert_kernel,
        grid_spec=pltpu.PrefetchScalarGridSpec(
            num_scalar_prefetch=2,
            grid=(n_rt,),
            in_specs=[
                pl.BlockSpec((tm, half), lambda r, te, tv: (r, 0)),
                pl.BlockSpec((None, None, D_MODEL, D_FF), expert_w),
                pl.BlockSpec((None, None, D_MODEL, D_FF), expert_w),
                pl.BlockSpec((None, None, D_FF, D_MODEL), expert_w),
            ],
            out_specs=pl.BlockSpec((tm, half), lambda r, te, tv: (r, 0)),
        ),
        out_shape=jax.ShapeDtypeStruct((n_rt * tm, half), jnp.int32),
        compiler_params=_cparams(("arbitrary",)),
        name="moe_experts",
    )(tile_expert, tile_valid, xs, w1, w3, w2)

    y = _sc_gather_rows(ys, jnp.where(pos >= n_rt * tm, 0, pos).reshape(-1))
    row = lambda i: (i, 0)
    return pl.pallas_call(
        _moe_combine_kernel,
        grid=(nt,),
        in_specs=[
            pl.BlockSpec((ROW_TILE, D_MODEL), row),
            pl.BlockSpec((ROW_TILE, half), row),
            pl.BlockSpec((ROW_TILE, half), lambda i: (nt + i, 0)),
            pl.BlockSpec((ROW_TILE, 128), row),
            pl.BlockSpec((None, 1, D_MODEL), lambda i: (i, 0, 0)),
        ],
        out_specs=pl.BlockSpec((ROW_TILE, D_MODEL), row),
        out_shape=jax.ShapeDtypeStruct((n, D_MODEL), F32),
        compiler_params=_cparams(("parallel",)),
        name="moe_combine",
    )(x, y, y, route, mod)


def _rope_tables(n_ctx, n_lat):
    pos_r = jnp.arange(n_lat, dtype=jnp.int32) // GRID_W
    pos_c = jnp.arange(n_lat, dtype=jnp.int32) % GRID_W

    def per_head(head_dim):
        nf = head_dim // 4
        inv = ROPE_BASE ** (-jnp.arange(nf, dtype=F32) / nf)
        ang_r = pos_r.astype(F32)[:, None] * inv[None, :]
        ang_c = pos_c.astype(F32)[:, None] * inv[None, :]
        z = jnp.zeros_like(ang_r)
        cos = jnp.concatenate([jnp.cos(ang_r)] * 2 + [jnp.cos(ang_c)] * 2, axis=-1)
        sa = jnp.concatenate([-jnp.sin(ang_r), z, -jnp.sin(ang_c), z], axis=-1)
        sb = jnp.concatenate([z, jnp.sin(ang_r), z, jnp.sin(ang_c)], axis=-1)
        reps = 256 // head_dim
        tabs = [jnp.tile(t, (1, reps)) for t in (cos, sa, sb)]
        ctx = [jnp.ones((n_ctx, 256), F32), jnp.zeros((n_ctx, 256), F32), jnp.zeros((n_ctx, 256), F32)]
        return [jnp.concatenate([c, t], axis=0) for c, t in zip(ctx, tabs)]

    return jnp.concatenate(per_head(DIFF_HEAD_DIM) + per_head(SWA_HEAD_DIM), axis=-1)


def _block_diag_mean(group):
    idx = jnp.arange(256) // group
    return jnp.where(idx[:, None] == idx[None, :], 1.0 / group, 0.0).astype(BF16)


def kernel(x, c, ctx, c_ctx, ada_w, ada_b, norm_mix_g, norm_ffn_g, w_in, w_out, diff_qk_norm_g, diff_lambda,
           diff_subln_g, hgrn_lb_logits, hgrn_norm_g, swa_qk_norm_g, swa_sink, ffn_w1, ffn_w3, ffn_w2,
           moe_router, moe_w1, moe_w3, moe_w2):
    n_batch, n_lat, _ = x.shape
    n_ctx = ctx.shape[1]
    depth = ada_w.shape[0]
    t_len = n_ctx + n_lat
    tps = t_len // ROW_TILE
    assert n_ctx == ROW_TILE and n_lat % ROW_TILE == 0 and n_lat >= ROW_TILE + 2 * WINDOW

    xc = jnp.concatenate([ctx, x], axis=1).reshape(n_batch * t_len, D_MODEL)

    n_rows = -(-(n_batch + 1) // 8) * 8
    cond = jnp.concatenate([c, c_ctx[None, :], jnp.zeros((n_rows - n_batch - 1, D_MODEL), F32)], axis=0)
    mods = _ada_modulation(cond, ada_w, ada_b).reshape(depth, n_rows, 6, D_MODEL)
    m_lat = jnp.broadcast_to(mods[:, :n_batch, None], (depth, n_batch, tps - 1, 6, D_MODEL))
    m_ctx = jnp.broadcast_to(mods[:, n_batch, None, None], (depth, n_batch, 1, 6, D_MODEL))
    mods = jnp.concatenate([m_ctx, m_lat], axis=2).reshape(depth, n_batch * tps, 6, D_MODEL)

    lb = jnp.cumsum(jax.nn.softmax(hgrn_lb_logits.astype(F32), axis=1), axis=1)
    lb = lb - lb[:, :1]
    rope = _rope_tables(n_ctx, n_lat)
    g32 = _block_diag_mean(DIFF_HEAD_DIM)
    g64 = _block_diag_mean(SWA_HEAD_DIM)

    perm_q = jnp.arange(SWA_WIDTH).reshape(SWA_KV_HEADS, SWA_GROUP, SWA_HEAD_DIM).transpose(1, 0, 2).reshape(-1)
    qc0 = 3 * DIFF_WIDTH + 5 * HGRN_WIDTH
    col_perm = jnp.concatenate([jnp.arange(qc0), qc0 + perm_q, jnp.arange(qc0 + SWA_WIDTH, D_IN)])
    oc0 = DIFF_WIDTH + HGRN_WIDTH
    row_perm = jnp.concatenate([jnp.arange(oc0), oc0 + perm_q])
    moe_w1_b, moe_w3_b, moe_w2_b = moe_w1.astype(BF16), moe_w3.astype(BF16), moe_w2.astype(BF16)

    for layer in range(depth):
        lam_init = 0.8 - 0.6 * math.exp(-0.3 * layer)
        mod = mods[layer]
        w_in_l = w_in[layer][:, col_perm].astype(BF16)
        w_out_l = w_out[layer][row_perm, :].astype(BF16)
        gains = jnp.stack([
            jnp.tile(diff_qk_norm_g[layer, 0], 8) * (DIFF_HEAD_DIM ** -0.5 * LOG2_E),
            jnp.tile(diff_qk_norm_g[layer, 1], 8),
            jnp.tile(swa_qk_norm_g[layer, 0], 4) * (SWA_HEAD_DIM ** -0.5 * LOG2_E),
            jnp.tile(swa_qk_norm_g[layer, 1], 4),
        ]).astype(F32)
        lbt = jnp.stack([
            jnp.log(lb[:, layer]).reshape(-1),
            jnp.log1p(-lb[:, layer]).reshape(-1),
        ]).astype(F32)
        lv = diff_lambda[layer].astype(F32)
        lam = jnp.exp(jnp.sum(lv[0] * lv[1])) - jnp.exp(jnp.sum(lv[2] * lv[3])) + lam_init
        lam_row = jnp.full((1, 128), lam, F32)
        post_row = (jnp.tile(diff_subln_g[layer], DIFF_HEADS) * (1.0 - lam_init)).reshape(1, DIFF_WIDTH).astype(F32)
        sink_tab = jnp.broadcast_to(swa_sink[layer].astype(F32)[:, None] * LOG2_E, (SWA_Q_HEADS, 128))

        (qa, ka, va, hq, hv, lff, lfb, hg, qc, kc, vc) = _in_projection(
            xc, mod[:, 0:2], norm_mix_g[layer].reshape(1, D_MODEL), w_in_l, rope, gains, lbt, g32, g64, tps)
        a = _diff_attention(qa, ka, va, lam_row, post_row, g64, n_batch, t_len, n_ctx)
        b = _hgrn(hq, hv, lff, lfb, hg, hgrn_norm_g[layer].reshape(1, HGRN_K), n_batch, t_len, n_ctx)
        cc = _swa_attention(qc, kc, vc, sink_tab, n_batch, t_len, n_ctx)

        jj = layer // 2
        g2 = norm_ffn_g[layer].reshape(1, D_MODEL)
        if layer % 2 == 1:
            router = jnp.pad(moe_router[jj].astype(F32), ((0, 0), (0, 128 - N_EXPERTS)))
            x_mid, h2, route = _out_projection_router(xc, a, b, cc, w_out_l, mod[:, 2:5], g2, router)
            if layer == depth - 1:
                is_lat = (jnp.arange(n_batch * t_len, dtype=jnp.int32) % t_len) >= n_ctx
                route = jnp.where(is_lat[:, None], route, 0.0)
            xc = _moe_ffn(x_mid, h2, route, moe_w1_b, moe_w3_b, moe_w2_b, jj, mod[:, 5:6])
        else:
            xc = _outproj_dense_ffn(xc, a, b, cc, w_out_l, mod[:, 2:6], g2, ffn_w1[jj].astype(BF16),
                                    ffn_w3[jj].astype(BF16), ffn_w2[jj].astype(BF16))

    return xc.reshape(n_batch, t_len, D_MODEL)[:, n_ctx:, :]
```

```python
import functools
import math

import jax
import jax.numpy as jnp
from jax import lax
from jax.experimental import pallas as pl
from jax.experimental.pallas import tpu as pltpu
from jax.experimental.pallas import tpu_sc as plsc

D_MODEL = 1024
GRID_W = 64
DIFF_HEADS = 4
DIFF_HEAD_DIM = 32
DIFF_V_DIM = 64
DIFF_WIDTH = 256
HGRN_HEADS = 4
HGRN_K = 128
HGRN_WIDTH = 512
SWA_Q_HEADS = 4
SWA_KV_HEADS = 2
SWA_GROUP = 2
SWA_HEAD_DIM = 64
SWA_WIDTH = 256
SWA_KV_WIDTH = 128
WINDOW = 128
D_FF = 2816
N_EXPERTS = 8
ROPE_BASE = 10000.0
NORM_EPS = 1e-6
D_IN = 3840

F32 = jnp.float32
BF16 = jnp.bfloat16

ROW_TILE = 256
HGRN_CHUNK = 64
HGRN_UNROLL = 4
HGRN_SAFE_DECAY = 80.0
NEG_BIG = -1e30
LOG2_E = 1.4426950408889634
VMEM_LIMIT = 56 * 1024 * 1024


def _cparams(sem):
    return pltpu.CompilerParams(dimension_semantics=sem, vmem_limit_bytes=VMEM_LIMIT)


def _split_bf16(v):
    hi = v.astype(BF16)
    lo = (v - hi.astype(F32)).astype(BF16)
    return hi, lo


def _dot(a, b):
    return jnp.dot(a, b, preferred_element_type=F32)


def _dot_nt(a, b):
    return lax.dot_general(a, b, (((1,), (1,)), ((), ())), preferred_element_type=F32)


def _dot_tn(a, b):
    return lax.dot_general(a, b, (((0,), (0,)), ((), ())), preferred_element_type=F32)


def _group_mean_sq(y, gmat):
    hi, lo = _split_bf16(y * y)
    return _dot(hi, gmat) + _dot(lo, gmat)


def _silu(v):
    return v * (1.0 / (1.0 + jnp.exp(-v)))


def _ada_kernel(s_ref, w_ref, b_ref, o_ref):
    s = s_ref[...]
    s = _silu(s)
    s_hi, s_lo = _split_bf16(s)
    w_hi, w_lo = _split_bf16(w_ref[...])
    o_ref[...] = _dot(s_hi, w_hi) + _dot(s_lo, w_hi) + _dot(s_hi, w_lo) + b_ref[...]


def _ada_modulation(cond, ada_w, ada_b):
    depth = ada_w.shape[0]
    r = cond.shape[0]
    nblk = 6 * D_MODEL // 1024
    return pl.pallas_call(
        _ada_kernel,
        grid=(depth, nblk),
        in_specs=[
            pl.BlockSpec((r, D_MODEL), lambda l, n: (0, 0)),
            pl.BlockSpec((None, D_MODEL, 1024), lambda l, n: (l, 0, n)),
            pl.BlockSpec((None, 1, 1024), lambda l, n: (l, 0, n)),
        ],
        out_specs=pl.BlockSpec((None, r, 1024), lambda l, n: (l, 0, n)),
        out_shape=jax.ShapeDtypeStruct((depth, r, 6 * D_MODEL), F32),
        compiler_params=_cparams(("parallel", "parallel")),
        name="ada_modulation",
    )(cond, ada_w, ada_b.reshape(depth, 1, 6 * D_MODEL))


def _rope(y, cos, sa, sb, half):
    w = y.shape[-1]
    fwd = pltpu.roll(y, w - half, 1)
    bwd = pltpu.roll(y, half, 1)
    return y * cos + fwd * sa + bwd * sb


def _inproj_kernel(x_ref, mod_ref, g_ref, w_ref, rope_ref, gains_ref, lbt_ref, ga_ref, gc_ref,
                   qa_ref, ka_ref, va_ref, hq_ref, hv_ref, lff_ref, lfb_ref, hg_ref,
                   qc_ref, kc_ref, vc_ref):
    x = x_ref[...]
    shift = mod_ref[0:1, :]
    scale = mod_ref[1:2, :]
    ms = jnp.mean(x * x, axis=-1, keepdims=True)
    h = (x * lax.rsqrt(ms + NORM_EPS) * g_ref[...]) * (1.0 + scale) + shift
    hb = h.astype(BF16)

    def proj(a, b):
        return _dot(hb, w_ref[:, a:b])

    def qk_prep(y, gmat, gain, cos, sa, sb, half):
        msq = _group_mean_sq(y, gmat)
        y = y * lax.rsqrt(msq + NORM_EPS) * gain
        return _rope(y, cos, sa, sb, half)

    ga = ga_ref[...]
    gc = gc_ref[...]
    ra = [rope_ref[:, 256 * i:256 * (i + 1)] for i in range(6)]
    qa = qk_prep(proj(0, 256), ga, gains_ref[0:1, :], ra[0], ra[1], ra[2], 8)
    qa_ref[...] = qa.astype(BF16)
    ka = qk_prep(proj(256, 512), ga, gains_ref[1:2, :], ra[0], ra[1], ra[2], 8)
    ka_ref[...] = ka.astype(BF16)
    va_ref[...] = proj(512, 768).astype(BF16)
    hq_ref[...] = _silu(proj(768, 1280)).astype(BF16)
    hv_ref[...] = proj(1280, 1792).astype(BF16)
    for d, lf_ref in enumerate((lff_ref, lfb_ref)):
        z = proj(1792 + 512 * d, 2304 + 512 * d)
        log_lb = lbt_ref[0:1, 512 * d:512 * (d + 1)]
        log1m_lb = lbt_ref[1:2, 512 * d:512 * (d + 1)]
        sp = jnp.maximum(-z, 0.0) + jnp.log(1.0 + jnp.exp(-jnp.abs(z)))
        b2 = log1m_lb - sp
        mx = jnp.maximum(log_lb, b2)
        lf_ref[...] = mx + jnp.log(1.0 + jnp.exp(-jnp.abs(log_lb - b2)))
    hg_ref[...] = _silu(proj(2816, 3328)).astype(BF16)
    qc = qk_prep(proj(3328, 3584), gc, gains_ref[2:3, :], ra[3], ra[4], ra[5], 16)
    qc_ref[...] = qc.astype(BF16)
    kc = qk_prep(proj(3584, 3712), gc[0:128, 0:128], gains_ref[3:4, 0:128],
                 ra[3][:, 0:128], ra[4][:, 0:128], ra[5][:, 0:128], 16)
    kc_ref[...] = kc.astype(BF16)
    vc_ref[...] = proj(3712, 3840).astype(BF16)


def _in_projection(xc, mod, g, w, rope, gains, lbt, ga, gc, tiles_per_seq):
    n = xc.shape[0]
    nt = n // ROW_TILE
    tps = tiles_per_seq

    def row(i):
        return (i, 0)

    def mod_idx(i):
        return (i, 0, 0)

    const = lambda i: (0, 0)
    widths = [(256, BF16), (256, BF16), (256, BF16), (512, BF16), (512, BF16),
              (512, F32), (512, F32), (512, BF16), (256, BF16), (128, BF16), (128, BF16)]
    return pl.pallas_call(
        _inproj_kernel,
        grid=(nt,),
        in_specs=[
            pl.BlockSpec((ROW_TILE, D_MODEL), row),
            pl.BlockSpec((None, 2, D_MODEL), mod_idx),
            pl.BlockSpec((1, D_MODEL), const),
            pl.BlockSpec((D_MODEL, D_IN), const),
            pl.BlockSpec((ROW_TILE, 6 * 256), lambda i: (i % tps, 0)),
            pl.BlockSpec((4, 256), const),
            pl.BlockSpec((2, 1024), const),
            pl.BlockSpec((256, 256), const),
            pl.BlockSpec((256, 256), const),
        ],
        out_specs=[pl.BlockSpec((ROW_TILE, wd), row) for wd, _ in widths],
        out_shape=[jax.ShapeDtypeStruct((n, wd), dt) for wd, dt in widths],
        compiler_params=_cparams(("parallel",)),
        name="in_projection",
    )(xc, mod, g, w, rope, gains, lbt, ga, gc)


def _diff_attend(q, k, v, lam):
    lane = lax.broadcasted_iota(jnp.int32, q.shape, 1)
    n_maps = 2 * DIFF_HEADS

    def scores(g):
        return _dot_nt(jnp.where(lane // DIFF_HEAD_DIM == g, q, jnp.zeros_like(q)), k)

    out = jnp.zeros(q.shape, F32)
    parts = []
    s_next = scores(0)
    for g in range(n_maps):
        s = s_next
        if g + 1 < n_maps:
            s_next = scores(g + 1)
        e = jnp.exp2(s - jnp.max(s, axis=-1, keepdims=True)).astype(BF16)
        o = _dot(e, v[g // 2])
        parts.append(o * (1.0 / pltpu.roll(o, DIFF_WIDTH - DIFF_V_DIM, 1)))
        if g % 2 == 1:
            out = out + jnp.where(lane // DIFF_V_DIM == g // 2, parts[g - 1] - lam * parts[g], 0.0)
    return out


def _diff_kernel(q_ref, k_ref, v_ref, lam_ref, post_ref, g64_ref, o_ref, vaug_ref, *, n_ctx):
    j = pl.program_id(1)
    lam = lam_ref[0:1, 0:1]

    @pl.when(j == 0)
    def _():
        v = v_ref[...]
        lane = lax.broadcasted_iota(jnp.int32, v.shape, 1)
        for hd in range(DIFF_HEADS):
            vaug_ref[hd] = jnp.where(lane // DIFF_V_DIM == (hd + 1) % DIFF_HEADS, jnp.ones_like(v), v)

    def finish(o):
        msq = _group_mean_sq(o, g64_ref[...])
        o_ref[...] = (o * lax.rsqrt(msq + NORM_EPS) * post_ref[...]).astype(BF16)

    @pl.when(j == 0)
    def _():
        finish(_diff_attend(q_ref[...], k_ref[0:n_ctx, :], [vaug_ref[hd, 0:n_ctx, :] for hd in range(DIFF_HEADS)],
                            lam))

    @pl.when(j > 0)
    def _():
        finish(_diff_attend(q_ref[...], k_ref[...], [vaug_ref[hd] for hd in range(DIFF_HEADS)], lam))


def _diff_attention(qa, ka, va, lam_row, post_row, g64, n_batch, t_len, n_ctx):
    tps = t_len // ROW_TILE
    q3 = qa.reshape(n_batch, t_len, DIFF_WIDTH)
    k3 = ka.reshape(n_batch, t_len, DIFF_WIDTH)
    v3 = va.reshape(n_batch, t_len, DIFF_WIDTH)
    const = lambda b, j: (0, 0)
    out = pl.pallas_call(
        functools.partial(_diff_kernel, n_ctx=n_ctx),
        grid=(n_batch, tps),
        in_specs=[
            pl.BlockSpec((None, ROW_TILE, DIFF_WIDTH), lambda b, j: (b, j, 0)),
            pl.BlockSpec((None, t_len, DIFF_WIDTH), lambda b, j: (b, 0, 0)),
            pl.BlockSpec((None, t_len, DIFF_WIDTH), lambda b, j: (b, 0, 0)),
            pl.BlockSpec((1, 128), const),
            pl.BlockSpec((1, DIFF_WIDTH), const),
            pl.BlockSpec((256, 256), const),
        ],
        out_specs=pl.BlockSpec((None, ROW_TILE, DIFF_WIDTH), lambda b, j: (b, j, 0)),
        out_shape=jax.ShapeDtypeStruct((n_batch, t_len, DIFF_WIDTH), BF16),
        scratch_shapes=[pltpu.VMEM((DIFF_HEADS, t_len, DIFF_WIDTH), BF16)],
        compiler_params=_cparams(("parallel", "arbitrary")),
        name="diff_attention",
    )(q3, k3, v3, lam_row, post_row, g64)
    return out.reshape(n_batch * t_len, DIFF_WIDTH)


def _swa_kernel(q_ref, k_ref, v_ref, sink_ref, o_ref, *, n_ctx, n_lat):
    j = pl.program_id(1)
    q = q_ref[...]
    tq = q.shape[0]
    lane = lax.broadcasted_iota(jnp.int32, (tq, SWA_KV_WIDTH), 1)
    span = ROW_TILE + 2 * WINDOW
    heads = [(g, hd) for g in range(SWA_GROUP) for hd in range(SWA_KV_HEADS)]

    def run(k, v, valid):
        vlane = lax.broadcasted_iota(jnp.int32, v.shape, 1)
        vaug = [jnp.where(vlane // SWA_HEAD_DIM == hd, v, jnp.ones_like(v)) for hd in range(SWA_KV_HEADS)]
        sinks = [sink_ref[SWA_GROUP * hd + g:SWA_GROUP * hd + g + 1, 0:1] for g, hd in heads]
        scores = []
        for g, hd in heads:
            qg = q[:, SWA_KV_WIDTH * g:SWA_KV_WIDTH * (g + 1)]
            s = _dot_nt(jnp.where(lane // SWA_HEAD_DIM == hd, qg, jnp.zeros_like(qg)), k)
            scores.append(s if valid is None else jnp.where(valid, s, NEG_BIG))
        maxes = [jnp.maximum(jnp.max(s, axis=-1, keepdims=True), sink) for s, sink in zip(scores, sinks)]
        outs = [_dot(jnp.exp2(s - mx).astype(BF16), vaug[hd]) for s, mx, (g, hd) in zip(scores, maxes, heads)]
        for g in range(SWA_GROUP):
            out = jnp.zeros((tq, SWA_KV_WIDTH), F32)
            for i, (gi, hd) in enumerate(heads):
                if gi == g:
                    den = pltpu.roll(outs[i], SWA_HEAD_DIM, 1) + jnp.exp2(sinks[i] - maxes[i])
                    out = out + jnp.where(lane // SWA_HEAD_DIM == hd, outs[i] * (1.0 / den), 0.0)
            o_ref[:, SWA_KV_WIDTH * g:SWA_KV_WIDTH * (g + 1)] = out.astype(BF16)

    @pl.when(j == 0)
    def _():
        run(k_ref[0:n_ctx, :], v_ref[0:n_ctx, :], None)

    @pl.when(j > 0)
    def _():
        q0 = (j - 1) * ROW_TILE
        ks = jnp.clip(q0 - WINDOW, 0, n_lat - span)
        ks = pl.multiple_of(ks, WINDOW)
        k = jnp.concatenate([k_ref[0:n_ctx, :], k_ref[pl.ds(n_ctx + ks, span), :]], axis=0)
        v = jnp.concatenate([v_ref[0:n_ctx, :], v_ref[pl.ds(n_ctx + ks, span), :]], axis=0)
        qpos = q0 + lax.broadcasted_iota(jnp.int32, (tq, n_ctx + span), 0)
        kpos = ks - n_ctx + lax.broadcasted_iota(jnp.int32, (tq, n_ctx + span), 1)
        run(k, v, (kpos < ks) | (jnp.abs(qpos - kpos) <= WINDOW))


def _swa_attention(qc, kc, vc, sink_tab, n_batch, t_len, n_ctx):
    tps = t_len // ROW_TILE
    q3 = qc.reshape(n_batch, t_len, SWA_WIDTH)
    k3 = kc.reshape(n_batch, t_len, SWA_KV_WIDTH)
    v3 = vc.reshape(n_batch, t_len, SWA_KV_WIDTH)
    out = pl.pallas_call(
        functools.partial(_swa_kernel, n_ctx=n_ctx, n_lat=t_len - n_ctx),
        grid=(n_batch, tps),
        in_specs=[
            pl.BlockSpec((None, ROW_TILE, SWA_WIDTH), lambda b, j: (b, j, 0)),
            pl.BlockSpec((None, t_len, SWA_KV_WIDTH), lambda b, j: (b, 0, 0)),
            pl.BlockSpec((None, t_len, SWA_KV_WIDTH), lambda b, j: (b, 0, 0)),
            pl.BlockSpec((SWA_Q_HEADS, 128), lambda b, j: (0, 0)),
        ],
        out_specs=pl.BlockSpec((None, ROW_TILE, SWA_WIDTH), lambda b, j: (b, j, 0)),
        out_shape=jax.ShapeDtypeStruct((n_batch, t_len, SWA_WIDTH), BF16),
        compiler_params=_cparams(("parallel", "arbitrary")),
        name="swa_attention",
    )(q3, k3, v3, sink_tab)
    return out.reshape(n_batch * t_len, SWA_WIDTH)


def _hgrn_kernel(q_ref, v_ref, lff_ref, lfb_ref, g_ref, ng_ref, o_ref, of_ref, ob_ref, st_ref, cum_ref, qrow_ref,
                 strong_ref, *, n_ctx, t_len):
    c = HGRN_CHUNK
    n_chunks = t_len // c
    n_ctx_chunks = n_ctx // c
    mid = c // 2
    ri = lax.broadcasted_iota(jnp.int32, (c, c), 0)
    ci = lax.broadcasted_iota(jnp.int32, (c, c), 1)
    causal = ri >= ci
    anti = ri <= ci
    tri_f = jnp.where(causal, 1.0, 0.0).astype(BF16)
    tri_b = jnp.where(anti, 1.0, 0.0).astype(BF16)

    st_ref[...] = jnp.zeros_like(st_ref)
    n_iter = n_chunks // HGRN_UNROLL

    def chains_of(i):
        chains = []
        for u in range(HGRN_UNROLL):
            step = i * HGRN_UNROLL + u
            rf = pl.multiple_of(step * c, c)
            cb = jnp.where(step < n_ctx_chunks, n_ctx_chunks - 1 - step, n_chunks - 1 + n_ctx_chunks - step)
            rb = pl.multiple_of(cb * c, c)
            for hd in range(HGRN_HEADS):
                cols = slice(HGRN_K * hd, HGRN_K * (hd + 1))
                chains.append((2 * hd, pl.ds(rf, c), cols, lff_ref, of_ref, tri_f, causal, mid - 1, c - 1))
                chains.append((2 * hd + 1, pl.ds(rb, c), cols, lfb_ref, ob_ref, tri_b, anti, mid, 0))
        return chains

    def max_half_decay(i):
        worst = jnp.zeros((1, HGRN_K), F32)
        for _, rows, cols, lf_ref, _, _, _, _, _ in chains_of(i):
            lf = lf_ref[rows, cols]
            worst = jnp.maximum(worst, jnp.maximum(jnp.abs(jnp.sum(lf[0:mid, :], axis=0, keepdims=True)),
                                                   jnp.abs(jnp.sum(lf[mid:c, :], axis=0, keepdims=True))))
        return jnp.max(worst)

    def exact_step(i):
        for slot, rows, cols, lf_ref, o_ref_d, tri, mask, _, end_row in chains_of(i):
            lf = lf_ref[rows, cols]
            lf_hi, lf_lo = _split_bf16(lf)
            cum = _dot(tri, lf_hi) + _dot(tri, lf_lo)
            tot = cum[end_row:end_row + 1, :]
            v = v_ref[rows, cols]
            k = 1.0 - jnp.exp(lf)
            q = q_ref[rows, cols].astype(F32)
            cum_ref[...] = cum
            qrow_ref[...] = q

            def row(t, sct):
                w = jnp.exp(jnp.minimum(cum_ref[pl.ds(t, 1), :] - cum, 0.0))
                col = jnp.sum(qrow_ref[pl.ds(t, 1), :] * k * w, axis=-1, keepdims=True)
                return jnp.where(ci == t, col, sct)

            sct = lax.fori_loop(0, c, row, jnp.zeros((c, c), F32))
            valid_t = anti if mask is causal else causal
            st = st_ref[slot]
            o = (_dot_tn(jnp.where(valid_t, sct, 0.0).astype(BF16), v)
                 + _dot_nt((q * jnp.exp(cum)).astype(BF16), st.astype(BF16)))
            o_ref_d[rows, cols] = o.astype(BF16)
            st_ref[slot] = st * jnp.exp(tot) + _dot_tn(v, (k * jnp.exp(tot - cum)).astype(BF16))

    def fast_step(i):
        chains = chains_of(i)
        cums = []
        for _, rows, cols, lf_ref, _, tri, _, _, _ in chains:
            lf_hi, lf_lo = _split_bf16(lf_ref[rows, cols])
            cums.append(_dot(tri, lf_hi) + _dot(tri, lf_lo))
        prods = []
        for (slot, rows, cols, lf_ref, _, _, mask, ref_row, end_row), cum in zip(chains, cums):
            ref = cum[ref_row:ref_row + 1, :]
            tot = cum[end_row:end_row + 1, :]
            v = v_ref[rows, cols]
            k = 1.0 - jnp.exp(lf_ref[rows, cols])
            qt = q_ref[rows, cols].astype(F32) * jnp.exp(cum - ref)
            kt = k * jnp.exp(ref - cum)
            sc = _dot_nt(qt.astype(BF16), kt.astype(BF16))
            upd = _dot_tn(v, (kt * jnp.exp(tot - ref)).astype(BF16))
            prods.append((sc, (qt * jnp.exp(ref)).astype(BF16), jnp.exp(tot), upd, v))
        states = {}
        for (slot, rows, cols, _, o_ref_d, _, mask, _, _), (sc, q_in, decay, upd, v) in zip(chains, prods):
            st = states[slot] if slot in states else st_ref[slot]
            o = _dot(jnp.where(mask, sc, 0.0).astype(BF16), v) + _dot_nt(q_in, st.astype(BF16))
            o_ref_d[rows, cols] = o.astype(BF16)
            states[slot] = st * decay + upd
        for slot, st in states.items():
            st_ref[slot] = st

    def test_decay(i):
        strong_ref[0] = (max_half_decay(i) > HGRN_SAFE_DECAY).astype(jnp.int32)

    def body(i, _):
        strong = strong_ref[0]

        @pl.when(strong == 0)
        def _():
            test_decay(jnp.minimum(i + 1, n_iter - 1))
            fast_step(i)

        @pl.when(strong != 0)
        def _():
            test_decay(jnp.minimum(i + 1, n_iter - 1))
            exact_step(i)

        return 0

    test_decay(0)
    lax.fori_loop(0, n_iter, body, 0)

    def fin(t, _):
        r = pl.multiple_of(t * ROW_TILE, ROW_TILE)
        for hd in range(HGRN_HEADS):
            cols = slice(HGRN_K * hd, HGRN_K * (hd + 1))
            o = of_ref[pl.ds(r, ROW_TILE), cols].astype(F32) + ob_ref[pl.ds(r, ROW_TILE), cols].astype(F32)
            ms = jnp.mean(o * o, axis=-1, keepdims=True)
            o = o * lax.rsqrt(ms + NORM_EPS) * ng_ref[...]
            o_ref[pl.ds(r, ROW_TILE), cols] = (o * g_ref[pl.ds(r, ROW_TILE), cols].astype(F32)).astype(BF16)
        return 0

    lax.fori_loop(0, t_len // ROW_TILE, fin, 0)


def _hgrn(hq, hv, lff, lfb, hg, norm_g, n_batch, t_len, n_ctx):
    def view(a):
        return a.reshape(n_batch, t_len, HGRN_WIDTH)

    blk = pl.BlockSpec((None, t_len, HGRN_WIDTH), lambda b: (b, 0, 0))
    out = pl.pallas_call(
        functools.partial(_hgrn_kernel, n_ctx=n_ctx, t_len=t_len),
        grid=(n_batch,),
        in_specs=[blk] * 5 + [pl.BlockSpec((1, HGRN_K), lambda b: (0, 0))],
        out_specs=blk,
        out_shape=jax.ShapeDtypeStruct((n_batch, t_len, HGRN_WIDTH), BF16),
        scratch_shapes=[pltpu.VMEM((t_len, HGRN_WIDTH), BF16), pltpu.VMEM((t_len, HGRN_WIDTH), BF16),
                        pltpu.VMEM((2 * HGRN_HEADS, HGRN_K, HGRN_K), F32),
                        pltpu.VMEM((HGRN_CHUNK, HGRN_K), F32), pltpu.VMEM((HGRN_CHUNK, HGRN_K), F32),
                        pltpu.SMEM((1,), jnp.int32)],
        compiler_params=_cparams(("parallel",)),
        name="hgrn2",
    )(view(hq), view(hv), view(lff), view(lfb), view(hg), norm_g)
    return out.reshape(n_batch * t_len, HGRN_WIDTH)


def _mix_residual_norm(x_ref, a_ref, b_ref, c_ref, w_ref, mod_ref, g_ref):
    y = (_dot(a_ref[...], w_ref[0:256, :]) + _dot(b_ref[...], w_ref[256:768, :])
         + _dot(c_ref[...], w_ref[768:1024, :]))
    x = x_ref[...] + mod_ref[0:1, :] * y
    ms = jnp.mean(x * x, axis=-1, keepdims=True)
    h = (x * lax.rsqrt(ms + NORM_EPS) * g_ref[...]) * (1.0 + mod_ref[2:3, :]) + mod_ref[1:2, :]
    return x, h


def _outproj_router_kernel(x_ref, a_ref, b_ref, c_ref, w_ref, mod_ref, g_ref, r_ref, xo_ref, h_ref, gate_ref):
    x, h = _mix_residual_norm(x_ref, a_ref, b_ref, c_ref, w_ref, mod_ref, g_ref)
    xo_ref[...] = x
    h_ref[...] = _pack_bf16_pairs(h)
    h_hi, h_lo = _split_bf16(h)
    r_hi, r_lo = _split_bf16(r_ref[...])
    logits = _dot(h_hi, r_hi) + _dot(h_lo, r_hi) + _dot(h_hi, r_lo)
    lane = lax.broadcasted_iota(jnp.int32, logits.shape, 1).astype(F32)
    logits = jnp.where(lane < N_EXPERTS, logits, NEG_BIG)
    v1 = jnp.max(logits, axis=-1, keepdims=True)
    i1 = jnp.min(jnp.where(logits == v1, lane, 128.0), axis=-1, keepdims=True)
    rest_l = jnp.where(lane == i1, NEG_BIG, logits)
    v2 = jnp.max(rest_l, axis=-1, keepdims=True)
    i2 = jnp.min(jnp.where(rest_l == v2, lane, 128.0), axis=-1, keepdims=True)
    e2 = jnp.exp(v2 - v1)
    w1 = 1.0 / (1.0 + e2)
    w2 = e2 * w1
    gate_ref[...] = (jnp.where(lane == i1, w1, 0.0) + jnp.where(lane == i2, w2, 0.0)
                     + jnp.where(lane == 8.0, i1, 0.0) + jnp.where(lane == 9.0, i2, 0.0)
                     + jnp.where(lane == 10.0, w1, 0.0) + jnp.where(lane == 11.0, w2, 0.0))


def _out_projection_router(xc, a, b, c, w, mod, g, router):
    n = xc.shape[0]
    nt = n // ROW_TILE
    row = lambda i: (i, 0)
    const = lambda i: (0, 0)
    return pl.pallas_call(
        _outproj_router_kernel,
        grid=(nt,),
        in_specs=[
            pl.BlockSpec((ROW_TILE, D_MODEL), row),
            pl.BlockSpec((ROW_TILE, DIFF_WIDTH), row),
            pl.BlockSpec((ROW_TILE, HGRN_WIDTH), row),
            pl.BlockSpec((ROW_TILE, SWA_WIDTH), row),
            pl.BlockSpec((D_MODEL, D_MODEL), const),
            pl.BlockSpec((None, 3, D_MODEL), lambda i: (i, 0, 0)),
            pl.BlockSpec((1, D_MODEL), const),
            pl.BlockSpec((D_MODEL, 128), const),
        ],
        out_specs=[pl.BlockSpec((ROW_TILE, D_MODEL), row), pl.BlockSpec((ROW_TILE, D_MODEL // 2), row),
                   pl.BlockSpec((ROW_TILE, 128), row)],
        out_shape=[jax.ShapeDtypeStruct((n, D_MODEL), F32), jax.ShapeDtypeStruct((n, D_MODEL // 2), jnp.int32),
                   jax.ShapeDtypeStruct((n, 128), F32)],
        compiler_params=_cparams(("parallel",)),
        name="out_projection",
    )(xc, a, b, c, w, mod, g, router)


def _outproj_ffn_kernel(x_ref, a_ref, b_ref, c_ref, wo_ref, mod_ref, g_ref, w1_ref, w3_ref, w2_ref, o_ref):
    x, h = _mix_residual_norm(x_ref, a_ref, b_ref, c_ref, wo_ref, mod_ref, g_ref)
    h = h.astype(BF16)
    u = _dot(h, w1_ref[...])
    act = (_silu(u) * _dot(h, w3_ref[...])).astype(BF16)
    o_ref[...] = x + mod_ref[3:4, :] * _dot(act, w2_ref[...])


def _outproj_dense_ffn(xc, a, b, c, wo, mod, g, w1, w3, w2):
    n = xc.shape[0]
    nt = n // ROW_TILE
    row = lambda i: (i, 0)
    const = lambda i: (0, 0)
    return pl.pallas_call(
        _outproj_ffn_kernel,
        grid=(nt,),
        in_specs=[
            pl.BlockSpec((ROW_TILE, D_MODEL), row),
            pl.BlockSpec((ROW_TILE, DIFF_WIDTH), row),
            pl.BlockSpec((ROW_TILE, HGRN_WIDTH), row),
            pl.BlockSpec((ROW_TILE, SWA_WIDTH), row),
            pl.BlockSpec((D_MODEL, D_MODEL), const),
            pl.BlockSpec((None, 4, D_MODEL), lambda i: (i, 0, 0)),
            pl.BlockSpec((1, D_MODEL), const),
            pl.BlockSpec((D_MODEL, D_FF), const),
            pl.BlockSpec((D_MODEL, D_FF), const),
            pl.BlockSpec((D_FF, D_MODEL), const),
        ],
        out_specs=pl.BlockSpec((ROW_TILE, D_MODEL), row),
        out_shape=jax.ShapeDtypeStruct((n, D_MODEL), F32),
        compiler_params=_cparams(("parallel",)),
        name="outproj_dense_ffn",
    )(xc, a, b, c, wo, mod, g, w1, w3, w2)


MOE_TM = 256
SC_CORES = 2
SC_SUBCORES = 16
SC_GATHER_ROWS = 128


def _pack_bf16_pairs(h):
    half = h.shape[1] // 2
    bits = pltpu.bitcast(h.astype(BF16).astype(F32), jnp.uint32)
    packed = (bits[:, :half] >> 16) | (bits[:, half:] & jnp.uint32(0xFFFF0000))
    return pltpu.bitcast(packed, jnp.int32)


def _unpack_bf16_pairs(u):
    bits = pltpu.bitcast(u, jnp.uint32)
    lo = pltpu.bitcast(bits << 16, F32)
    hi = pltpu.bitcast(bits & jnp.uint32(0xFFFF0000), F32)
    return jnp.concatenate([lo, hi], axis=1)


def _sc_gather_rows(table, idx):
    n_workers = SC_CORES * SC_SUBCORES
    half = SC_GATHER_ROWS // 2
    n_rows, width = idx.shape[0], table.shape[1]
    assert n_rows % (n_workers * SC_GATHER_ROWS) == 0
    pairs_per_worker = n_rows // (n_workers * SC_GATHER_ROWS)
    idx = idx.reshape(n_workers, 2 * pairs_per_worker, half)
    mesh = plsc.VectorSubcoreMesh(core_axis_name="c", subcore_axis_name="s",
                                  num_cores=SC_CORES, num_subcores=SC_SUBCORES)

    @functools.partial(
        pl.kernel, mesh=mesh,
        out_type=jax.ShapeDtypeStruct((n_rows, width), table.dtype),
        scratch_types=[pltpu.VMEM((2 * pairs_per_worker, half), jnp.int32),
                       pltpu.VMEM((half, width), table.dtype), pltpu.VMEM((half, width), table.dtype),
                       pltpu.SemaphoreType.DMA, pltpu.SemaphoreType.DMA],
        name="sc_gather_rows",
    )
    def gather(table_hbm, idx_hbm, out_hbm, idx_v, rows_a, rows_b, sem_a, sem_b):
        worker = lax.axis_index("s") * SC_CORES + lax.axis_index("c")
        first = worker * 2 * pairs_per_worker
        pltpu.sync_copy(idx_hbm.at[worker], idx_v)

        def fetch(c, buf, sem):
            return pltpu.make_async_copy(table_hbm.at[idx_v.at[c]], buf, sem)

        def drain(c, buf):
            off = pl.multiple_of((first + c) * half, half)
            pltpu.sync_copy(buf, out_hbm.at[pl.ds(off, half)])

        fetch(0, rows_a, sem_a).start()
        fetch(1, rows_b, sem_b).start()

        @pl.loop(0, pairs_per_worker)
        def _(p):
            more = p + 1 < pairs_per_worker
            fetch(2 * p, rows_a, sem_a).wait()
            drain(2 * p, rows_a)

            @pl.when(more)
            def _():
                fetch(2 * p + 2, rows_a, sem_a).start()

            fetch(2 * p + 1, rows_b, sem_b).wait()
            drain(2 * p + 1, rows_b)

            @pl.when(more)
            def _():
                fetch(2 * p + 3, rows_b, sem_b).start()

    return gather(table, idx)


def _sc_scatter_rows(rows, idx, n_out):
    n_workers = SC_CORES * SC_SUBCORES
    half = SC_GATHER_ROWS // 2
    n_rows, width = rows.shape
    assert n_rows % (n_workers * SC_GATHER_ROWS) == 0
    pairs_per_worker = n_rows // (n_workers * SC_GATHER_ROWS)
    idx = idx.reshape(2, n_workers, 2 * pairs_per_worker, half)
    mesh = plsc.VectorSubcoreMesh(core_axis_name="c", subcore_axis_name="s",
                                  num_cores=SC_CORES, num_subcores=SC_SUBCORES)

    @functools.partial(
        pl.kernel, mesh=mesh,
        out_type=jax.ShapeDtypeStruct((n_out, width), rows.dtype),
        scratch_types=[pltpu.VMEM((2, 2 * pairs_per_worker, half), jnp.int32),
                       pltpu.VMEM((half, width), rows.dtype), pltpu.VMEM((half, width), rows.dtype),
                       pltpu.SemaphoreType.DMA, pltpu.SemaphoreType.DMA,
                       pltpu.SemaphoreType.DMA, pltpu.SemaphoreType.DMA],
        name="sc_scatter_rows",
    )
    def scatter(rows_hbm, idx_hbm, out_hbm, idx_v, rows_a, rows_b, sem_a, sem_b, sem_0, sem_1):
        worker = lax.axis_index("s") * SC_CORES + lax.axis_index("c")
        first = worker * 2 * pairs_per_worker
        for k in range(2):
            pltpu.sync_copy(idx_hbm.at[k, worker], idx_v.at[k])

        def load(c, buf, sem):
            off = pl.multiple_of((first + c) * half, half)
            return pltpu.make_async_copy(rows_hbm.at[pl.ds(off, half)], buf, sem)

        def put(k, c, buf, sem):
            return pltpu.make_async_copy(buf, out_hbm.at[idx_v.at[k, c]], sem)

        def put_both(c, buf):
            put(0, c, buf, sem_0).start()
            put(1, c, buf, sem_1).start()
            put(0, c, buf, sem_0).wait()
            put(1, c, buf, sem_1).wait()

        load(0, rows_a, sem_a).start()

        @pl.loop(0, pairs_per_worker)
        def _(p):
            load(2 * p, rows_a, sem_a).wait()
            load(2 * p + 1, rows_b, sem_b).start()
            put_both(2 * p, rows_a)
            load(2 * p + 1, rows_b, sem_b).wait()

            @pl.when(p + 1 < pairs_per_worker)
            def _():
                load(2 * p + 2, rows_a, sem_a).start()

            put_both(2 * p + 1, rows_b)

    return scatter(rows, idx)


def _moe_expert_kernel(te_ref, tv_ref, xs_ref, w1_ref, w3_ref, w2_ref, ys_ref):
    r = pl.program_id(0)

    @pl.when(tv_ref[r] == 1)
    def _():
        h = _unpack_bf16_pairs(xs_ref[...]).astype(BF16)
        u = _dot(h, w1_ref[...])
        act = (_silu(u) * _dot(h, w3_ref[...])).astype(BF16)
        ys_ref[...] = _pack_bf16_pairs(_dot(act, w2_ref[...]))

    @pl.when(tv_ref[r] == 0)
    def _():
        ys_ref[...] = jnp.zeros_like(ys_ref)


def _moe_combine_kernel(x_ref, y0_ref, y1_ref, route_ref, mod_ref, o_ref):
    y = (route_ref[:, 10:11] * _unpack_bf16_pairs(y0_ref[...])
         + route_ref[:, 11:12] * _unpack_bf16_pairs(y1_ref[...]))
    o_ref[...] = x_ref[...] + mod_ref[...] * y


def _moe_plan(route, n_rt):
    n = route.shape[0]
    tm = MOE_TM
    gates_t = route[:, :N_EXPERTS].T
    sel = gates_t != 0.0
    si = sel.astype(jnp.int32)
    rank = jnp.cumsum(si, axis=1) - si
    ntile = (jnp.sum(si, axis=1) + tm - 1) // tm
    tile_end = jnp.cumsum(ntile)
    tile_off = tile_end - ntile
    used = tile_end[-1]
    n_rows = n_rt * tm
    dest = jnp.where(sel, tile_off[:, None] * tm + rank, n_rows)
    expert = jnp.arange(N_EXPERTS, dtype=jnp.int32)[:, None]

    def row_of(lane):
        pick = route[:, lane].astype(jnp.int32)[None, :]
        return jnp.sum(jnp.where(expert == pick, dest, 0), axis=0).astype(jnp.int32)

    pos = jnp.stack([row_of(8), row_of(9)])
    r = jnp.arange(n_rt, dtype=jnp.int32)
    rc = jnp.minimum(r, used - 1)
    tile_expert = jnp.sum((tile_end[None, :] <= rc[:, None]).astype(jnp.int32), axis=1)
    tile_valid = (r < used).astype(jnp.int32)
    return pos, tile_expert, tile_valid


def _moe_ffn(x, h_packed, route, w1, w3, w2, layer_idx, mod):
    n = x.shape[0]
    nt = n // ROW_TILE
    tm = MOE_TM
    half = D_MODEL // 2
    n_rt = 2 * n // tm + N_EXPERTS
    pos, tile_expert, tile_valid = _moe_plan(route, n_rt)

    xs = _sc_scatter_rows(h_packed, pos, n_rt * tm + SC_GATHER_ROWS)
    expert_w = lambda r, te, tv: (layer_idx, te[r], 0, 0)
    ys = pl.pallas_call(
        _moe_expert_kernel,
        grid_spec=pltpu.PrefetchScalarGridSpec(
            num_scalar_prefetch=2,
            grid=(n_rt,),
            in_specs=[
                pl.BlockSpec((tm, half), lambda r, te, tv: (r, 0)),
                pl.BlockSpec((None, None, D_MODEL, D_FF), expert_w),
                pl.BlockSpec((None, None, D_MODEL, D_FF), expert_w),
                pl.BlockSpec((None, None, D_FF, D_MODEL), expert_w),
            ],
            out_specs=pl.BlockSpec((tm, half), lambda r, te, tv: (r, 0)),
        ),
        out_shape=jax.ShapeDtypeStruct((n_rt * tm, half), jnp.int32),
        compiler_params=_cparams(("arbitrary",)),
        name="moe_experts",
    )(tile_expert, tile_valid, xs, w1, w3, w2)

    y = _sc_gather_rows(ys, jnp.where(pos >= n_rt * tm, 0, pos).reshape(-1))
    row = lambda i: (i, 0)
    return pl.pallas_call(
        _moe_combine_kernel,
        grid=(nt,),
        in_specs=[
            pl.BlockSpec((ROW_TILE, D_MODEL), row),
            pl.BlockSpec((ROW_TILE, half), row),
            pl.BlockSpec((ROW_TILE, half), lambda i: (nt + i, 0)),
            pl.BlockSpec((ROW_TILE, 128), row),
            pl.BlockSpec((None, 1, D_MODEL), lambda i: (i, 0, 0)),
        ],
        out_specs=pl.BlockSpec((ROW_TILE, D_MODEL), row),
        out_shape=jax.ShapeDtypeStruct((n, D_MODEL), F32),
        compiler_params=_cparams(("parallel",)),
        name="moe_combine",
    )(x, y, y, route, mod)


def _rope_tables(n_ctx, n_lat):
    pos_r = jnp.arange(n_lat, dtype=jnp.int32) // GRID_W
    pos_c = jnp.arange(n_lat, dtype=jnp.int32) % GRID_W

    def per_head(head_dim):
        nf = head_dim // 4
        inv = ROPE_BASE ** (-jnp.arange(nf, dtype=F32) / nf)
        ang_r = pos_r.astype(F32)[:, None] * inv[None, :]
        ang_c = pos_c.astype(F32)[:, None] * inv[None, :]
        z = jnp.zeros_like(ang_r)
        cos = jnp.concatenate([jnp.cos(ang_r)] * 2 + [jnp.cos(ang_c)] * 2, axis=-1)
        sa = jnp.concatenate([-jnp.sin(ang_r), z, -jnp.sin(ang_c), z], axis=-1)
        sb = jnp.concatenate([z, jnp.sin(ang_r), z, jnp.sin(ang_c)], axis=-1)
        reps = 256 // head_dim
        tabs = [jnp.tile(t, (1, reps)) for t in (cos, sa, sb)]
        ctx = [jnp.ones((n_ctx, 256), F32), jnp.zeros((n_ctx, 256), F32), jnp.zeros((n_ctx, 256), F32)]
        return [jnp.concatenate([c, t], axis=0) for c, t in zip(ctx, tabs)]

    return jnp.concatenate(per_head(DIFF_HEAD_DIM) + per_head(SWA_HEAD_DIM), axis=-1)


def _block_diag_mean(group):
    idx = jnp.arange(256) // group
    return jnp.where(idx[:, None] == idx[None, :], 1.0 / group, 0.0).astype(BF16)


def kernel(x, c, ctx, c_ctx, ada_w, ada_b, norm_mix_g, norm_ffn_g, w_in, w_out, diff_qk_norm_g, diff_lambda,
           diff_subln_g, hgrn_lb_logits, hgrn_norm_g, swa_qk_norm_g, swa_sink, ffn_w1, ffn_w3, ffn_w2,
           moe_router, moe_w1, moe_w3, moe_w2):
    n_batch, n_lat, _ = x.shape
    n_ctx = ctx.shape[1]
    depth = ada_w.shape[0]
    t_len = n_ctx + n_lat
    tps = t_len // ROW_TILE
    assert n_ctx == ROW_TILE and n_lat % ROW_TILE == 0 and n_lat >= ROW_TILE + 2 * WINDOW

    xc = jnp.concatenate([ctx, x], axis=1).reshape(n_batch * t_len, D_MODEL)

    n_rows = -(-(n_batch + 1) // 8) * 8
    cond = jnp.concatenate([c, c_ctx[None, :], jnp.zeros((n_rows - n_batch - 1, D_MODEL), F32)], axis=0)
    mods = _ada_modulation(cond, ada_w, ada_b).reshape(depth, n_rows, 6, D_MODEL)
    m_lat = jnp.broadcast_to(mods[:, :n_batch, None], (depth, n_batch, tps - 1, 6, D_MODEL))
    m_ctx = jnp.broadcast_to(mods[:, n_batch, None, None], (depth, n_batch, 1, 6, D_MODEL))
    mods = jnp.concatenate([m_ctx, m_lat], axis=2).reshape(depth, n_batch * tps, 6, D_MODEL)

    lb = jnp.cumsum(jax.nn.softmax(hgrn_lb_logits.astype(F32), axis=1), axis=1)
    lb = lb - lb[:, :1]
    rope = _rope_tables(n_ctx, n_lat)
    g32 = _block_diag_mean(DIFF_HEAD_DIM)
    g64 = _block_diag_mean(SWA_HEAD_DIM)

    perm_q = jnp.arange(SWA_WIDTH).reshape(SWA_KV_HEADS, SWA_GROUP, SWA_HEAD_DIM).transpose(1, 0, 2).reshape(-1)
    qc0 = 3 * DIFF_WIDTH + 5 * HGRN_WIDTH
    col_perm = jnp.concatenate([jnp.arange(qc0), qc0 + perm_q, jnp.arange(qc0 + SWA_WIDTH, D_IN)])
    oc0 = DIFF_WIDTH + HGRN_WIDTH
    row_perm = jnp.concatenate([jnp.arange(oc0), oc0 + perm_q])
    moe_w1_b, moe_w3_b, moe_w2_b = moe_w1.astype(BF16), moe_w3.astype(BF16), moe_w2.astype(BF16)

    for layer in range(depth):
        lam_init = 0.8 - 0.6 * math.exp(-0.3 * layer)
        mod = mods[layer]
        w_in_l = w_in[layer][:, col_perm].astype(BF16)
        w_out_l = w_out[layer][row_perm, :].astype(BF16)
        gains = jnp.stack([
            jnp.tile(diff_qk_norm_g[layer, 0], 8) * (DIFF_HEAD_DIM ** -0.5 * LOG2_E),
            jnp.tile(diff_qk_norm_g[layer, 1], 8),
            jnp.tile(swa_qk_norm_g[layer, 0], 4) * (SWA_HEAD_DIM ** -0.5 * LOG2_E),
            jnp.tile(swa_qk_norm_g[layer, 1], 4),
        ]).astype(F32)
        lbt = jnp.stack([
            jnp.log(lb[:, layer]).reshape(-1),
            jnp.log1p(-lb[:, layer]).reshape(-1),
        ]).astype(F32)
        lv = diff_lambda[layer].astype(F32)
        lam = jnp.exp(jnp.sum(lv[0] * lv[1])) - jnp.exp(jnp.sum(lv[2] * lv[3])) + lam_init
        lam_row = jnp.full((1, 128), lam, F32)
        post_row = (jnp.tile(diff_subln_g[layer], DIFF_HEADS) * (1.0 - lam_init)).reshape(1, DIFF_WIDTH).astype(F32)
        sink_tab = jnp.broadcast_to(swa_sink[layer].astype(F32)[:, None] * LOG2_E, (SWA_Q_HEADS, 128))

        (qa, ka, va, hq, hv, lff, lfb, hg, qc, kc, vc) = _in_projection(
            xc, mod[:, 0:2], norm_mix_g[layer].reshape(1, D_MODEL), w_in_l, rope, gains, lbt, g32, g64, tps)
        a = _diff_attention(qa, ka, va, lam_row, post_row, g64, n_batch, t_len, n_ctx)
        b = _hgrn(hq, hv, lff, lfb, hg, hgrn_norm_g[layer].reshape(1, HGRN_K), n_batch, t_len, n_ctx)
        cc = _swa_attention(qc, kc, vc, sink_tab, n_batch, t_len, n_ctx)

        jj = layer // 2
        g2 = norm_ffn_g[layer].reshape(1, D_MODEL)
        if layer % 2 == 1:
            router = jnp.pad(moe_router[jj].astype(F32), ((0, 0), (0, 128 - N_EXPERTS)))
            x_mid, h2, route = _out_projection_router(xc, a, b, cc, w_out_l, mod[:, 2:5], g2, router)
            if layer == depth - 1:
                is_lat = (jnp.arange(n_batch * t_len, dtype=jnp.int32) % t_len) >= n_ctx
                route = jnp.where(is_lat[:, None], route, 0.0)
            xc = _moe_ffn(x_mid, h2, route, moe_w1_b, moe_w3_b, moe_w2_b, jj, mod[:, 5:6])
        else:
            xc = _outproj_dense_ffn(xc, a, b, cc, w_out_l, mod[:, 2:6], g2, ffn_w1[jj].astype(BF16),
                                    ffn_w3[jj].astype(BF16), ffn_w2[jj].astype(BF16))

    return xc.reshape(n_batch, t_len, D_MODEL)[:, n_ctx:, :]
```

```python
import functools
import math

import jax
import jax.numpy as jnp
from jax import lax
from jax.experimental import pallas as pl
from jax.experimental.pallas import tpu as pltpu
from jax.experimental.pallas import tpu_sc as plsc

D_MODEL = 1024
GRID_W = 64
DIFF_HEADS = 4
DIFF_HEAD_DIM = 32
DIFF_V_DIM = 64
DIFF_WIDTH = 256
HGRN_HEADS = 4
HGRN_K = 128
HGRN_WIDTH = 512
SWA_Q_HEADS = 4
SWA_KV_HEADS = 2
SWA_GROUP = 2
SWA_HEAD_DIM = 64
SWA_WIDTH = 256
SWA_KV_WIDTH = 128
WINDOW = 128
D_FF = 2816
N_EXPERTS = 8
ROPE_BASE = 10000.0
NORM_EPS = 1e-6
D_IN = 3840

F32 = jnp.float32
BF16 = jnp.bfloat16

ROW_TILE = 256
HGRN_CHUNK = 64
HGRN_UNROLL = 4
HGRN_SAFE_DECAY = 80.0
NEG_BIG = -1e30
LOG2_E = 1.4426950408889634
VMEM_LIMIT = 56 * 1024 * 1024


def _cparams(sem):
    return pltpu.CompilerParams(dimension_semantics=sem, vmem_limit_bytes=VMEM_LIMIT)


def _split_bf16(v):
    hi = v.astype(BF16)
    lo = (v - hi.astype(F32)).astype(BF16)
    return hi, lo


def _dot(a, b):
    return jnp.dot(a, b, preferred_element_type=F32)


def _dot_nt(a, b):
    return lax.dot_general(a, b, (((1,), (1,)), ((), ())), preferred_element_type=F32)


def _dot_tn(a, b):
    return lax.dot_general(a, b, (((0,), (0,)), ((), ())), preferred_element_type=F32)


def _group_mean_sq(y, gmat):
    hi, lo = _split_bf16(y * y)
    return _dot(hi, gmat) + _dot(lo, gmat)


def _silu(v):
    return v * (1.0 / (1.0 + jnp.exp(-v)))


def _ada_kernel(s_ref, w_ref, b_ref, o_ref):
    s = s_ref[...]
    s = _silu(s)
    s_hi, s_lo = _split_bf16(s)
    w_hi, w_lo = _split_bf16(w_ref[...])
    o_ref[...] = _dot(s_hi, w_hi) + _dot(s_lo, w_hi) + _dot(s_hi, w_lo) + b_ref[...]


def _ada_modulation(cond, ada_w, ada_b):
    depth = ada_w.shape[0]
    r = cond.shape[0]
    nblk = 6 * D_MODEL // 1024
    return pl.pallas_call(
        _ada_kernel,
        grid=(depth, nblk),
        in_specs=[
            pl.BlockSpec((r, D_MODEL), lambda l, n: (0, 0)),
            pl.BlockSpec((None, D_MODEL, 1024), lambda l, n: (l, 0, n)),
            pl.BlockSpec((None, 1, 1024), lambda l, n: (l, 0, n)),
        ],
        out_specs=pl.BlockSpec((None, r, 1024), lambda l, n: (l, 0, n)),
        out_shape=jax.ShapeDtypeStruct((depth, r, 6 * D_MODEL), F32),
        compiler_params=_cparams(("parallel", "parallel")),
        name="ada_modulation",
    )(cond, ada_w, ada_b.reshape(depth, 1, 6 * D_MODEL))


def _rope(y, cos, sa, sb, half):
    w = y.shape[-1]
    fwd = pltpu.roll(y, w - half, 1)
    bwd = pltpu.roll(y, half, 1)
    return y * cos + fwd * sa + bwd * sb


def _inproj_kernel(x_ref, mod_ref, g_ref, w_ref, rope_ref, gains_ref, lbt_ref, ga_ref, gc_ref,
                   qa_ref, ka_ref, va_ref, hq_ref, hv_ref, lff_ref, lfb_ref, hg_ref,
                   qc_ref, kc_ref, vc_ref):
    x = x_ref[...]
    shift = mod_ref[0:1, :]
    scale = mod_ref[1:2, :]
    ms = jnp.mean(x * x, axis=-1, keepdims=True)
    h = (x * lax.rsqrt(ms + NORM_EPS) * g_ref[...]) * (1.0 + scale) + shift
    hb = h.astype(BF16)

    def proj(a, b):
        return _dot(hb, w_ref[:, a:b])

    def qk_prep(y, gmat, gain, cos, sa, sb, half):
        msq = _group_mean_sq(y, gmat)
        y = y * lax.rsqrt(msq + NORM_EPS) * gain
        return _rope(y, cos, sa, sb, half)

    ga = ga_ref[...]
    gc = gc_ref[...]
    ra = [rope_ref[:, 256 * i:256 * (i + 1)] for i in range(6)]
    qa = qk_prep(proj(0, 256), ga, gains_ref[0:1, :], ra[0], ra[1], ra[2], 8)
    qa_ref[...] = qa.astype(BF16)
    ka = qk_prep(proj(256, 512), ga, gains_ref[1:2, :], ra[0], ra[1], ra[2], 8)
    ka_ref[...] = ka.astype(BF16)
    va_ref[...] = proj(512, 768).astype(BF16)
    hq_ref[...] = _silu(proj(768, 1280)).astype(BF16)
    hv_ref[...] = proj(1280, 1792).astype(BF16)
    for d, lf_ref in enumerate((lff_ref, lfb_ref)):
        z = proj(1792 + 512 * d, 2304 + 512 * d)
        log_lb = lbt_ref[0:1, 512 * d:512 * (d + 1)]
        log1m_lb = lbt_ref[1:2, 512 * d:512 * (d + 1)]
        sp = jnp.maximum(-z, 0.0) + jnp.log(1.0 + jnp.exp(-jnp.abs(z)))
        b2 = log1m_lb - sp
        mx = jnp.maximum(log_lb, b2)
        lf_ref[...] = mx + jnp.log(1.0 + jnp.exp(-jnp.abs(log_lb - b2)))
    hg_ref[...] = _silu(proj(2816, 3328)).astype(BF16)
    qc = qk_prep(proj(3328, 3584), gc, gains_ref[2:3, :], ra[3], ra[4], ra[5], 16)
    qc_ref[...] = qc.astype(BF16)
    kc = qk_prep(proj(3584, 3712), gc[0:128, 0:128], gains_ref[3:4, 0:128],
                 ra[3][:, 0:128], ra[4][:, 0:128], ra[5][:, 0:128], 16)
    kc_ref[...] = kc.astype(BF16)
    vc_ref[...] = proj(3712, 3840).astype(BF16)


def _in_projection(xc, mod, g, w, rope, gains, lbt, ga, gc, tiles_per_seq):
    n = xc.shape[0]
    nt = n // ROW_TILE
    tps = tiles_per_seq

    def row(i):
        return (i, 0)

    def mod_idx(i):
        return (i, 0, 0)

    const = lambda i: (0, 0)
    widths = [(256, BF16), (256, BF16), (256, BF16), (512, BF16), (512, BF16),
              (512, F32), (512, F32), (512, BF16), (256, BF16), (128, BF16), (128, BF16)]
    return pl.pallas_call(
        _inproj_kernel,
        grid=(nt,),
        in_specs=[
            pl.BlockSpec((ROW_TILE, D_MODEL), row),
            pl.BlockSpec((None, 2, D_MODEL), mod_idx),
            pl.BlockSpec((1, D_MODEL), const),
            pl.BlockSpec((D_MODEL, D_IN), const),
            pl.BlockSpec((ROW_TILE, 6 * 256), lambda i: (i % tps, 0)),
            pl.BlockSpec((4, 256), const),
            pl.BlockSpec((2, 1024), const),
            pl.BlockSpec((256, 256), const),
            pl.BlockSpec((256, 256), const),
        ],
        out_specs=[pl.BlockSpec((ROW_TILE, wd), row) for wd, _ in widths],
        out_shape=[jax.ShapeDtypeStruct((n, wd), dt) for wd, dt in widths],
        compiler_params=_cparams(("parallel",)),
        name="in_projection",
    )(xc, mod, g, w, rope, gains, lbt, ga, gc)


def _diff_attend(q, k, v, lam):
    lane = lax.broadcasted_iota(jnp.int32, q.shape, 1)
    n_maps = 2 * DIFF_HEADS

    def scores(g):
        return _dot_nt(jnp.where(lane // DIFF_HEAD_DIM == g, q, jnp.zeros_like(q)), k)

    out = jnp.zeros(q.shape, F32)
    parts = []
    s_next = scores(0)
    for g in range(n_maps):
        s = s_next
        if g + 1 < n_maps:
            s_next = scores(g + 1)
        e = jnp.exp2(s - jnp.max(s, axis=-1, keepdims=True)).astype(BF16)
        o = _dot(e, v[g // 2])
        parts.append(o * (1.0 / pltpu.roll(o, DIFF_WIDTH - DIFF_V_DIM, 1)))
        if g % 2 == 1:
            out = out + jnp.where(lane // DIFF_V_DIM == g // 2, parts[g - 1] - lam * parts[g], 0.0)
    return out


def _diff_kernel(q_ref, k_ref, v_ref, lam_ref, post_ref, g64_ref, o_ref, vaug_ref, *, n_ctx):
    j = pl.program_id(1)
    lam = lam_ref[0:1, 0:1]

    @pl.when(j == 0)
    def _():
        v = v_ref[...]
        lane = lax.broadcasted_iota(jnp.int32, v.shape, 1)
        for hd in range(DIFF_HEADS):
            vaug_ref[hd] = jnp.where(lane // DIFF_V_DIM == (hd + 1) % DIFF_HEADS, jnp.ones_like(v), v)

    def finish(o):
        msq = _group_mean_sq(o, g64_ref[...])
        o_ref[...] = (o * lax.rsqrt(msq + NORM_EPS) * post_ref[...]).astype(BF16)

    @pl.when(j == 0)
    def _():
        finish(_diff_attend(q_ref[...], k_ref[0:n_ctx, :], [vaug_ref[hd, 0:n_ctx, :] for hd in range(DIFF_HEADS)],
                            lam))

    @pl.when(j > 0)
    def _():
        finish(_diff_attend(q_ref[...], k_ref[...], [vaug_ref[hd] for hd in range(DIFF_HEADS)], lam))


def _diff_attention(qa, ka, va, lam_row, post_row, g64, n_batch, t_len, n_ctx):
    tps = t_len // ROW_TILE
    q3 = qa.reshape(n_batch, t_len, DIFF_WIDTH)
    k3 = ka.reshape(n_batch, t_len, DIFF_WIDTH)
    v3 = va.reshape(n_batch, t_len, DIFF_WIDTH)
    const = lambda b, j: (0, 0)
    out = pl.pallas_call(
        functools.partial(_diff_kernel, n_ctx=n_ctx),
        grid=(n_batch, tps),
        in_specs=[
            pl.BlockSpec((None, ROW_TILE, DIFF_WIDTH), lambda b, j: (b, j, 0)),
            pl.BlockSpec((None, t_len, DIFF_WIDTH), lambda b, j: (b, 0, 0)),
            pl.BlockSpec((None, t_len, DIFF_WIDTH), lambda b, j: (b, 0, 0)),
            pl.BlockSpec((1, 128), const),
            pl.BlockSpec((1, DIFF_WIDTH), const),
            pl.BlockSpec((256, 256), const),
        ],
        out_specs=pl.BlockSpec((None, ROW_TILE, DIFF_WIDTH), lambda b, j: (b, j, 0)),
        out_shape=jax.ShapeDtypeStruct((n_batch, t_len, DIFF_WIDTH), BF16),
        scratch_shapes=[pltpu.VMEM((DIFF_HEADS, t_len, DIFF_WIDTH), BF16)],
        compiler_params=_cparams(("parallel", "arbitrary")),
        name="diff_attention",
    )(q3, k3, v3, lam_row, post_row, g64)
    return out.reshape(n_batch * t_len, DIFF_WIDTH)


def _swa_kernel(q_ref, k_ref, v_ref, sink_ref, o_ref, *, n_ctx, n_lat):
    j = pl.program_id(1)
    q = q_ref[...]
    tq = q.shape[0]
    lane = lax.broadcasted_iota(jnp.int32, (tq, SWA_KV_WIDTH), 1)
    span = ROW_TILE + 2 * WINDOW
    heads = [(g, hd) for g in range(SWA_GROUP) for hd in range(SWA_KV_HEADS)]

    def run(k, v, valid):
        vlane = lax.broadcasted_iota(jnp.int32, v.shape, 1)
        vaug = [jnp.where(vlane // SWA_HEAD_DIM == hd, v, jnp.ones_like(v)) for hd in range(SWA_KV_HEADS)]
        sinks = [sink_ref[SWA_GROUP * hd + g:SWA_GROUP * hd + g + 1, 0:1] for g, hd in heads]
        scores = []
        for g, hd in heads:
            qg = q[:, SWA_KV_WIDTH * g:SWA_KV_WIDTH * (g + 1)]
            s = _dot_nt(jnp.where(lane // SWA_HEAD_DIM == hd, qg, jnp.zeros_like(qg)), k)
            scores.append(s if valid is None else jnp.where(valid, s, NEG_BIG))
        maxes = [jnp.maximum(jnp.max(s, axis=-1, keepdims=True), sink) for s, sink in zip(scores, sinks)]
        outs = [_dot(jnp.exp2(s - mx).astype(BF16), vaug[hd]) for s, mx, (g, hd) in zip(scores, maxes, heads)]
        for g in range(SWA_GROUP):
            out = jnp.zeros((tq, SWA_KV_WIDTH), F32)
            for i, (gi, hd) in enumerate(heads):
                if gi == g:
                    den = pltpu.roll(outs[i], SWA_HEAD_DIM, 1) + jnp.exp2(sinks[i] - maxes[i])
                    out = out + jnp.where(lane // SWA_HEAD_DIM == hd, outs[i] * (1.0 / den), 0.0)
            o_ref[:, SWA_KV_WIDTH * g:SWA_KV_WIDTH * (g + 1)] = out.astype(BF16)

    @pl.when(j == 0)
    def _():
        run(k_ref[0:n_ctx, :], v_ref[0:n_ctx, :], None)

    @pl.when(j > 0)
    def _():
        q0 = (j - 1) * ROW_TILE
        ks = jnp.clip(q0 - WINDOW, 0, n_lat - span)
        ks = pl.multiple_of(ks, WINDOW)
        k = jnp.concatenate([k_ref[0:n_ctx, :], k_ref[pl.ds(n_ctx + ks, span), :]], axis=0)
        v = jnp.concatenate([v_ref[0:n_ctx, :], v_ref[pl.ds(n_ctx + ks, span), :]], axis=0)
        qpos = q0 + lax.broadcasted_iota(jnp.int32, (tq, n_ctx + span), 0)
        kpos = ks - n_ctx + lax.broadcasted_iota(jnp.int32, (tq, n_ctx + span), 1)
        run(k, v, (kpos < ks) | (jnp.abs(qpos - kpos) <= WINDOW))


def _swa_attention(qc, kc, vc, sink_tab, n_batch, t_len, n_ctx):
    tps = t_len // ROW_TILE
    q3 = qc.reshape(n_batch, t_len, SWA_WIDTH)
    k3 = kc.reshape(n_batch, t_len, SWA_KV_WIDTH)
    v3 = vc.reshape(n_batch, t_len, SWA_KV_WIDTH)
    out = pl.pallas_call(
        functools.partial(_swa_kernel, n_ctx=n_ctx, n_lat=t_len - n_ctx),
        grid=(n_batch, tps),
        in_specs=[
            pl.BlockSpec((None, ROW_TILE, SWA_WIDTH), lambda b, j: (b, j, 0)),
            pl.BlockSpec((None, t_len, SWA_KV_WIDTH), lambda b, j: (b, 0, 0)),
            pl.BlockSpec((None, t_len, SWA_KV_WIDTH), lambda b, j: (b, 0, 0)),
            pl.BlockSpec((SWA_Q_HEADS, 128), lambda b, j: (0, 0)),
        ],
        out_specs=pl.BlockSpec((None, ROW_TILE, SWA_WIDTH), lambda b, j: (b, j, 0)),
        out_shape=jax.ShapeDtypeStruct((n_batch, t_len, SWA_WIDTH), BF16),
        compiler_params=_cparams(("parallel", "arbitrary")),
        name="swa_attention",
    )(q3, k3, v3, sink_tab)
    return out.reshape(n_batch * t_len, SWA_WIDTH)


def _hgrn_kernel(q_ref, v_ref, lff_ref, lfb_ref, g_ref, ng_ref, o_ref, of_ref, ob_ref, st_ref, cum_ref, qrow_ref,
                 strong_ref, *, n_ctx, t_len):
    c = HGRN_CHUNK
    n_chunks = t_len // c
    n_ctx_chunks = n_ctx // c
    mid = c // 2
    ri = lax.broadcasted_iota(jnp.int32, (c, c), 0)
    ci = lax.broadcasted_iota(jnp.int32, (c, c), 1)
    causal = ri >= ci
    anti = ri <= ci
    tri_f = jnp.where(causal, 1.0, 0.0).astype(BF16)
    tri_b = jnp.where(anti, 1.0, 0.0).astype(BF16)

    st_ref[...] = jnp.zeros_like(st_ref)
    n_iter = n_chunks // HGRN_UNROLL

    def chains_of(i):
        chains = []
        for u in range(HGRN_UNROLL):
            step = i * HGRN_UNROLL + u
            rf = pl.multiple_of(step * c, c)
            cb = jnp.where(step < n_ctx_chunks, n_ctx_chunks - 1 - step, n_chunks - 1 + n_ctx_chunks - step)
            rb = pl.multiple_of(cb * c, c)
            for hd in range(HGRN_HEADS):
                cols = slice(HGRN_K * hd, HGRN_K * (hd + 1))
                chains.append((2 * hd, pl.ds(rf, c), cols, lff_ref, of_ref, tri_f, causal, mid - 1, c - 1))
                chains.append((2 * hd + 1, pl.ds(rb, c), cols, lfb_ref, ob_ref, tri_b, anti, mid, 0))
        return chains

    def max_half_decay(i):
        worst = jnp.zeros((1, HGRN_K), F32)
        for _, rows, cols, lf_ref, _, _, _, _, _ in chains_of(i):
            lf = lf_ref[rows, cols]
            worst = jnp.maximum(worst, jnp.maximum(jnp.abs(jnp.sum(lf[0:mid, :], axis=0, keepdims=True)),
                                                   jnp.abs(jnp.sum(lf[mid:c, :], axis=0, keepdims=True))))
        return jnp.max(worst)

    def exact_step(i):
        for slot, rows, cols, lf_ref, o_ref_d, tri, mask, _, end_row in chains_of(i):
            lf = lf_ref[rows, cols]
            lf_hi, lf_lo = _split_bf16(lf)
            cum = _dot(tri, lf_hi) + _dot(tri, lf_lo)
            tot = cum[end_row:end_row + 1, :]
            v = v_ref[rows, cols]
            k = 1.0 - jnp.exp(lf)
            q = q_ref[rows, cols].astype(F32)
            cum_ref[...] = cum
            qrow_ref[...] = q

            def row(t, sct):
                w = jnp.exp(jnp.minimum(cum_ref[pl.ds(t, 1), :] - cum, 0.0))
                col = jnp.sum(qrow_ref[pl.ds(t, 1), :] * k * w, axis=-1, keepdims=True)
                return jnp.where(ci == t, col, sct)

            sct = lax.fori_loop(0, c, row, jnp.zeros((c, c), F32))
            valid_t = anti if mask is causal else causal
            st = st_ref[slot]
            o = (_dot_tn(jnp.where(valid_t, sct, 0.0).astype(BF16), v)
                 + _dot_nt((q * jnp.exp(cum)).astype(BF16), st.astype(BF16)))
            o_ref_d[rows, cols] = o.astype(BF16)
            st_ref[slot] = st * jnp.exp(tot) + _dot_tn(v, (k * jnp.exp(tot - cum)).astype(BF16))

    def fast_step(i):
        chains = chains_of(i)
        cums = []
        for _, rows, cols, lf_ref, _, tri, _, _, _ in chains:
            lf_hi, lf_lo = _split_bf16(lf_ref[rows, cols])
            cums.append(_dot(tri, lf_hi) + _dot(tri, lf_lo))
        prods = []
        for (slot, rows, cols, lf_ref, _, _, mask, ref_row, end_row), cum in zip(chains, cums):
            ref = cum[ref_row:ref_row + 1, :]
            tot = cum[end_row:end_row + 1, :]
            v = v_ref[rows, cols]
            k = 1.0 - jnp.exp(lf_ref[rows, cols])
            qt = q_ref[rows, cols].astype(F32) * jnp.exp(cum - ref)
            kt = k * jnp.exp(ref - cum)
            sc = _dot_nt(qt.astype(BF16), kt.astype(BF16))
            upd = _dot_tn(v, (kt * jnp.exp(tot - ref)).astype(BF16))
            prods.append((sc, (qt * jnp.exp(ref)).astype(BF16), jnp.exp(tot), upd, v))
        states = {}
        for (slot, rows, cols, _, o_ref_d, _, mask, _, _), (sc, q_in, decay, upd, v) in zip(chains, prods):
            st = states[slot] if slot in states else st_ref[slot]
            o = _dot(jnp.where(mask, sc, 0.0).astype(BF16), v) + _dot_nt(q_in, st.astype(BF16))
            o_ref_d[rows, cols] = o.astype(BF16)
            states[slot] = st * decay + upd
        for slot, st in states.items():
            st_ref[slot] = st

    def test_decay(i):
        strong_ref[0] = (max_half_decay(i) > HGRN_SAFE_DECAY).astype(jnp.int32)

    def body(i, _):
        strong = strong_ref[0]

        @pl.when(strong == 0)
        def _():
            test_decay(jnp.minimum(i + 1, n_iter - 1))
            fast_step(i)

        @pl.when(strong != 0)
        def _():
            test_decay(jnp.minimum(i + 1, n_iter - 1))
            exact_step(i)

        return 0

    test_decay(0)
    lax.fori_loop(0, n_iter, body, 0)

    def fin(t, _):
        r = pl.multiple_of(t * ROW_TILE, ROW_TILE)
        for hd in range(HGRN_HEADS):
            cols = slice(HGRN_K * hd, HGRN_K * (hd + 1))
            o = of_ref[pl.ds(r, ROW_TILE), cols].astype(F32) + ob_ref[pl.ds(r, ROW_TILE), cols].astype(F32)
            ms = jnp.mean(o * o, axis=-1, keepdims=True)
            o = o * lax.rsqrt(ms + NORM_EPS) * ng_ref[...]
            o_ref[pl.ds(r, ROW_TILE), cols] = (o * g_ref[pl.ds(r, ROW_TILE), cols].astype(F32)).astype(BF16)
        return 0

    lax.fori_loop(0, t_len // ROW_TILE, fin, 0)


def _hgrn(hq, hv, lff, lfb, hg, norm_g, n_batch, t_len, n_ctx):
    def view(a):
        return a.reshape(n_batch, t_len, HGRN_WIDTH)

    blk = pl.BlockSpec((None, t_len, HGRN_WIDTH), lambda b: (b, 0, 0))
    out = pl.pallas_call(
        functools.partial(_hgrn_kernel, n_ctx=n_ctx, t_len=t_len),
        grid=(n_batch,),
        in_specs=[blk] * 5 + [pl.BlockSpec((1, HGRN_K), lambda b: (0, 0))],
        out_specs=blk,
        out_shape=jax.ShapeDtypeStruct((n_batch, t_len, HGRN_WIDTH), BF16),
        scratch_shapes=[pltpu.VMEM((t_len, HGRN_WIDTH), BF16), pltpu.VMEM((t_len, HGRN_WIDTH), BF16),
                        pltpu.VMEM((2 * HGRN_HEADS, HGRN_K, HGRN_K), F32),
                        pltpu.VMEM((HGRN_CHUNK, HGRN_K), F32), pltpu.VMEM((HGRN_CHUNK, HGRN_K), F32),
                        pltpu.SMEM((1,), jnp.int32)],
        compiler_params=_cparams(("parallel",)),
        name="hgrn2",
    )(view(hq), view(hv), view(lff), view(lfb), view(hg), norm_g)
    return out.reshape(n_batch * t_len, HGRN_WIDTH)


def _mix_residual_norm(x_ref, a_ref, b_ref, c_ref, w_ref, mod_ref, g_ref):
    y = (_dot(a_ref[...], w_ref[0:256, :]) + _dot(b_ref[...], w_ref[256:768, :])
         + _dot(c_ref[...], w_ref[768:1024, :]))
    x = x_ref[...] + mod_ref[0:1, :] * y
    ms = jnp.mean(x * x, axis=-1, keepdims=True)
    h = (x * lax.rsqrt(ms + NORM_EPS) * g_ref[...]) * (1.0 + mod_ref[2:3, :]) + mod_ref[1:2, :]
    return x, h


def _outproj_router_kernel(x_ref, a_ref, b_ref, c_ref, w_ref, mod_ref, g_ref, r_ref, xo_ref, h_ref, gate_ref):
    x, h = _mix_residual_norm(x_ref, a_ref, b_ref, c_ref, w_ref, mod_ref, g_ref)
    xo_ref[...] = x
    h_ref[...] = _pack_bf16_pairs(h)
    h_hi, h_lo = _split_bf16(h)
    r_hi, r_lo = _split_bf16(r_ref[...])
    logits = _dot(h_hi, r_hi) + _dot(h_lo, r_hi) + _dot(h_hi, r_lo)
    lane = lax.broadcasted_iota(jnp.int32, logits.shape, 1).astype(F32)
    logits = jnp.where(lane < N_EXPERTS, logits, NEG_BIG)
    v1 = jnp.max(logits, axis=-1, keepdims=True)
    i1 = jnp.min(jnp.where(logits == v1, lane, 128.0), axis=-1, keepdims=True)
    rest_l = jnp.where(lane == i1, NEG_BIG, logits)
    v2 = jnp.max(rest_l, axis=-1, keepdims=True)
    i2 = jnp.min(jnp.where(rest_l == v2, lane, 128.0), axis=-1, keepdims=True)
    e2 = jnp.exp(v2 - v1)
    w1 = 1.0 / (1.0 + e2)
    w2 = e2 * w1
    gate_ref[...] = (jnp.where(lane == i1, w1, 0.0) + jnp.where(lane == i2, w2, 0.0)
                     + jnp.where(lane == 8.0, i1, 0.0) + jnp.where(lane == 9.0, i2, 0.0)
                     + jnp.where(lane == 10.0, w1, 0.0) + jnp.where(lane == 11.0, w2, 0.0))


def _out_projection_router(xc, a, b, c, w, mod, g, router):
    n = xc.shape[0]
    nt = n // ROW_TILE
    row = lambda i: (i, 0)
    const = lambda i: (0, 0)
    return pl.pallas_call(
        _outproj_router_kernel,
        grid=(nt,),
        in_specs=[
            pl.BlockSpec((ROW_TILE, D_MODEL), row),
            pl.BlockSpec((ROW_TILE, DIFF_WIDTH), row),
            pl.BlockSpec((ROW_TILE, HGRN_WIDTH), row),
            pl.BlockSpec((ROW_TILE, SWA_WIDTH), row),
            pl.BlockSpec((D_MODEL, D_MODEL), const),
            pl.BlockSpec((None, 3, D_MODEL), lambda i: (i, 0, 0)),
            pl.BlockSpec((1, D_MODEL), const),
            pl.BlockSpec((D_MODEL, 128), const),
        ],
        out_specs=[pl.BlockSpec((ROW_TILE, D_MODEL), row), pl.BlockSpec((ROW_TILE, D_MODEL // 2), row),
                   pl.BlockSpec((ROW_TILE, 128), row)],
        out_shape=[jax.ShapeDtypeStruct((n, D_MODEL), F32), jax.ShapeDtypeStruct((n, D_MODEL // 2), jnp.int32),
                   jax.ShapeDtypeStruct((n, 128), F32)],
        compiler_params=_cparams(("parallel",)),
        name="out_projection",
    )(xc, a, b, c, w, mod, g, router)


def _outproj_ffn_kernel(x_ref, a_ref, b_ref, c_ref, wo_ref, mod_ref, g_ref, w1_ref, w3_ref, w2_ref, o_ref):
    x, h = _mix_residual_norm(x_ref, a_ref, b_ref, c_ref, wo_ref, mod_ref, g_ref)
    h = h.astype(BF16)
    u = _dot(h, w1_ref[...])
    act = (_silu(u) * _dot(h, w3_ref[...])).astype(BF16)
    o_ref[...] = x + mod_ref[3:4, :] * _dot(act, w2_ref[...])


def _outproj_dense_ffn(xc, a, b, c, wo, mod, g, w1, w3, w2):
    n = xc.shape[0]
    nt = n // ROW_TILE
    row = lambda i: (i, 0)
    const = lambda i: (0, 0)
    return pl.pallas_call(
        _outproj_ffn_kernel,
        grid=(nt,),
        in_specs=[
            pl.BlockSpec((ROW_TILE, D_MODEL), row),
            pl.BlockSpec((ROW_TILE, DIFF_WIDTH), row),
            pl.BlockSpec((ROW_TILE, HGRN_WIDTH), row),
            pl.BlockSpec((ROW_TILE, SWA_WIDTH), row),
            pl.BlockSpec((D_MODEL, D_MODEL), const),
            pl.BlockSpec((None, 4, D_MODEL), lambda i: (i, 0, 0)),
            pl.BlockSpec((1, D_MODEL), const),
            pl.BlockSpec((D_MODEL, D_FF), const),
            pl.BlockSpec((D_MODEL, D_FF), const),
            pl.BlockSpec((D_FF, D_MODEL), const),
        ],
        out_specs=pl.BlockSpec((ROW_TILE, D_MODEL), row),
        out_shape=jax.ShapeDtypeStruct((n, D_MODEL), F32),
        compiler_params=_cparams(("parallel",)),
        name="outproj_dense_ffn",
    )(xc, a, b, c, wo, mod, g, w1, w3, w2)


MOE_TM = 256
SC_CORES = 2
SC_SUBCORES = 16
SC_GATHER_ROWS = 128
SC_SPILL_ROWS = 4096


def _pack_bf16_pairs(h):
    half = h.shape[1] // 2
    bits = pltpu.bitcast(h.astype(BF16).astype(F32), jnp.uint32)
    packed = (bits[:, :half] >> 16) | (bits[:, half:] & jnp.uint32(0xFFFF0000))
    return pltpu.bitcast(packed, jnp.int32)


def _unpack_bf16_pairs(u):
    bits = pltpu.bitcast(u, jnp.uint32)
    lo = pltpu.bitcast(bits << 16, F32)
    hi = pltpu.bitcast(bits & jnp.uint32(0xFFFF0000), F32)
    return jnp.concatenate([lo, hi], axis=1)


def _sc_gather_rows(table, idx):
    n_workers = SC_CORES * SC_SUBCORES
    n_rows, width = idx.shape[0], table.shape[1]
    assert n_rows % (n_workers * SC_GATHER_ROWS) == 0
    per_worker = n_rows // n_workers
    mesh = plsc.VectorSubcoreMesh(core_axis_name="c", subcore_axis_name="s",
                                  num_cores=SC_CORES, num_subcores=SC_SUBCORES)

    @functools.partial(
        pl.kernel, mesh=mesh,
        out_type=jax.ShapeDtypeStruct((n_rows, width), table.dtype),
        scratch_types=[pltpu.VMEM((SC_GATHER_ROWS,), jnp.int32),
                       pltpu.VMEM((SC_GATHER_ROWS, width), table.dtype),
                       pltpu.SemaphoreType.DMA],
        name="sc_gather_rows",
    )
    def gather(table_hbm, idx_hbm, out_hbm, idx_v, rows_v, sem):
        worker = lax.axis_index("s") * SC_CORES + lax.axis_index("c")
        base = worker * per_worker

        @pl.loop(0, per_worker // SC_GATHER_ROWS)
        def _(step):
            off = pl.multiple_of(base + step * SC_GATHER_ROWS, SC_GATHER_ROWS)
            pltpu.sync_copy(idx_hbm.at[pl.ds(off, SC_GATHER_ROWS)], idx_v)
            pltpu.async_copy(table_hbm.at[idx_v], rows_v, sem).wait()
            pltpu.sync_copy(rows_v, out_hbm.at[pl.ds(off, SC_GATHER_ROWS)])

    return gather(table, idx)


def _sc_scatter_rows(rows, idx, n_out):
    n_workers = SC_CORES * SC_SUBCORES
    n_rows, width = rows.shape
    assert n_rows % (n_workers * SC_GATHER_ROWS) == 0
    chunks_per_worker = n_rows // (n_workers * SC_GATHER_ROWS)
    idx = idx.reshape(2, n_rows // SC_GATHER_ROWS, SC_GATHER_ROWS)
    mesh = plsc.VectorSubcoreMesh(core_axis_name="c", subcore_axis_name="s",
                                  num_cores=SC_CORES, num_subcores=SC_SUBCORES)

    @functools.partial(
        pl.kernel, mesh=mesh,
        out_type=jax.ShapeDtypeStruct((n_out, width), rows.dtype),
        scratch_types=[pltpu.VMEM((SC_GATHER_ROWS,), jnp.int32), pltpu.VMEM((SC_GATHER_ROWS,), jnp.int32),
                       pltpu.VMEM((SC_GATHER_ROWS, width), rows.dtype),
                       pltpu.SemaphoreType.DMA, pltpu.SemaphoreType.DMA],
        name="sc_scatter_rows",
    )
    def scatter(rows_hbm, idx_hbm, out_hbm, idx0_v, idx1_v, rows_v, sem0, sem1):
        worker = lax.axis_index("s") * SC_CORES + lax.axis_index("c")

        @pl.loop(0, chunks_per_worker)
        def _(step):
            chunk = worker * chunks_per_worker + step
            off = pl.multiple_of(chunk * SC_GATHER_ROWS, SC_GATHER_ROWS)
            pltpu.sync_copy(rows_hbm.at[pl.ds(off, SC_GATHER_ROWS)], rows_v)
            pltpu.sync_copy(idx_hbm.at[0, chunk], idx0_v)
            first = pltpu.async_copy(rows_v, out_hbm.at[idx0_v], sem0)
            pltpu.sync_copy(idx_hbm.at[1, chunk], idx1_v)
            second = pltpu.async_copy(rows_v, out_hbm.at[idx1_v], sem1)
            first.wait()
            second.wait()

    return scatter(rows, idx)


def _moe_expert_kernel(te_ref, tv_ref, xs_ref, w1_ref, w3_ref, w2_ref, ys_ref):
    r = pl.program_id(0)

    @pl.when(tv_ref[r] == 1)
    def _():
        h = _unpack_bf16_pairs(xs_ref[...]).astype(BF16)
        u = _dot(h, w1_ref[...])
        act = (_silu(u) * _dot(h, w3_ref[...])).astype(BF16)
        ys_ref[...] = _pack_bf16_pairs(_dot(act, w2_ref[...]))

    @pl.when(tv_ref[r] == 0)
    def _():
        ys_ref[...] = jnp.zeros_like(ys_ref)


def _moe_combine_kernel(x_ref, y0_ref, y1_ref, route_ref, mod_ref, o_ref):
    g0, g1 = route_ref[:, 10:11], route_ref[:, 11:12]
    y = (jnp.where(g0 != 0.0, g0 * _unpack_bf16_pairs(y0_ref[...]), 0.0)
         + jnp.where(g1 != 0.0, g1 * _unpack_bf16_pairs(y1_ref[...]), 0.0))
    o_ref[...] = x_ref[...] + mod_ref[...] * y


def _moe_plan(route, n_rt):
    n = route.shape[0]
    tm = MOE_TM
    gates_t = route[:, :N_EXPERTS].T
    sel = gates_t != 0.0
    si = sel.astype(jnp.int32)
    rank = jnp.cumsum(si, axis=1) - si
    ntile = (jnp.sum(si, axis=1) + tm - 1) // tm
    tile_end = jnp.cumsum(ntile)
    tile_off = tile_end - ntile
    used = tile_end[-1]
    n_rows = n_rt * tm
    dest = jnp.where(sel, tile_off[:, None] * tm + rank, n_rows)
    expert = jnp.arange(N_EXPERTS, dtype=jnp.int32)[:, None]

    def row_of(lane):
        pick = route[:, lane].astype(jnp.int32)[None, :]
        return jnp.sum(jnp.where(expert == pick, dest, 0), axis=0).astype(jnp.int32)

    pos = jnp.stack([row_of(8), row_of(9)])
    r = jnp.arange(n_rt, dtype=jnp.int32)
    rc = jnp.minimum(r, used - 1)
    tile_expert = jnp.sum((tile_end[None, :] <= rc[:, None]).astype(jnp.int32), axis=1)
    tile_valid = (r < used).astype(jnp.int32)
    return pos, tile_expert, tile_valid


def _moe_ffn(x, h_packed, route, w1, w3, w2, layer_idx, mod):
    n = x.shape[0]
    nt = n // ROW_TILE
    tm = MOE_TM
    half = D_MODEL // 2
    n_rt = 2 * n // tm + N_EXPERTS
    pos, tile_expert, tile_valid = _moe_plan(route, n_rt)
    routed = pos < n_rt * tm
    token = jnp.arange(n, dtype=jnp.int32)[None, :]

    xs = _sc_scatter_rows(h_packed, jnp.where(routed, pos, n_rt * tm + token % SC_SPILL_ROWS),
                          n_rt * tm + SC_SPILL_ROWS)
    expert_w = lambda r, te, tv: (layer_idx, te[r], 0, 0)
    ys = pl.pallas_call(
        _moe_expert_kernel,
        grid_spec=pltpu.PrefetchScalarGridSpec(
            num_scalar_prefetch=2,
            grid=(n_rt,),
            in_specs=[
                pl.BlockSpec((tm, half), lambda r, te, tv: (r, 0)),
                pl.BlockSpec((None, None, D_MODEL, D_FF), expert_w),
                pl.BlockSpec((None, None, D_MODEL, D_FF), expert_w),
                pl.BlockSpec((None, None, D_FF, D_MODEL), expert_w),
            ],
            out_specs=pl.BlockSpec((tm, half), lambda r, te, tv: (r, 0)),
        ),
        out_shape=jax.ShapeDtypeStruct((n_rt * tm, half), jnp.int32),
        compiler_params=_cparams(("arbitrary",)),
        name="moe_experts",
    )(tile_expert, tile_valid, xs, w1, w3, w2)

    y = _sc_gather_rows(ys, jnp.where(routed, pos, token).reshape(-1))
    row = lambda i: (i, 0)
    return pl.pallas_call(
        _moe_combine_kernel,
        grid=(nt,),
        in_specs=[
            pl.BlockSpec((ROW_TILE, D_MODEL), row),
            pl.BlockSpec((ROW_TILE, half), row),
            pl.BlockSpec((ROW_TILE, half), lambda i: (nt + i, 0)),
            pl.BlockSpec((ROW_TILE, 128), row),
            pl.BlockSpec((None, 1, D_MODEL), lambda i: (i, 0, 0)),
        ],
        out_specs=pl.BlockSpec((ROW_TILE, D_MODEL), row),
        out_shape=jax.ShapeDtypeStruct((n, D_MODEL), F32),
        compiler_params=_cparams(("parallel",)),
        name="moe_combine",
    )(x, y, y, route, mod)


def _rope_tables(n_ctx, n_lat):
    pos_r = jnp.arange(n_lat, dtype=jnp.int32) // GRID_W
    pos_c = jnp.arange(n_lat, dtype=jnp.int32) % GRID_W

    def per_head(head_dim):
        nf = head_dim // 4
        inv = ROPE_BASE ** (-jnp.arange(nf, dtype=F32) / nf)
        ang_r = pos_r.astype(F32)[:, None] * inv[None, :]
        ang_c = pos_c.astype(F32)[:, None] * inv[None, :]
        z = jnp.zeros_like(ang_r)
        cos = jnp.concatenate([jnp.cos(ang_r)] * 2 + [jnp.cos(ang_c)] * 2, axis=-1)
        sa = jnp.concatenate([-jnp.sin(ang_r), z, -jnp.sin(ang_c), z], axis=-1)
        sb = jnp.concatenate([z, jnp.sin(ang_r), z, jnp.sin(ang_c)], axis=-1)
        reps = 256 // head_dim
        tabs = [jnp.tile(t, (1, reps)) for t in (cos, sa, sb)]
        ctx = [jnp.ones((n_ctx, 256), F32), jnp.zeros((n_ctx, 256), F32), jnp.zeros((n_ctx, 256), F32)]
        return [jnp.concatenate([c, t], axis=0) for c, t in zip(ctx, tabs)]

    return jnp.concatenate(per_head(DIFF_HEAD_DIM) + per_head(SWA_HEAD_DIM), axis=-1)


def _block_diag_mean(group):
    idx = jnp.arange(256) // group
    return jnp.where(idx[:, None] == idx[None, :], 1.0 / group, 0.0).astype(BF16)


def kernel(x, c, ctx, c_ctx, ada_w, ada_b, norm_mix_g, norm_ffn_g, w_in, w_out, diff_qk_norm_g, diff_lambda,
           diff_subln_g, hgrn_lb_logits, hgrn_norm_g, swa_qk_norm_g, swa_sink, ffn_w1, ffn_w3, ffn_w2,
           moe_router, moe_w1, moe_w3, moe_w2):
    n_batch, n_lat, _ = x.shape
    n_ctx = ctx.shape[1]
    depth = ada_w.shape[0]
    t_len = n_ctx + n_lat
    tps = t_len // ROW_TILE
    assert n_ctx == ROW_TILE and n_lat % ROW_TILE == 0 and n_lat >= ROW_TILE + 2 * WINDOW

    xc = jnp.concatenate([ctx, x], axis=1).reshape(n_batch * t_len, D_MODEL)

    n_rows = -(-(n_batch + 1) // 8) * 8
    cond = jnp.concatenate([c, c_ctx[None, :], jnp.zeros((n_rows - n_batch - 1, D_MODEL), F32)], axis=0)
    mods = _ada_modulation(cond, ada_w, ada_b).reshape(depth, n_rows, 6, D_MODEL)
    m_lat = jnp.broadcast_to(mods[:, :n_batch, None], (depth, n_batch, tps - 1, 6, D_MODEL))
    m_ctx = jnp.broadcast_to(mods[:, n_batch, None, None], (depth, n_batch, 1, 6, D_MODEL))
    mods = jnp.concatenate([m_ctx, m_lat], axis=2).reshape(depth, n_batch * tps, 6, D_MODEL)

    lb = jnp.cumsum(jax.nn.softmax(hgrn_lb_logits.astype(F32), axis=1), axis=1)
    lb = lb - lb[:, :1]
    rope = _rope_tables(n_ctx, n_lat)
    g32 = _block_diag_mean(DIFF_HEAD_DIM)
    g64 = _block_diag_mean(SWA_HEAD_DIM)

    perm_q = jnp.arange(SWA_WIDTH).reshape(SWA_KV_HEADS, SWA_GROUP, SWA_HEAD_DIM).transpose(1, 0, 2).reshape(-1)
    qc0 = 3 * DIFF_WIDTH + 5 * HGRN_WIDTH
    col_perm = jnp.concatenate([jnp.arange(qc0), qc0 + perm_q, jnp.arange(qc0 + SWA_WIDTH, D_IN)])
    oc0 = DIFF_WIDTH + HGRN_WIDTH
    row_perm = jnp.concatenate([jnp.arange(oc0), oc0 + perm_q])
    moe_w1_b, moe_w3_b, moe_w2_b = moe_w1.astype(BF16), moe_w3.astype(BF16), moe_w2.astype(BF16)

    for layer in range(depth):
        lam_init = 0.8 - 0.6 * math.exp(-0.3 * layer)
        mod = mods[layer]
        w_in_l = w_in[layer][:, col_perm].astype(BF16)
        w_out_l = w_out[layer][row_perm, :].astype(BF16)
        gains = jnp.stack([
            jnp.tile(diff_qk_norm_g[layer, 0], 8) * (DIFF_HEAD_DIM ** -0.5 * LOG2_E),
            jnp.tile(diff_qk_norm_g[layer, 1], 8),
            jnp.tile(swa_qk_norm_g[layer, 0], 4) * (SWA_HEAD_DIM ** -0.5 * LOG2_E),
            jnp.tile(swa_qk_norm_g[layer, 1], 4),
        ]).astype(F32)
        lbt = jnp.stack([
            jnp.log(lb[:, layer]).reshape(-1),
            jnp.log1p(-lb[:, layer]).reshape(-1),
        ]).astype(F32)
        lv = diff_lambda[layer].astype(F32)
        lam = jnp.exp(jnp.sum(lv[0] * lv[1])) - jnp.exp(jnp.sum(lv[2] * lv[3])) + lam_init
        lam_row = jnp.full((1, 128), lam, F32)
        post_row = (jnp.tile(diff_subln_g[layer], DIFF_HEADS) * (1.0 - lam_init)).reshape(1, DIFF_WIDTH).astype(F32)
        sink_tab = jnp.broadcast_to(swa_sink[layer].astype(F32)[:, None] * LOG2_E, (SWA_Q_HEADS, 128))

        (qa, ka, va, hq, hv, lff, lfb, hg, qc, kc, vc) = _in_projection(
            xc, mod[:, 0:2], norm_mix_g[layer].reshape(1, D_MODEL), w_in_l, rope, gains, lbt, g32, g64, tps)
        a = _diff_attention(qa, ka, va, lam_row, post_row, g64, n_batch, t_len, n_ctx)
        b = _hgrn(hq, hv, lff, lfb, hg, hgrn_norm_g[layer].reshape(1, HGRN_K), n_batch, t_len, n_ctx)
        cc = _swa_attention(qc, kc, vc, sink_tab, n_batch, t_len, n_ctx)

        jj = layer // 2
        g2 = norm_ffn_g[layer].reshape(1, D_MODEL)
        if layer % 2 == 1:
            router = jnp.pad(moe_router[jj].astype(F32), ((0, 0), (0, 128 - N_EXPERTS)))
            x_mid, h2, route = _out_projection_router(xc, a, b, cc, w_out_l, mod[:, 2:5], g2, router)
            if layer == depth - 1:
                is_lat = (jnp.arange(n_batch * t_len, dtype=jnp.int32) % t_len) >= n_ctx
                route = jnp.where(is_lat[:, None], route, 0.0)
            xc = _moe_ffn(x_mid, h2, route, moe_w1_b, moe_w3_b, moe_w2_b, jj, mod[:, 5:6])
        else:
            xc = _outproj_dense_ffn(xc, a, b, cc, w_out_l, mod[:, 2:6], g2, ffn_w1[jj].astype(BF16),
                                    ffn_w3[jj].astype(BF16), ffn_w2[jj].astype(BF16))

    return xc.reshape(n_batch, t_len, D_MODEL)[:, n_ctx:, :]
```

```python
import functools
import math

import jax
import jax.numpy as jnp
from jax import lax
from jax.experimental import pallas as pl
from jax.experimental.pallas import tpu as pltpu
from jax.experimental.pallas import tpu_sc as plsc

D_MODEL = 1024
GRID_W = 64
DIFF_HEADS = 4
DIFF_HEAD_DIM = 32
DIFF_V_DIM = 64
DIFF_WIDTH = 256
HGRN_HEADS = 4
HGRN_K = 128
HGRN_WIDTH = 512
SWA_Q_HEADS = 4
SWA_KV_HEADS = 2
SWA_GROUP = 2
SWA_HEAD_DIM = 64
SWA_WIDTH = 256
SWA_KV_WIDTH = 128
WINDOW = 128
D_FF = 2816
N_EXPERTS = 8
ROPE_BASE = 10000.0
NORM_EPS = 1e-6
D_IN = 3840

F32 = jnp.float32
BF16 = jnp.bfloat16

ROW_TILE = 256
HGRN_CHUNK = 64
HGRN_UNROLL = 4
HGRN_SAFE_DECAY = 80.0
NEG_BIG = -1e30
LOG2_E = 1.4426950408889634
VMEM_LIMIT = 56 * 1024 * 1024


def _cparams(sem):
    return pltpu.CompilerParams(dimension_semantics=sem, vmem_limit_bytes=VMEM_LIMIT)


def _split_bf16(v):
    hi = v.astype(BF16)
    lo = (v - hi.astype(F32)).astype(BF16)
    return hi, lo


def _dot(a, b):
    return jnp.dot(a, b, preferred_element_type=F32)


def _dot_nt(a, b):
    return lax.dot_general(a, b, (((1,), (1,)), ((), ())), preferred_element_type=F32)


def _dot_tn(a, b):
    return lax.dot_general(a, b, (((0,), (0,)), ((), ())), preferred_element_type=F32)


def _group_mean_sq(y, gmat):
    hi, lo = _split_bf16(y * y)
    return _dot(hi, gmat) + _dot(lo, gmat)


def _silu(v):
    return v * (1.0 / (1.0 + jnp.exp(-v)))


def _ada_kernel(s_ref, w_ref, b_ref, o_ref):
    s = s_ref[...]
    s = _silu(s)
    s_hi, s_lo = _split_bf16(s)
    w_hi, w_lo = _split_bf16(w_ref[...])
    o_ref[...] = _dot(s_hi, w_hi) + _dot(s_lo, w_hi) + _dot(s_hi, w_lo) + b_ref[...]


def _ada_modulation(cond, ada_w, ada_b):
    depth = ada_w.shape[0]
    r = cond.shape[0]
    nblk = 6 * D_MODEL // 1024
    return pl.pallas_call(
        _ada_kernel,
        grid=(depth, nblk),
        in_specs=[
            pl.BlockSpec((r, D_MODEL), lambda l, n: (0, 0)),
            pl.BlockSpec((None, D_MODEL, 1024), lambda l, n: (l, 0, n)),
            pl.BlockSpec((None, 1, 1024), lambda l, n: (l, 0, n)),
        ],
        out_specs=pl.BlockSpec((None, r, 1024), lambda l, n: (l, 0, n)),
        out_shape=jax.ShapeDtypeStruct((depth, r, 6 * D_MODEL), F32),
        compiler_params=_cparams(("parallel", "parallel")),
        name="ada_modulation",
    )(cond, ada_w, ada_b.reshape(depth, 1, 6 * D_MODEL))


def _rope(y, cos, sa, sb, half):
    w = y.shape[-1]
    fwd = pltpu.roll(y, w - half, 1)
    bwd = pltpu.roll(y, half, 1)
    return y * cos + fwd * sa + bwd * sb


def _token_row_specs(xs, tiles_per_seq):
    if len(xs) == 1:
        return [pl.BlockSpec((ROW_TILE, D_MODEL), lambda i: (i, 0))]
    tps = tiles_per_seq
    return [pl.BlockSpec((ROW_TILE, D_MODEL), lambda i: ((i // tps) * (tps - 1) + jnp.maximum(i % tps - 1, 0), 0)),
            pl.BlockSpec((ROW_TILE, D_MODEL), lambda i: (i // tps, 0))]


def _token_rows(x_refs, tiles_per_seq):
    if len(x_refs) == 1:
        return x_refs[0][...]
    return jnp.where(pl.program_id(0) % tiles_per_seq == 0, x_refs[1][...], x_refs[0][...])


def _inproj_kernel(*refs, n_x, tiles_per_seq):
    (mod_ref, g_ref, w_ref, rope_ref, gains_ref, lbt_ref, ga_ref, gc_ref,
     qa_ref, ka_ref, va_ref, hq_ref, hv_ref, lff_ref, lfb_ref, hg_ref, qc_ref, kc_ref, vc_ref) = refs[n_x:]
    x = _token_rows(refs[:n_x], tiles_per_seq)
    shift = mod_ref[0:1, :]
    scale = mod_ref[1:2, :]
    ms = jnp.mean(x * x, axis=-1, keepdims=True)
    h = (x * lax.rsqrt(ms + NORM_EPS) * g_ref[...]) * (1.0 + scale) + shift
    hb = h.astype(BF16)

    def proj(a, b):
        return _dot(hb, w_ref[:, a:b])

    def qk_prep(y, gmat, gain, cos, sa, sb, half):
        msq = _group_mean_sq(y, gmat)
        y = y * lax.rsqrt(msq + NORM_EPS) * gain
        return _rope(y, cos, sa, sb, half)

    ga = ga_ref[...]
    gc = gc_ref[...]
    ra = [rope_ref[:, 256 * i:256 * (i + 1)] for i in range(6)]
    qa = qk_prep(proj(0, 256), ga, gains_ref[0:1, :], ra[0], ra[1], ra[2], 8)
    qa_ref[...] = qa.astype(BF16)
    ka = qk_prep(proj(256, 512), ga, gains_ref[1:2, :], ra[0], ra[1], ra[2], 8)
    ka_ref[...] = ka.astype(BF16)
    va_ref[...] = proj(512, 768).astype(BF16)
    hq_ref[...] = _silu(proj(768, 1280)).astype(BF16)
    hv_ref[...] = proj(1280, 1792).astype(BF16)
    for d, lf_ref in enumerate((lff_ref, lfb_ref)):
        z = proj(1792 + 512 * d, 2304 + 512 * d)
        log_lb = lbt_ref[0:1, 512 * d:512 * (d + 1)]
        log1m_lb = lbt_ref[1:2, 512 * d:512 * (d + 1)]
        sp = jnp.maximum(-z, 0.0) + jnp.log(1.0 + jnp.exp(-jnp.abs(z)))
        b2 = log1m_lb - sp
        mx = jnp.maximum(log_lb, b2)
        lf_ref[...] = mx + jnp.log(1.0 + jnp.exp(-jnp.abs(log_lb - b2)))
    hg_ref[...] = _silu(proj(2816, 3328)).astype(BF16)
    qc = qk_prep(proj(3328, 3584), gc, gains_ref[2:3, :], ra[3], ra[4], ra[5], 16)
    qc_ref[...] = qc.astype(BF16)
    kc = qk_prep(proj(3584, 3712), gc[0:128, 0:128], gains_ref[3:4, 0:128],
                 ra[3][:, 0:128], ra[4][:, 0:128], ra[5][:, 0:128], 16)
    kc_ref[...] = kc.astype(BF16)
    vc_ref[...] = proj(3712, 3840).astype(BF16)


def _in_projection(xs, mod, g, w, rope, gains, lbt, ga, gc, tiles_per_seq):
    n = sum(a.shape[0] for a in xs)
    nt = n // ROW_TILE
    tps = tiles_per_seq

    def row(i):
        return (i, 0)

    def mod_idx(i):
        return (i, 0, 0)

    const = lambda i: (0, 0)
    widths = [(256, BF16), (256, BF16), (256, BF16), (512, BF16), (512, BF16),
              (512, F32), (512, F32), (512, BF16), (256, BF16), (128, BF16), (128, BF16)]
    return pl.pallas_call(
        functools.partial(_inproj_kernel, n_x=len(xs), tiles_per_seq=tps),
        grid=(nt,),
        in_specs=_token_row_specs(xs, tps) + [
            pl.BlockSpec((None, 2, D_MODEL), mod_idx),
            pl.BlockSpec((1, D_MODEL), const),
            pl.BlockSpec((D_MODEL, D_IN), const),
            pl.BlockSpec((ROW_TILE, 6 * 256), lambda i: (i % tps, 0)),
            pl.BlockSpec((4, 256), const),
            pl.BlockSpec((2, 1024), const),
            pl.BlockSpec((256, 256), const),
            pl.BlockSpec((256, 256), const),
        ],
        out_specs=[pl.BlockSpec((ROW_TILE, wd), row) for wd, _ in widths],
        out_shape=[jax.ShapeDtypeStruct((n, wd), dt) for wd, dt in widths],
        compiler_params=_cparams(("parallel",)),
        name="in_projection",
    )(*xs, mod, g, w, rope, gains, lbt, ga, gc)


def _diff_attend(q, k, v, lam):
    lane = lax.broadcasted_iota(jnp.int32, q.shape, 1)
    n_maps = 2 * DIFF_HEADS

    def scores(g):
        return _dot_nt(jnp.where(lane // DIFF_HEAD_DIM == g, q, jnp.zeros_like(q)), k)

    out = jnp.zeros(q.shape, F32)
    parts = []
    s_next = scores(0)
    for g in range(n_maps):
        s = s_next
        if g + 1 < n_maps:
            s_next = scores(g + 1)
        e = jnp.exp2(s - jnp.max(s, axis=-1, keepdims=True)).astype(BF16)
        o = _dot(e, v[g // 2])
        parts.append(o * (1.0 / pltpu.roll(o, DIFF_WIDTH - DIFF_V_DIM, 1)))
        if g % 2 == 1:
            out = out + jnp.where(lane // DIFF_V_DIM == g // 2, parts[g - 1] - lam * parts[g], 0.0)
    return out


def _diff_kernel(q_ref, k_ref, v_ref, lam_ref, post_ref, g64_ref, o_ref, vaug_ref, *, n_ctx):
    j = pl.program_id(1)
    lam = lam_ref[0:1, 0:1]

    @pl.when(j == 0)
    def _():
        v = v_ref[...]
        lane = lax.broadcasted_iota(jnp.int32, v.shape, 1)
        for hd in range(DIFF_HEADS):
            vaug_ref[hd] = jnp.where(lane // DIFF_V_DIM == (hd + 1) % DIFF_HEADS, jnp.ones_like(v), v)

    def finish(o):
        msq = _group_mean_sq(o, g64_ref[...])
        o_ref[...] = (o * lax.rsqrt(msq + NORM_EPS) * post_ref[...]).astype(BF16)

    @pl.when(j == 0)
    def _():
        finish(_diff_attend(q_ref[...], k_ref[0:n_ctx, :], [vaug_ref[hd, 0:n_ctx, :] for hd in range(DIFF_HEADS)],
                            lam))

    @pl.when(j > 0)
    def _():
        finish(_diff_attend(q_ref[...], k_ref[...], [vaug_ref[hd] for hd in range(DIFF_HEADS)], lam))


def _diff_attention(qa, ka, va, lam_row, post_row, g64, n_batch, t_len, n_ctx):
    tps = t_len // ROW_TILE
    q3 = qa.reshape(n_batch, t_len, DIFF_WIDTH)
    k3 = ka.reshape(n_batch, t_len, DIFF_WIDTH)
    v3 = va.reshape(n_batch, t_len, DIFF_WIDTH)
    const = lambda b, j: (0, 0)
    out = pl.pallas_call(
        functools.partial(_diff_kernel, n_ctx=n_ctx),
        grid=(n_batch, tps),
        in_specs=[
            pl.BlockSpec((None, ROW_TILE, DIFF_WIDTH), lambda b, j: (b, j, 0)),
            pl.BlockSpec((None, t_len, DIFF_WIDTH), lambda b, j: (b, 0, 0)),
            pl.BlockSpec((None, t_len, DIFF_WIDTH), lambda b, j: (b, 0, 0)),
            pl.BlockSpec((1, 128), const),
            pl.BlockSpec((1, DIFF_WIDTH), const),
            pl.BlockSpec((256, 256), const),
        ],
        out_specs=pl.BlockSpec((None, ROW_TILE, DIFF_WIDTH), lambda b, j: (b, j, 0)),
        out_shape=jax.ShapeDtypeStruct((n_batch, t_len, DIFF_WIDTH), BF16),
        scratch_shapes=[pltpu.VMEM((DIFF_HEADS, t_len, DIFF_WIDTH), BF16)],
        compiler_params=_cparams(("parallel", "arbitrary")),
        name="diff_attention",
    )(q3, k3, v3, lam_row, post_row, g64)
    return out.reshape(n_batch * t_len, DIFF_WIDTH)


def _swa_span_start(j, n_lat):
    q0 = (j - 1) * ROW_TILE
    return jnp.clip(q0 - WINDOW, 0, n_lat - (ROW_TILE + 2 * WINDOW)), q0


def _swa_kernel(q_ref, k_ref, v_ref, sink_ref, bias_ref, o_ref, *, n_ctx, n_lat):
    j = pl.program_id(1)
    q = q_ref[...]
    tq = q.shape[0]
    lane = lax.broadcasted_iota(jnp.int32, (tq, SWA_KV_WIDTH), 1)
    span = ROW_TILE + 2 * WINDOW
    heads = [(g, hd) for g in range(SWA_GROUP) for hd in range(SWA_KV_HEADS)]

    def run(k, v, bias):
        vlane = lax.broadcasted_iota(jnp.int32, v.shape, 1)
        vaug = [jnp.where(vlane // SWA_HEAD_DIM == hd, v, jnp.ones_like(v)) for hd in range(SWA_KV_HEADS)]
        sinks = [sink_ref[SWA_GROUP * hd + g:SWA_GROUP * hd + g + 1, 0:1] for g, hd in heads]
        scores = []
        for g, hd in heads:
            qg = q[:, SWA_KV_WIDTH * g:SWA_KV_WIDTH * (g + 1)]
            s = _dot_nt(jnp.where(lane // SWA_HEAD_DIM == hd, qg, jnp.zeros_like(qg)), k)
            scores.append(s if bias is None else s + bias)
        maxes = [jnp.maximum(jnp.max(s, axis=-1, keepdims=True), sink) for s, sink in zip(scores, sinks)]
        outs = [_dot(jnp.exp2(s - mx).astype(BF16), vaug[hd]) for s, mx, (g, hd) in zip(scores, maxes, heads)]
        for g in range(SWA_GROUP):
            out = jnp.zeros((tq, SWA_KV_WIDTH), F32)
            for i, (gi, hd) in enumerate(heads):
                if gi == g:
                    den = pltpu.roll(outs[i], SWA_HEAD_DIM, 1) + jnp.exp2(sinks[i] - maxes[i])
                    out = out + jnp.where(lane // SWA_HEAD_DIM == hd, outs[i] * (1.0 / den), 0.0)
            o_ref[:, SWA_KV_WIDTH * g:SWA_KV_WIDTH * (g + 1)] = out.astype(BF16)

    @pl.when(j == 0)
    def _():
        run(k_ref[0:n_ctx, :], v_ref[0:n_ctx, :], None)

    @pl.when(j > 0)
    def _():
        ks = pl.multiple_of(_swa_span_start(j, n_lat)[0], WINDOW)
        k = jnp.concatenate([k_ref[0:n_ctx, :], k_ref[pl.ds(n_ctx + ks, span), :]], axis=0)
        v = jnp.concatenate([v_ref[0:n_ctx, :], v_ref[pl.ds(n_ctx + ks, span), :]], axis=0)
        run(k, v, bias_ref[...])


def _swa_attention(qc, kc, vc, sink_tab, n_batch, t_len, n_ctx):
    tps = t_len // ROW_TILE
    n_lat = t_len - n_ctx
    n_keys = n_ctx + ROW_TILE + 2 * WINDOW
    q3 = qc.reshape(n_batch, t_len, SWA_WIDTH)
    k3 = kc.reshape(n_batch, t_len, SWA_KV_WIDTH)
    v3 = vc.reshape(n_batch, t_len, SWA_KV_WIDTH)
    row = jnp.arange(ROW_TILE)[None, :, None]
    key = jnp.arange(n_keys)[None, None, :] - n_ctx
    lead = WINDOW * jnp.arange(3)[:, None, None]
    bias = jnp.where((key < 0) | (jnp.abs(row + lead - key) <= WINDOW), 0.0, NEG_BIG).astype(F32)

    def bias_idx(b, j):
        ks, q0 = _swa_span_start(jnp.maximum(j, 1), n_lat)
        return ((q0 - ks) // WINDOW, 0, 0)

    out = pl.pallas_call(
        functools.partial(_swa_kernel, n_ctx=n_ctx, n_lat=n_lat),
        grid=(n_batch, tps),
        in_specs=[
            pl.BlockSpec((None, ROW_TILE, SWA_WIDTH), lambda b, j: (b, j, 0)),
            pl.BlockSpec((None, t_len, SWA_KV_WIDTH), lambda b, j: (b, 0, 0)),
            pl.BlockSpec((None, t_len, SWA_KV_WIDTH), lambda b, j: (b, 0, 0)),
            pl.BlockSpec((SWA_Q_HEADS, 128), lambda b, j: (0, 0)),
            pl.BlockSpec((None, ROW_TILE, n_keys), bias_idx),
        ],
        out_specs=pl.BlockSpec((None, ROW_TILE, SWA_WIDTH), lambda b, j: (b, j, 0)),
        out_shape=jax.ShapeDtypeStruct((n_batch, t_len, SWA_WIDTH), BF16),
        compiler_params=_cparams(("parallel", "arbitrary")),
        name="swa_attention",
    )(q3, k3, v3, sink_tab, bias)
    return out.reshape(n_batch * t_len, SWA_WIDTH)


def _hgrn_kernel(q_ref, v_ref, lff_ref, lfb_ref, g_ref, ng_ref, o_ref, of_ref, ob_ref, st_ref, cum_ref, qrow_ref,
                 strong_ref, *, n_ctx, t_len):
    c = HGRN_CHUNK
    n_chunks = t_len // c
    n_ctx_chunks = n_ctx // c
    mid = c // 2
    ri = lax.broadcasted_iota(jnp.int32, (c, c), 0)
    ci = lax.broadcasted_iota(jnp.int32, (c, c), 1)
    causal = ri >= ci
    anti = ri <= ci
    tri_f = jnp.where(causal, 1.0, 0.0).astype(BF16)
    tri_b = jnp.where(anti, 1.0, 0.0).astype(BF16)

    st_ref[...] = jnp.zeros_like(st_ref)
    n_iter = n_chunks // HGRN_UNROLL

    def chains_of(i):
        chains = []
        for u in range(HGRN_UNROLL):
            step = i * HGRN_UNROLL + u
            rf = pl.multiple_of(step * c, c)
            cb = jnp.where(step < n_ctx_chunks, n_ctx_chunks - 1 - step, n_chunks - 1 + n_ctx_chunks - step)
            rb = pl.multiple_of(cb * c, c)
            for hd in range(HGRN_HEADS):
                cols = slice(HGRN_K * hd, HGRN_K * (hd + 1))
                chains.append((2 * hd, pl.ds(rf, c), cols, lff_ref, of_ref, tri_f, causal, mid - 1, c - 1))
                chains.append((2 * hd + 1, pl.ds(rb, c), cols, lfb_ref, ob_ref, tri_b, anti, mid, 0))
        return chains

    def max_half_decay(i):
        worst = jnp.zeros((1, HGRN_K), F32)
        for _, rows, cols, lf_ref, _, _, _, _, _ in chains_of(i):
            lf = lf_ref[rows, cols]
            worst = jnp.maximum(worst, jnp.maximum(jnp.abs(jnp.sum(lf[0:mid, :], axis=0, keepdims=True)),
                                                   jnp.abs(jnp.sum(lf[mid:c, :], axis=0, keepdims=True))))
        return jnp.max(worst)

    def exact_step(i):
        for slot, rows, cols, lf_ref, o_ref_d, tri, mask, _, end_row in chains_of(i):
            lf = lf_ref[rows, cols]
            lf_hi, lf_lo = _split_bf16(lf)
            cum = _dot(tri, lf_hi) + _dot(tri, lf_lo)
            tot = cum[end_row:end_row + 1, :]
            v = v_ref[rows, cols]
            k = 1.0 - jnp.exp(lf)
            q = q_ref[rows, cols].astype(F32)
            cum_ref[...] = cum
            qrow_ref[...] = q

            def row(t, sct):
                w = jnp.exp(jnp.minimum(cum_ref[pl.ds(t, 1), :] - cum, 0.0))
                col = jnp.sum(qrow_ref[pl.ds(t, 1), :] * k * w, axis=-1, keepdims=True)
                return jnp.where(ci == t, col, sct)

            sct = lax.fori_loop(0, c, row, jnp.zeros((c, c), F32))
            valid_t = anti if mask is causal else causal
            st = st_ref[slot]
            o = (_dot_tn(jnp.where(valid_t, sct, 0.0).astype(BF16), v)
                 + _dot_nt((q * jnp.exp(cum)).astype(BF16), st.astype(BF16)))
            o_ref_d[rows, cols] = o.astype(BF16)
            st_ref[slot] = st * jnp.exp(tot) + _dot_tn(v, (k * jnp.exp(tot - cum)).astype(BF16))

    def fast_step(i):
        chains = chains_of(i)
        cums = []
        for _, rows, cols, lf_ref, _, tri, _, _, _ in chains:
            lf_hi, lf_lo = _split_bf16(lf_ref[rows, cols])
            cums.append(_dot(tri, lf_hi) + _dot(tri, lf_lo))
        prods = []
        for (slot, rows, cols, lf_ref, _, _, mask, ref_row, end_row), cum in zip(chains, cums):
            ref = cum[ref_row:ref_row + 1, :]
            tot = cum[end_row:end_row + 1, :]
            v = v_ref[rows, cols]
            k = 1.0 - jnp.exp(lf_ref[rows, cols])
            qt = q_ref[rows, cols].astype(F32) * jnp.exp(cum - ref)
            kt = k * jnp.exp(ref - cum)
            sc = _dot_nt(qt.astype(BF16), kt.astype(BF16))
            upd = _dot_tn(v, (kt * jnp.exp(tot - ref)).astype(BF16))
            prods.append((sc, (qt * jnp.exp(ref)).astype(BF16), jnp.exp(tot), upd, v))
        states = {}
        for (slot, rows, cols, _, o_ref_d, _, mask, _, _), (sc, q_in, decay, upd, v) in zip(chains, prods):
            st = states[slot] if slot in states else st_ref[slot]
            o = _dot(jnp.where(mask, sc, 0.0).astype(BF16), v) + _dot_nt(q_in, st.astype(BF16))
            o_ref_d[rows, cols] = o.astype(BF16)
            states[slot] = st * decay + upd
        for slot, st in states.items():
            st_ref[slot] = st

    def test_decay(i):
        strong_ref[0] = (max_half_decay(i) > HGRN_SAFE_DECAY).astype(jnp.int32)

    def body(i, _):
        strong = strong_ref[0]

        @pl.when(strong == 0)
        def _():
            test_decay(jnp.minimum(i + 1, n_iter - 1))
            fast_step(i)

        @pl.when(strong != 0)
        def _():
            test_decay(jnp.minimum(i + 1, n_iter - 1))
            exact_step(i)

        return 0

    test_decay(0)
    lax.fori_loop(0, n_iter, body, 0)

    def fin(t, _):
        r = pl.multiple_of(t * ROW_TILE, ROW_TILE)
        for hd in range(HGRN_HEADS):
            cols = slice(HGRN_K * hd, HGRN_K * (hd + 1))
            o = of_ref[pl.ds(r, ROW_TILE), cols].astype(F32) + ob_ref[pl.ds(r, ROW_TILE), cols].astype(F32)
            ms = jnp.mean(o * o, axis=-1, keepdims=True)
            o = o * lax.rsqrt(ms + NORM_EPS) * ng_ref[...]
            o_ref[pl.ds(r, ROW_TILE), cols] = (o * g_ref[pl.ds(r, ROW_TILE), cols].astype(F32)).astype(BF16)
        return 0

    lax.fori_loop(0, t_len // ROW_TILE, fin, 0)


def _hgrn(hq, hv, lff, lfb, hg, norm_g, n_batch, t_len, n_ctx):
    def view(a):
        return a.reshape(n_batch, t_len, HGRN_WIDTH)

    blk = pl.BlockSpec((None, t_len, HGRN_WIDTH), lambda b: (b, 0, 0))
    out = pl.pallas_call(
        functools.partial(_hgrn_kernel, n_ctx=n_ctx, t_len=t_len),
        grid=(n_batch,),
        in_specs=[blk] * 5 + [pl.BlockSpec((1, HGRN_K), lambda b: (0, 0))],
        out_specs=blk,
        out_shape=jax.ShapeDtypeStruct((n_batch, t_len, HGRN_WIDTH), BF16),
        scratch_shapes=[pltpu.VMEM((t_len, HGRN_WIDTH), BF16), pltpu.VMEM((t_len, HGRN_WIDTH), BF16),
                        pltpu.VMEM((2 * HGRN_HEADS, HGRN_K, HGRN_K), F32),
                        pltpu.VMEM((HGRN_CHUNK, HGRN_K), F32), pltpu.VMEM((HGRN_CHUNK, HGRN_K), F32),
                        pltpu.SMEM((1,), jnp.int32)],
        compiler_params=_cparams(("parallel",)),
        name="hgrn2",
    )(view(hq), view(hv), view(lff), view(lfb), view(hg), norm_g)
    return out.reshape(n_batch * t_len, HGRN_WIDTH)


def _mix_residual_norm(x, a_ref, b_ref, c_ref, w_ref, mod_ref, g_ref):
    y = (_dot(a_ref[...], w_ref[0:256, :]) + _dot(b_ref[...], w_ref[256:768, :])
         + _dot(c_ref[...], w_ref[768:1024, :]))
    x = x + mod_ref[0:1, :] * y
    ms = jnp.mean(x * x, axis=-1, keepdims=True)
    h = (x * lax.rsqrt(ms + NORM_EPS) * g_ref[...]) * (1.0 + mod_ref[2:3, :]) + mod_ref[1:2, :]
    return x, h


def _outproj_router_kernel(x_ref, a_ref, b_ref, c_ref, w_ref, mod_ref, g_ref, r_ref, xo_ref, h_ref, gate_ref):
    x, h = _mix_residual_norm(x_ref[...], a_ref, b_ref, c_ref, w_ref, mod_ref, g_ref)
    xo_ref[...] = x
    h_ref[...] = _pack_bf16_pairs(h)
    h_hi, h_lo = _split_bf16(h)
    r_hi, r_lo = _split_bf16(r_ref[...])
    logits = _dot(h_hi, r_hi) + _dot(h_lo, r_hi) + _dot(h_hi, r_lo)
    lane = lax.broadcasted_iota(jnp.int32, logits.shape, 1).astype(F32)
    logits = jnp.where(lane < N_EXPERTS, logits, NEG_BIG)
    v1 = jnp.max(logits, axis=-1, keepdims=True)
    i1 = jnp.min(jnp.where(logits == v1, lane, 128.0), axis=-1, keepdims=True)
    rest_l = jnp.where(lane == i1, NEG_BIG, logits)
    v2 = jnp.max(rest_l, axis=-1, keepdims=True)
    i2 = jnp.min(jnp.where(rest_l == v2, lane, 128.0), axis=-1, keepdims=True)
    e2 = jnp.exp(v2 - v1)
    w1 = 1.0 / (1.0 + e2)
    w2 = e2 * w1
    gate_ref[...] = (jnp.where(lane == i1, w1, 0.0) + jnp.where(lane == i2, w2, 0.0)
                     + jnp.where(lane == 8.0, i1, 0.0) + jnp.where(lane == 9.0, i2, 0.0)
                     + jnp.where(lane == 10.0, w1, 0.0) + jnp.where(lane == 11.0, w2, 0.0))


def _out_projection_router(xc, a, b, c, w, mod, g, router):
    n = xc.shape[0]
    nt = n // ROW_TILE
    row = lambda i: (i, 0)
    const = lambda i: (0, 0)
    return pl.pallas_call(
        _outproj_router_kernel,
        grid=(nt,),
        in_specs=[
            pl.BlockSpec((ROW_TILE, D_MODEL), row),
            pl.BlockSpec((ROW_TILE, DIFF_WIDTH), row),
            pl.BlockSpec((ROW_TILE, HGRN_WIDTH), row),
            pl.BlockSpec((ROW_TILE, SWA_WIDTH), row),
            pl.BlockSpec((D_MODEL, D_MODEL), const),
            pl.BlockSpec((None, 3, D_MODEL), lambda i: (i, 0, 0)),
            pl.BlockSpec((1, D_MODEL), const),
            pl.BlockSpec((D_MODEL, 128), const),
        ],
        out_specs=[pl.BlockSpec((ROW_TILE, D_MODEL), row), pl.BlockSpec((ROW_TILE, D_MODEL // 2), row),
                   pl.BlockSpec((ROW_TILE, 128), row)],
        out_shape=[jax.ShapeDtypeStruct((n, D_MODEL), F32), jax.ShapeDtypeStruct((n, D_MODEL // 2), jnp.int32),
                   jax.ShapeDtypeStruct((n, 128), F32)],
        compiler_params=_cparams(("parallel",)),
        name="out_projection",
    )(xc, a, b, c, w, mod, g, router)


def _outproj_ffn_kernel(*refs, n_x, tiles_per_seq):
    a_ref, b_ref, c_ref, wo_ref, mod_ref, g_ref, w1_ref, w3_ref, w2_ref, o_ref = refs[n_x:]
    x, h = _mix_residual_norm(_token_rows(refs[:n_x], tiles_per_seq), a_ref, b_ref, c_ref, wo_ref, mod_ref, g_ref)
    h = h.astype(BF16)
    u = _dot(h, w1_ref[...])
    act = (_silu(u) * _dot(h, w3_ref[...])).astype(BF16)
    o_ref[...] = x + mod_ref[3:4, :] * _dot(act, w2_ref[...])


def _outproj_dense_ffn(xs, a, b, c, wo, mod, g, w1, w3, w2, tiles_per_seq):
    n = a.shape[0]
    nt = n // ROW_TILE
    row = lambda i: (i, 0)
    const = lambda i: (0, 0)
    return pl.pallas_call(
        functools.partial(_outproj_ffn_kernel, n_x=len(xs), tiles_per_seq=tiles_per_seq),
        grid=(nt,),
        in_specs=_token_row_specs(xs, tiles_per_seq) + [
            pl.BlockSpec((ROW_TILE, DIFF_WIDTH), row),
            pl.BlockSpec((ROW_TILE, HGRN_WIDTH), row),
            pl.BlockSpec((ROW_TILE, SWA_WIDTH), row),
            pl.BlockSpec((D_MODEL, D_MODEL), const),
            pl.BlockSpec((None, 4, D_MODEL), lambda i: (i, 0, 0)),
            pl.BlockSpec((1, D_MODEL), const),
            pl.BlockSpec((D_MODEL, D_FF), const),
            pl.BlockSpec((D_MODEL, D_FF), const),
            pl.BlockSpec((D_FF, D_MODEL), const),
        ],
        out_specs=pl.BlockSpec((ROW_TILE, D_MODEL), row),
        out_shape=jax.ShapeDtypeStruct((n, D_MODEL), F32),
        compiler_params=_cparams(("parallel",)),
        name="outproj_dense_ffn",
    )(*xs, a, b, c, wo, mod, g, w1, w3, w2)


MOE_TM = 256
SC_CORES = 2
SC_SUBCORES = 16
SC_GATHER_ROWS = 128
SC_SPILL_ROWS = 4096


def _pack_bf16_pairs(h):
    half = h.shape[1] // 2
    bits = pltpu.bitcast(h.astype(BF16).astype(F32), jnp.uint32)
    packed = (bits[:, :half] >> 16) | (bits[:, half:] & jnp.uint32(0xFFFF0000))
    return pltpu.bitcast(packed, jnp.int32)


def _unpack_bf16_pairs(u):
    bits = pltpu.bitcast(u, jnp.uint32)
    lo = pltpu.bitcast(bits << 16, F32)
    hi = pltpu.bitcast(bits & jnp.uint32(0xFFFF0000), F32)
    return jnp.concatenate([lo, hi], axis=1)


def _sc_gather_rows(table, idx):
    n_workers = SC_CORES * SC_SUBCORES
    n_rows, width = idx.shape[0], table.shape[1]
    assert n_rows % (n_workers * SC_GATHER_ROWS) == 0
    per_worker = n_rows // n_workers
    mesh = plsc.VectorSubcoreMesh(core_axis_name="c", subcore_axis_name="s",
                                  num_cores=SC_CORES, num_subcores=SC_SUBCORES)

    @functools.partial(
        pl.kernel, mesh=mesh,
        out_type=jax.ShapeDtypeStruct((n_rows, width), table.dtype),
        scratch_types=[pltpu.VMEM((SC_GATHER_ROWS,), jnp.int32),
                       pltpu.VMEM((SC_GATHER_ROWS, width), table.dtype),
                       pltpu.SemaphoreType.DMA],
        name="sc_gather_rows",
    )
    def gather(table_hbm, idx_hbm, out_hbm, idx_v, rows_v, sem):
        worker = lax.axis_index("s") * SC_CORES + lax.axis_index("c")
        base = worker * per_worker

        @pl.loop(0, per_worker // SC_GATHER_ROWS)
        def _(step):
            off = pl.multiple_of(base + step * SC_GATHER_ROWS, SC_GATHER_ROWS)
            pltpu.sync_copy(idx_hbm.at[pl.ds(off, SC_GATHER_ROWS)], idx_v)
            pltpu.async_copy(table_hbm.at[idx_v], rows_v, sem).wait()
            pltpu.sync_copy(rows_v, out_hbm.at[pl.ds(off, SC_GATHER_ROWS)])

    return gather(table, idx)


def _sc_scatter_rows(rows, idx, n_out):
    n_workers = SC_CORES * SC_SUBCORES
    n_rows, width = rows.shape
    assert n_rows % (n_workers * SC_GATHER_ROWS) == 0
    chunks_per_worker = n_rows // (n_workers * SC_GATHER_ROWS)
    idx = idx.reshape(2, n_rows // SC_GATHER_ROWS, SC_GATHER_ROWS)
    mesh = plsc.VectorSubcoreMesh(core_axis_name="c", subcore_axis_name="s",
                                  num_cores=SC_CORES, num_subcores=SC_SUBCORES)

    @functools.partial(
        pl.kernel, mesh=mesh,
        out_type=jax.ShapeDtypeStruct((n_out, width), rows.dtype),
        scratch_types=[pltpu.VMEM((SC_GATHER_ROWS,), jnp.int32), pltpu.VMEM((SC_GATHER_ROWS,), jnp.int32),
                       pltpu.VMEM((SC_GATHER_ROWS, width), rows.dtype),
                       pltpu.SemaphoreType.DMA, pltpu.SemaphoreType.DMA],
        name="sc_scatter_rows",
    )
    def scatter(rows_hbm, idx_hbm, out_hbm, idx0_v, idx1_v, rows_v, sem0, sem1):
        worker = lax.axis_index("s") * SC_CORES + lax.axis_index("c")

        @pl.loop(0, chunks_per_worker)
        def _(step):
            chunk = worker * chunks_per_worker + step
            off = pl.multiple_of(chunk * SC_GATHER_ROWS, SC_GATHER_ROWS)
            pltpu.sync_copy(rows_hbm.at[pl.ds(off, SC_GATHER_ROWS)], rows_v)
            pltpu.sync_copy(idx_hbm.at[0, chunk], idx0_v)
            first = pltpu.async_copy(rows_v, out_hbm.at[idx0_v], sem0)
            pltpu.sync_copy(idx_hbm.at[1, chunk], idx1_v)
            second = pltpu.async_copy(rows_v, out_hbm.at[idx1_v], sem1)
            first.wait()
            second.wait()

    return scatter(rows, idx)


def _moe_expert_kernel(te_ref, tv_ref, xs_ref, w1_ref, w3_ref, w2_ref, ys_ref):
    r = pl.program_id(0)

    @pl.when(tv_ref[r] == 1)
    def _():
        h = _unpack_bf16_pairs(xs_ref[...]).astype(BF16)
        u = _dot(h, w1_ref[...])
        act = (_silu(u) * _dot(h, w3_ref[...])).astype(BF16)
        ys_ref[...] = _pack_bf16_pairs(_dot(act, w2_ref[...]))

    @pl.when(tv_ref[r] == 0)
    def _():
        ys_ref[...] = jnp.zeros_like(ys_ref)


def _moe_combine_kernel(x_ref, y0_ref, y1_ref, route_ref, mod_ref, o_ref):
    g0, g1 = route_ref[:, 10:11], route_ref[:, 11:12]
    y = (jnp.where(g0 != 0.0, g0 * _unpack_bf16_pairs(y0_ref[...]), 0.0)
         + jnp.where(g1 != 0.0, g1 * _unpack_bf16_pairs(y1_ref[...]), 0.0))
    o_ref[...] = x_ref[...] + mod_ref[...] * y


def _moe_plan(route, n_rt):
    n = route.shape[0]
    tm = MOE_TM
    gates_t = route[:, :N_EXPERTS].T
    sel = gates_t != 0.0
    si = sel.astype(jnp.int32)
    rank = jnp.cumsum(si, axis=1) - si
    ntile = (jnp.sum(si, axis=1) + tm - 1) // tm
    tile_end = jnp.cumsum(ntile)
    tile_off = tile_end - ntile
    used = tile_end[-1]
    n_rows = n_rt * tm
    dest = jnp.where(sel, tile_off[:, None] * tm + rank, n_rows)
    expert = jnp.arange(N_EXPERTS, dtype=jnp.int32)[:, None]

    def row_of(lane):
        pick = route[:, lane].astype(jnp.int32)[None, :]
        return jnp.sum(jnp.where(expert == pick, dest, 0), axis=0).astype(jnp.int32)

    pos = jnp.stack([row_of(8), row_of(9)])
    r = jnp.arange(n_rt, dtype=jnp.int32)
    rc = jnp.minimum(r, used - 1)
    tile_expert = jnp.sum((tile_end[None, :] <= rc[:, None]).astype(jnp.int32), axis=1)
    tile_valid = (r < used).astype(jnp.int32)
    return pos, tile_expert, tile_valid


def _moe_ffn(x, h_packed, route, w1, w3, w2, layer_idx, mod, latent_tiles_per_seq=None):
    n = x.shape[0]
    nt = n // ROW_TILE
    tm = MOE_TM
    half = D_MODEL // 2
    n_rt = 2 * n // tm + N_EXPERTS
    pos, tile_expert, tile_valid = _moe_plan(route, n_rt)
    routed = pos < n_rt * tm
    token = jnp.arange(n, dtype=jnp.int32)[None, :]

    xs = _sc_scatter_rows(h_packed, jnp.where(routed, pos, n_rt * tm + token % SC_SPILL_ROWS),
                          n_rt * tm + SC_SPILL_ROWS)
    expert_w = lambda r, te, tv: (layer_idx, te[r], 0, 0)
    ys = pl.pallas_call(
        _moe_expert_kernel,
        grid_spec=pltpu.PrefetchScalarGridSpec(
            num_scalar_prefetch=2,
            grid=(n_rt,),
            in_specs=[
                pl.BlockSpec((tm, half), lambda r, te, tv: (r, 0)),
                pl.BlockSpec((None, None, D_MODEL, D_FF), expert_w),
                pl.BlockSpec((None, None, D_MODEL, D_FF), expert_w),
                pl.BlockSpec((None, None, D_FF, D_MODEL), expert_w),
            ],
            out_specs=pl.BlockSpec((tm, half), lambda r, te, tv: (r, 0)),
        ),
        out_shape=jax.ShapeDtypeStruct((n_rt * tm, half), jnp.int32),
        compiler_params=_cparams(("arbitrary",)),
        name="moe_experts",
    )(tile_expert, tile_valid, xs, w1, w3, w2)

    y = _sc_gather_rows(ys, jnp.where(routed, pos, token).reshape(-1))
    row = lambda i: (i, 0)
    if latent_tiles_per_seq is None:
        out_rows, out_row = n, row
    else:
        lt = latent_tiles_per_seq
        out_rows = n // (lt + 1) * lt
        out_row = lambda i: ((i // (lt + 1)) * lt + jnp.maximum(i % (lt + 1) - 1, 0), 0)
    return pl.pallas_call(
        _moe_combine_kernel,
        grid=(nt,),
        in_specs=[
            pl.BlockSpec((ROW_TILE, D_MODEL), row),
            pl.BlockSpec((ROW_TILE, half), row),
            pl.BlockSpec((ROW_TILE, half), lambda i: (nt + i, 0)),
            pl.BlockSpec((ROW_TILE, 128), row),
            pl.BlockSpec((None, 1, D_MODEL), lambda i: (i, 0, 0)),
        ],
        out_specs=pl.BlockSpec((ROW_TILE, D_MODEL), out_row),
        out_shape=jax.ShapeDtypeStruct((out_rows, D_MODEL), F32),
        compiler_params=_cparams(("arbitrary",)),
        name="moe_combine",
    )(x, y, y, route, mod)


def _rope_tables(n_ctx, n_lat):
    pos_r = jnp.arange(n_lat, dtype=jnp.int32) // GRID_W
    pos_c = jnp.arange(n_lat, dtype=jnp.int32) % GRID_W

    def per_head(head_dim):
        nf = head_dim // 4
        inv = ROPE_BASE ** (-jnp.arange(nf, dtype=F32) / nf)
        ang_r = pos_r.astype(F32)[:, None] * inv[None, :]
        ang_c = pos_c.astype(F32)[:, None] * inv[None, :]
        z = jnp.zeros_like(ang_r)
        cos = jnp.concatenate([jnp.cos(ang_r)] * 2 + [jnp.cos(ang_c)] * 2, axis=-1)
        sa = jnp.concatenate([-jnp.sin(ang_r), z, -jnp.sin(ang_c), z], axis=-1)
        sb = jnp.concatenate([z, jnp.sin(ang_r), z, jnp.sin(ang_c)], axis=-1)
        reps = 256 // head_dim
        tabs = [jnp.tile(t, (1, reps)) for t in (cos, sa, sb)]
        ctx = [jnp.ones((n_ctx, 256), F32), jnp.zeros((n_ctx, 256), F32), jnp.zeros((n_ctx, 256), F32)]
        return [jnp.concatenate([c, t], axis=0) for c, t in zip(ctx, tabs)]

    return jnp.concatenate(per_head(DIFF_HEAD_DIM) + per_head(SWA_HEAD_DIM), axis=-1)


def _block_diag_mean(group):
    idx = jnp.arange(256) // group
    return jnp.where(idx[:, None] == idx[None, :], 1.0 / group, 0.0).astype(BF16)


def kernel(x, c, ctx, c_ctx, ada_w, ada_b, norm_mix_g, norm_ffn_g, w_in, w_out, diff_qk_norm_g, diff_lambda,
           diff_subln_g, hgrn_lb_logits, hgrn_norm_g, swa_qk_norm_g, swa_sink, ffn_w1, ffn_w3, ffn_w2,
           moe_router, moe_w1, moe_w3, moe_w2):
    n_batch, n_lat, _ = x.shape
    n_ctx = ctx.shape[1]
    depth = ada_w.shape[0]
    t_len = n_ctx + n_lat
    tps = t_len // ROW_TILE
    assert n_ctx == ROW_TILE and n_lat % ROW_TILE == 0 and n_lat >= ROW_TILE + 2 * WINDOW

    xs = (x.reshape(n_batch * n_lat, D_MODEL), ctx.reshape(n_batch * n_ctx, D_MODEL))

    n_rows = -(-(n_batch + 1) // 8) * 8
    cond = jnp.concatenate([c, c_ctx[None, :], jnp.zeros((n_rows - n_batch - 1, D_MODEL), F32)], axis=0)
    mods = _ada_modulation(cond, ada_w, ada_b).reshape(depth, n_rows, 6, D_MODEL)
    m_lat = jnp.broadcast_to(mods[:, :n_batch, None], (depth, n_batch, tps - 1, 6, D_MODEL))
    m_ctx = jnp.broadcast_to(mods[:, n_batch, None, None], (depth, n_batch, 1, 6, D_MODEL))
    mods = jnp.concatenate([m_ctx, m_lat], axis=2).reshape(depth, n_batch * tps, 6, D_MODEL)

    lb = jnp.cumsum(jax.nn.softmax(hgrn_lb_logits.astype(F32), axis=1), axis=1)
    lb = lb - lb[:, :1]
    rope = _rope_tables(n_ctx, n_lat)
    g32 = _block_diag_mean(DIFF_HEAD_DIM)
    g64 = _block_diag_mean(SWA_HEAD_DIM)

    perm_q = jnp.arange(SWA_WIDTH).reshape(SWA_KV_HEADS, SWA_GROUP, SWA_HEAD_DIM).transpose(1, 0, 2).reshape(-1)
    qc0 = 3 * DIFF_WIDTH + 5 * HGRN_WIDTH
    col_perm = jnp.concatenate([jnp.arange(qc0), qc0 + perm_q, jnp.arange(qc0 + SWA_WIDTH, D_IN)])
    oc0 = DIFF_WIDTH + HGRN_WIDTH
    row_perm = jnp.concatenate([jnp.arange(oc0), oc0 + perm_q])
    moe_w1_b, moe_w3_b, moe_w2_b = moe_w1.astype(BF16), moe_w3.astype(BF16), moe_w2.astype(BF16)

    for layer in range(depth):
        lam_init = 0.8 - 0.6 * math.exp(-0.3 * layer)
        mod = mods[layer]
        w_in_l = w_in[layer][:, col_perm].astype(BF16)
        w_out_l = w_out[layer][row_perm, :].astype(BF16)
        gains = jnp.stack([
            jnp.tile(diff_qk_norm_g[layer, 0], 8) * (DIFF_HEAD_DIM ** -0.5 * LOG2_E),
            jnp.tile(diff_qk_norm_g[layer, 1], 8),
            jnp.tile(swa_qk_norm_g[layer, 0], 4) * (SWA_HEAD_DIM ** -0.5 * LOG2_E),
            jnp.tile(swa_qk_norm_g[layer, 1], 4),
        ]).astype(F32)
        lbt = jnp.stack([
            jnp.log(lb[:, layer]).reshape(-1),
            jnp.log1p(-lb[:, layer]).reshape(-1),
        ]).astype(F32)
        lv = diff_lambda[layer].astype(F32)
        lam = jnp.exp(jnp.sum(lv[0] * lv[1])) - jnp.exp(jnp.sum(lv[2] * lv[3])) + lam_init
        lam_row = jnp.full((1, 128), lam, F32)
        post_row = (jnp.tile(diff_subln_g[layer], DIFF_HEADS) * (1.0 - lam_init)).reshape(1, DIFF_WIDTH).astype(F32)
        sink_tab = jnp.broadcast_to(swa_sink[layer].astype(F32)[:, None] * LOG2_E, (SWA_Q_HEADS, 128))

        (qa, ka, va, hq, hv, lff, lfb, hg, qc, kc, vc) = _in_projection(
            xs, mod[:, 0:2], norm_mix_g[layer].reshape(1, D_MODEL), w_in_l, rope, gains, lbt, g32, g64, tps)
        a = _diff_attention(qa, ka, va, lam_row, post_row, g64, n_batch, t_len, n_ctx)
        b = _hgrn(hq, hv, lff, lfb, hg, hgrn_norm_g[layer].reshape(1, HGRN_K), n_batch, t_len, n_ctx)
        cc = _swa_attention(qc, kc, vc, sink_tab, n_batch, t_len, n_ctx)

        jj = layer // 2
        g2 = norm_ffn_g[layer].reshape(1, D_MODEL)
        if layer % 2 == 1:
            router = jnp.pad(moe_router[jj].astype(F32), ((0, 0), (0, 128 - N_EXPERTS)))
            x_mid, h2, route = _out_projection_router(xs[0], a, b, cc, w_out_l, mod[:, 2:5], g2, router)
            if layer == depth - 1:
                is_lat = (jnp.arange(n_batch * t_len, dtype=jnp.int32) % t_len) >= n_ctx
                route = jnp.where(is_lat[:, None], route, 0.0)
            xc = _moe_ffn(x_mid, h2, route, moe_w1_b, moe_w3_b, moe_w2_b, jj, mod[:, 5:6],
                          latent_tiles_per_seq=tps - 1 if layer == depth - 1 else None)
        else:
            xc = _outproj_dense_ffn(xs, a, b, cc, w_out_l, mod[:, 2:6], g2, ffn_w1[jj].astype(BF16),
                                    ffn_w3[jj].astype(BF16), ffn_w2[jj].astype(BF16), tps)
        xs = (xc,)

    if depth % 2 == 0:
        return xc.reshape(n_batch, n_lat, D_MODEL)
    return xc.reshape(n_batch, t_len, D_MODEL)[:, n_ctx:, :]
```

```python
import functools
import math

import jax
import jax.numpy as jnp
from jax import lax
from jax.experimental import pallas as pl
from jax.experimental.pallas import tpu as pltpu
from jax.experimental.pallas import tpu_sc as plsc

D_MODEL = 1024
GRID_W = 64
DIFF_HEADS = 4
DIFF_HEAD_DIM = 32
DIFF_V_DIM = 64
DIFF_WIDTH = 256
HGRN_HEADS = 4
HGRN_K = 128
HGRN_WIDTH = 512
SWA_Q_HEADS = 4
SWA_KV_HEADS = 2
SWA_GROUP = 2
SWA_HEAD_DIM = 64
SWA_WIDTH = 256
SWA_KV_WIDTH = 128
WINDOW = 128
D_FF = 2816
N_EXPERTS = 8
ROPE_BASE = 10000.0
NORM_EPS = 1e-6
D_IN = 3840

F32 = jnp.float32
BF16 = jnp.bfloat16

ROW_TILE = 256
HGRN_CHUNK = 64
HGRN_UNROLL = 4
HGRN_SAFE_DECAY = 80.0
NEG_BIG = -1e30
LOG2_E = 1.4426950408889634
VMEM_LIMIT = 56 * 1024 * 1024


def _cparams(sem):
    return pltpu.CompilerParams(dimension_semantics=sem, vmem_limit_bytes=VMEM_LIMIT)


def _split_bf16(v):
    hi = v.astype(BF16)
    lo = (v - hi.astype(F32)).astype(BF16)
    return hi, lo


def _dot(a, b):
    return jnp.dot(a, b, preferred_element_type=F32)


def _dot_nt(a, b):
    return lax.dot_general(a, b, (((1,), (1,)), ((), ())), preferred_element_type=F32)


def _dot_tn(a, b):
    return lax.dot_general(a, b, (((0,), (0,)), ((), ())), preferred_element_type=F32)


def _group_mean_sq(y, gmat):
    hi, lo = _split_bf16(y * y)
    return _dot(hi, gmat) + _dot(lo, gmat)


def _silu(v):
    return v * (1.0 / (1.0 + jnp.exp(-v)))


def _ada_kernel(s_ref, w_ref, b_ref, o_ref):
    s = s_ref[...]
    s = _silu(s)
    s_hi, s_lo = _split_bf16(s)
    w_hi, w_lo = _split_bf16(w_ref[...])
    o_ref[...] = _dot(s_hi, w_hi) + _dot(s_lo, w_hi) + _dot(s_hi, w_lo) + b_ref[...]


def _ada_modulation(cond, ada_w, ada_b):
    depth = ada_w.shape[0]
    r = cond.shape[0]
    nblk = 6 * D_MODEL // 1024
    return pl.pallas_call(
        _ada_kernel,
        grid=(depth, nblk),
        in_specs=[
            pl.BlockSpec((r, D_MODEL), lambda l, n: (0, 0)),
            pl.BlockSpec((None, D_MODEL, 1024), lambda l, n: (l, 0, n)),
            pl.BlockSpec((None, 1, 1024), lambda l, n: (l, 0, n)),
        ],
        out_specs=pl.BlockSpec((None, r, 1024), lambda l, n: (l, 0, n)),
        out_shape=jax.ShapeDtypeStruct((depth, r, 6 * D_MODEL), F32),
        compiler_params=_cparams(("parallel", "parallel")),
        name="ada_modulation",
    )(cond, ada_w, ada_b.reshape(depth, 1, 6 * D_MODEL))


def _rope(y, cos, sa, sb, half):
    w = y.shape[-1]
    fwd = pltpu.roll(y, w - half, 1)
    bwd = pltpu.roll(y, half, 1)
    return y * cos + fwd * sa + bwd * sb


def _token_row_specs(xs, tiles_per_seq):
    if len(xs) == 1:
        return [pl.BlockSpec((ROW_TILE, D_MODEL), lambda i: (i, 0))]
    tps = tiles_per_seq
    return [pl.BlockSpec((ROW_TILE, D_MODEL), lambda i: ((i // tps) * (tps - 1) + jnp.maximum(i % tps - 1, 0), 0)),
            pl.BlockSpec((ROW_TILE, D_MODEL), lambda i: (i // tps, 0))]


def _token_rows(x_refs, tiles_per_seq):
    if len(x_refs) == 1:
        return x_refs[0][...]
    return jnp.where(pl.program_id(0) % tiles_per_seq == 0, x_refs[1][...], x_refs[0][...])


def _inproj_kernel(*refs, n_x, tiles_per_seq):
    (mod_ref, g_ref, w_ref, rope_ref, gains_ref, lbt_ref, ga_ref, gc_ref,
     qa_ref, ka_ref, va_ref, hq_ref, hv_ref, lff_ref, lfb_ref, hg_ref, qc_ref, kc_ref, vc_ref) = refs[n_x:]
    x = _token_rows(refs[:n_x], tiles_per_seq)
    shift = mod_ref[0:1, :]
    scale = mod_ref[1:2, :]
    ms = jnp.mean(x * x, axis=-1, keepdims=True)
    h = (x * lax.rsqrt(ms + NORM_EPS) * g_ref[...]) * (1.0 + scale) + shift
    hb = h.astype(BF16)

    def proj(a, b):
        return _dot(hb, w_ref[:, a:b])

    def qk_prep(y, msq, gain, cos, sa, sb, half):
        y = y * lax.rsqrt(msq + NORM_EPS) * gain
        return _rope(y, cos, sa, sb, half)

    ga = ga_ref[...]
    gc = gc_ref[...]
    ra = [rope_ref[:, 256 * i:256 * (i + 1)] for i in range(6)]
    y_qa, y_ka, y_qc, y_kc = proj(0, 256), proj(256, 512), proj(3328, 3584), proj(3584, 3712)
    va_ref[...] = proj(512, 768).astype(BF16)
    hq_ref[...] = _silu(proj(768, 1280)).astype(BF16)
    m_qa, m_ka = _group_mean_sq(y_qa, ga), _group_mean_sq(y_ka, ga)
    m_qc, m_kc = _group_mean_sq(y_qc, gc), _group_mean_sq(y_kc, gc[0:128, 0:128])
    hv_ref[...] = proj(1280, 1792).astype(BF16)
    for d, lf_ref in enumerate((lff_ref, lfb_ref)):
        z = proj(1792 + 512 * d, 2304 + 512 * d)
        log_lb = lbt_ref[0:1, 512 * d:512 * (d + 1)]
        log1m_lb = lbt_ref[1:2, 512 * d:512 * (d + 1)]
        sp = jnp.maximum(-z, 0.0) + jnp.log(1.0 + jnp.exp(-jnp.abs(z)))
        b2 = log1m_lb - sp
        mx = jnp.maximum(log_lb, b2)
        lf_ref[...] = mx + jnp.log(1.0 + jnp.exp(-jnp.abs(log_lb - b2)))
    hg_ref[...] = _silu(proj(2816, 3328)).astype(BF16)
    vc_ref[...] = proj(3712, 3840).astype(BF16)
    qa_ref[...] = qk_prep(y_qa, m_qa, gains_ref[0:1, :], ra[0], ra[1], ra[2], 8).astype(BF16)
    ka_ref[...] = qk_prep(y_ka, m_ka, gains_ref[1:2, :], ra[0], ra[1], ra[2], 8).astype(BF16)
    qc_ref[...] = qk_prep(y_qc, m_qc, gains_ref[2:3, :], ra[3], ra[4], ra[5], 16).astype(BF16)
    kc_ref[...] = qk_prep(y_kc, m_kc, gains_ref[3:4, 0:128],
                          ra[3][:, 0:128], ra[4][:, 0:128], ra[5][:, 0:128], 16).astype(BF16)


def _in_projection(xs, mod, g, w, rope, gains, lbt, ga, gc, tiles_per_seq):
    n = sum(a.shape[0] for a in xs)
    nt = n // ROW_TILE
    tps = tiles_per_seq

    def row(i):
        return (i, 0)

    def mod_idx(i):
        return (i, 0, 0)

    const = lambda i: (0, 0)
    widths = [(256, BF16), (256, BF16), (256, BF16), (512, BF16), (512, BF16),
              (512, F32), (512, F32), (512, BF16), (256, BF16), (128, BF16), (128, BF16)]
    return pl.pallas_call(
        functools.partial(_inproj_kernel, n_x=len(xs), tiles_per_seq=tps),
        grid=(nt,),
        in_specs=_token_row_specs(xs, tps) + [
            pl.BlockSpec((None, 2, D_MODEL), mod_idx),
            pl.BlockSpec((1, D_MODEL), const),
            pl.BlockSpec((D_MODEL, D_IN), const),
            pl.BlockSpec((ROW_TILE, 6 * 256), lambda i: (i % tps, 0)),
            pl.BlockSpec((4, 256), const),
            pl.BlockSpec((2, 1024), const),
            pl.BlockSpec((256, 256), const),
            pl.BlockSpec((256, 256), const),
        ],
        out_specs=[pl.BlockSpec((ROW_TILE, wd), row) for wd, _ in widths],
        out_shape=[jax.ShapeDtypeStruct((n, wd), dt) for wd, dt in widths],
        compiler_params=_cparams(("parallel",)),
        name="in_projection",
    )(*xs, mod, g, w, rope, gains, lbt, ga, gc)


def _diff_attend(q, k, v, lam):
    lane = lax.broadcasted_iota(jnp.int32, q.shape, 1)
    n_maps = 2 * DIFF_HEADS

    def scores(g):
        return _dot_nt(jnp.where(lane // DIFF_HEAD_DIM == g, q, jnp.zeros_like(q)), k)

    out = jnp.zeros(q.shape, F32)
    parts = []
    s_next = scores(0)
    for g in range(n_maps):
        s = s_next
        if g + 1 < n_maps:
            s_next = scores(g + 1)
        e = jnp.exp2(s - jnp.max(s, axis=-1, keepdims=True)).astype(BF16)
        o = _dot(e, v[g // 2])
        parts.append(o * (1.0 / pltpu.roll(o, DIFF_WIDTH - DIFF_V_DIM, 1)))
        if g % 2 == 1:
            out = out + jnp.where(lane // DIFF_V_DIM == g // 2, parts[g - 1] - lam * parts[g], 0.0)
    return out


def _diff_kernel(q_ref, k_ref, v_ref, lam_ref, post_ref, g64_ref, o_ref, vaug_ref, *, n_ctx):
    j = pl.program_id(1)
    lam = lam_ref[0:1, 0:1]

    @pl.when(j == 0)
    def _():
        v = v_ref[...]
        lane = lax.broadcasted_iota(jnp.int32, v.shape, 1)
        for hd in range(DIFF_HEADS):
            vaug_ref[hd] = jnp.where(lane // DIFF_V_DIM == (hd + 1) % DIFF_HEADS, jnp.ones_like(v), v)

    def finish(o):
        msq = _group_mean_sq(o, g64_ref[...])
        o_ref[...] = (o * lax.rsqrt(msq + NORM_EPS) * post_ref[...]).astype(BF16)

    @pl.when(j == 0)
    def _():
        finish(_diff_attend(q_ref[...], k_ref[0:n_ctx, :], [vaug_ref[hd, 0:n_ctx, :] for hd in range(DIFF_HEADS)],
                            lam))

    @pl.when(j > 0)
    def _():
        finish(_diff_attend(q_ref[...], k_ref[...], [vaug_ref[hd] for hd in range(DIFF_HEADS)], lam))


def _diff_attention(qa, ka, va, lam_row, post_row, g64, n_batch, t_len, n_ctx):
    tps = t_len // ROW_TILE
    q3 = qa.reshape(n_batch, t_len, DIFF_WIDTH)
    k3 = ka.reshape(n_batch, t_len, DIFF_WIDTH)
    v3 = va.reshape(n_batch, t_len, DIFF_WIDTH)
    const = lambda b, j: (0, 0)
    out = pl.pallas_call(
        functools.partial(_diff_kernel, n_ctx=n_ctx),
        grid=(n_batch, tps),
        in_specs=[
            pl.BlockSpec((None, ROW_TILE, DIFF_WIDTH), lambda b, j: (b, j, 0)),
            pl.BlockSpec((None, t_len, DIFF_WIDTH), lambda b, j: (b, 0, 0)),
            pl.BlockSpec((None, t_len, DIFF_WIDTH), lambda b, j: (b, 0, 0)),
            pl.BlockSpec((1, 128), const),
            pl.BlockSpec((1, DIFF_WIDTH), const),
            pl.BlockSpec((256, 256), const),
        ],
        out_specs=pl.BlockSpec((None, ROW_TILE, DIFF_WIDTH), lambda b, j: (b, j, 0)),
        out_shape=jax.ShapeDtypeStruct((n_batch, t_len, DIFF_WIDTH), BF16),
        scratch_shapes=[pltpu.VMEM((DIFF_HEADS, t_len, DIFF_WIDTH), BF16)],
        compiler_params=_cparams(("parallel", "arbitrary")),
        name="diff_attention",
    )(q3, k3, v3, lam_row, post_row, g64)
    return out.reshape(n_batch * t_len, DIFF_WIDTH)


def _swa_span_start(j, n_lat):
    q0 = (j - 1) * ROW_TILE
    return jnp.clip(q0 - WINDOW, 0, n_lat - (ROW_TILE + 2 * WINDOW)), q0


def _swa_kernel(q_ref, k_ref, v_ref, sink_ref, bias_ref, o_ref, *, n_ctx, n_lat):
    j = pl.program_id(1)
    q = q_ref[...]
    tq = q.shape[0]
    lane = lax.broadcasted_iota(jnp.int32, (tq, SWA_KV_WIDTH), 1)
    span = ROW_TILE + 2 * WINDOW
    heads = [(g, hd) for g in range(SWA_GROUP) for hd in range(SWA_KV_HEADS)]

    def run(k, v, bias):
        vlane = lax.broadcasted_iota(jnp.int32, v.shape, 1)
        vaug = [jnp.where(vlane // SWA_HEAD_DIM == hd, v, jnp.ones_like(v)) for hd in range(SWA_KV_HEADS)]
        sinks = [sink_ref[SWA_GROUP * hd + g:SWA_GROUP * hd + g + 1, 0:1] for g, hd in heads]
        scores = []
        for g, hd in heads:
            qg = q[:, SWA_KV_WIDTH * g:SWA_KV_WIDTH * (g + 1)]
            s = _dot_nt(jnp.where(lane // SWA_HEAD_DIM == hd, qg, jnp.zeros_like(qg)), k)
            scores.append(s if bias is None else s + bias)
        maxes = [jnp.maximum(jnp.max(s, axis=-1, keepdims=True), sink) for s, sink in zip(scores, sinks)]
        outs = [_dot(jnp.exp2(s - mx).astype(BF16), vaug[hd]) for s, mx, (g, hd) in zip(scores, maxes, heads)]
        for g in range(SWA_GROUP):
            out = jnp.zeros((tq, SWA_KV_WIDTH), F32)
            for i, (gi, hd) in enumerate(heads):
                if gi == g:
                    den = pltpu.roll(outs[i], SWA_HEAD_DIM, 1) + jnp.exp2(sinks[i] - maxes[i])
                    out = out + jnp.where(lane // SWA_HEAD_DIM == hd, outs[i] * (1.0 / den), 0.0)
            o_ref[:, SWA_KV_WIDTH * g:SWA_KV_WIDTH * (g + 1)] = out.astype(BF16)

    @pl.when(j == 0)
    def _():
        run(k_ref[0:n_ctx, :], v_ref[0:n_ctx, :], None)

    @pl.when(j > 0)
    def _():
        ks = pl.multiple_of(_swa_span_start(j, n_lat)[0], WINDOW)
        k = jnp.concatenate([k_ref[0:n_ctx, :], k_ref[pl.ds(n_ctx + ks, span), :]], axis=0)
        v = jnp.concatenate([v_ref[0:n_ctx, :], v_ref[pl.ds(n_ctx + ks, span), :]], axis=0)
        run(k, v, bias_ref[...])


def _swa_attention(qc, kc, vc, sink_tab, n_batch, t_len, n_ctx):
    tps = t_len // ROW_TILE
    n_lat = t_len - n_ctx
    n_keys = n_ctx + ROW_TILE + 2 * WINDOW
    q3 = qc.reshape(n_batch, t_len, SWA_WIDTH)
    k3 = kc.reshape(n_batch, t_len, SWA_KV_WIDTH)
    v3 = vc.reshape(n_batch, t_len, SWA_KV_WIDTH)
    row = jnp.arange(ROW_TILE)[None, :, None]
    key = jnp.arange(n_keys)[None, None, :] - n_ctx
    lead = WINDOW * jnp.arange(3)[:, None, None]
    bias = jnp.where((key < 0) | (jnp.abs(row + lead - key) <= WINDOW), 0.0, NEG_BIG).astype(F32)

    def bias_idx(b, j):
        ks, q0 = _swa_span_start(jnp.maximum(j, 1), n_lat)
        return ((q0 - ks) // WINDOW, 0, 0)

    out = pl.pallas_call(
        functools.partial(_swa_kernel, n_ctx=n_ctx, n_lat=n_lat),
        grid=(n_batch, tps),
        in_specs=[
            pl.BlockSpec((None, ROW_TILE, SWA_WIDTH), lambda b, j: (b, j, 0)),
            pl.BlockSpec((None, t_len, SWA_KV_WIDTH), lambda b, j: (b, 0, 0)),
            pl.BlockSpec((None, t_len, SWA_KV_WIDTH), lambda b, j: (b, 0, 0)),
            pl.BlockSpec((SWA_Q_HEADS, 128), lambda b, j: (0, 0)),
            pl.BlockSpec((None, ROW_TILE, n_keys), bias_idx),
        ],
        out_specs=pl.BlockSpec((None, ROW_TILE, SWA_WIDTH), lambda b, j: (b, j, 0)),
        out_shape=jax.ShapeDtypeStruct((n_batch, t_len, SWA_WIDTH), BF16),
        compiler_params=_cparams(("parallel", "arbitrary")),
        name="swa_attention",
    )(q3, k3, v3, sink_tab, bias)
    return out.reshape(n_batch * t_len, SWA_WIDTH)


def _hgrn_kernel(q_ref, v_ref, lff_ref, lfb_ref, g_ref, ng_ref, o_ref, of_ref, ob_ref, st_ref, cum_ref, qrow_ref,
                 strong_ref, *, n_ctx, t_len):
    c = HGRN_CHUNK
    n_chunks = t_len // c
    n_ctx_chunks = n_ctx // c
    mid = c // 2
    ri = lax.broadcasted_iota(jnp.int32, (c, c), 0)
    ci = lax.broadcasted_iota(jnp.int32, (c, c), 1)
    causal = ri >= ci
    anti = ri <= ci
    tri_f = jnp.where(causal, 1.0, 0.0).astype(BF16)
    tri_b = jnp.where(anti, 1.0, 0.0).astype(BF16)

    st_ref[...] = jnp.zeros_like(st_ref)
    n_iter = n_chunks // HGRN_UNROLL

    def chains_of(i):
        chains = []
        for u in range(HGRN_UNROLL):
            step = i * HGRN_UNROLL + u
            rf = pl.multiple_of(step * c, c)
            cb = jnp.where(step < n_ctx_chunks, n_ctx_chunks - 1 - step, n_chunks - 1 + n_ctx_chunks - step)
            rb = pl.multiple_of(cb * c, c)
            for hd in range(HGRN_HEADS):
                cols = slice(HGRN_K * hd, HGRN_K * (hd + 1))
                chains.append((2 * hd, pl.ds(rf, c), cols, lff_ref, of_ref, tri_f, causal, mid - 1, c - 1))
                chains.append((2 * hd + 1, pl.ds(rb, c), cols, lfb_ref, ob_ref, tri_b, anti, mid, 0))
        return chains

    def max_half_decay(i):
        worst = jnp.zeros((1, HGRN_K), F32)
        for _, rows, cols, lf_ref, _, _, _, _, _ in chains_of(i):
            lf = lf_ref[rows, cols]
            worst = jnp.maximum(worst, jnp.maximum(jnp.abs(jnp.sum(lf[0:mid, :], axis=0, keepdims=True)),
                                                   jnp.abs(jnp.sum(lf[mid:c, :], axis=0, keepdims=True))))
        return jnp.max(worst)

    def exact_step(i):
        for slot, rows, cols, lf_ref, o_ref_d, tri, mask, _, end_row in chains_of(i):
            lf = lf_ref[rows, cols]
            lf_hi, lf_lo = _split_bf16(lf)
            cum = _dot(tri, lf_hi) + _dot(tri, lf_lo)
            tot = cum[end_row:end_row + 1, :]
            v = v_ref[rows, cols]
            k = 1.0 - jnp.exp(lf)
            q = q_ref[rows, cols].astype(F32)
            cum_ref[...] = cum
            qrow_ref[...] = q

            def row(t, sct):
                w = jnp.exp(jnp.minimum(cum_ref[pl.ds(t, 1), :] - cum, 0.0))
                col = jnp.sum(qrow_ref[pl.ds(t, 1), :] * k * w, axis=-1, keepdims=True)
                return jnp.where(ci == t, col, sct)

            sct = lax.fori_loop(0, c, row, jnp.zeros((c, c), F32))
            valid_t = anti if mask is causal else causal
            st = st_ref[slot]
            o = (_dot_tn(jnp.where(valid_t, sct, 0.0).astype(BF16), v)
                 + _dot_nt((q * jnp.exp(cum)).astype(BF16), st.astype(BF16)))
            o_ref_d[rows, cols] = o.astype(BF16)
            st_ref[slot] = st * jnp.exp(tot) + _dot_tn(v, (k * jnp.exp(tot - cum)).astype(BF16))

    def fast_step(i):
        chains = chains_of(i)
        cums = []
        for _, rows, cols, lf_ref, _, tri, _, _, _ in chains:
            lf_hi, lf_lo = _split_bf16(lf_ref[rows, cols])
            cums.append(_dot(tri, lf_hi) + _dot(tri, lf_lo))
        prods = []
        for (slot, rows, cols, lf_ref, _, _, mask, ref_row, end_row), cum in zip(chains, cums):
            ref = cum[ref_row:ref_row + 1, :]
            tot = cum[end_row:end_row + 1, :]
            v = v_ref[rows, cols]
            k = 1.0 - jnp.exp(lf_ref[rows, cols])
            qt = q_ref[rows, cols].astype(F32) * jnp.exp(cum - ref)
            kt = k * jnp.exp(ref - cum)
            sc = _dot_nt(qt.astype(BF16), kt.astype(BF16))
            upd = _dot_tn(v, (kt * jnp.exp(tot - ref)).astype(BF16))
            prods.append((sc, (qt * jnp.exp(ref)).astype(BF16), jnp.exp(tot), upd, v))
        states = {}
        for (slot, rows, cols, _, o_ref_d, _, mask, _, _), (sc, q_in, decay, upd, v) in zip(chains, prods):
            st = states[slot] if slot in states else st_ref[slot]
            o = _dot(jnp.where(mask, sc, 0.0).astype(BF16), v) + _dot_nt(q_in, st.astype(BF16))
            o_ref_d[rows, cols] = o.astype(BF16)
            states[slot] = st * decay + upd
        for slot, st in states.items():
            st_ref[slot] = st

    def test_decay(i):
        strong_ref[0] = (max_half_decay(i) > HGRN_SAFE_DECAY).astype(jnp.int32)

    def body(i, _):
        strong = strong_ref[0]

        @pl.when(strong == 0)
        def _():
            test_decay(jnp.minimum(i + 1, n_iter - 1))
            fast_step(i)

        @pl.when(strong != 0)
        def _():
            test_decay(jnp.minimum(i + 1, n_iter - 1))
            exact_step(i)

        return 0

    test_decay(0)
    lax.fori_loop(0, n_iter, body, 0)

    def fin(t, _):
        r = pl.multiple_of(t * ROW_TILE, ROW_TILE)
        for hd in range(HGRN_HEADS):
            cols = slice(HGRN_K * hd, HGRN_K * (hd + 1))
            o = of_ref[pl.ds(r, ROW_TILE), cols].astype(F32) + ob_ref[pl.ds(r, ROW_TILE), cols].astype(F32)
            ms = jnp.mean(o * o, axis=-1, keepdims=True)
            o = o * lax.rsqrt(ms + NORM_EPS) * ng_ref[...]
            o_ref[pl.ds(r, ROW_TILE), cols] = (o * g_ref[pl.ds(r, ROW_TILE), cols].astype(F32)).astype(BF16)
        return 0

    lax.fori_loop(0, t_len // ROW_TILE, fin, 0)


def _hgrn(hq, hv, lff, lfb, hg, norm_g, n_batch, t_len, n_ctx):
    def view(a):
        return a.reshape(n_batch, t_len, HGRN_WIDTH)

    blk = pl.BlockSpec((None, t_len, HGRN_WIDTH), lambda b: (b, 0, 0))
    out = pl.pallas_call(
        functools.partial(_hgrn_kernel, n_ctx=n_ctx, t_len=t_len),
        grid=(n_batch,),
        in_specs=[blk] * 5 + [pl.BlockSpec((1, HGRN_K), lambda b: (0, 0))],
        out_specs=blk,
        out_shape=jax.ShapeDtypeStruct((n_batch, t_len, HGRN_WIDTH), BF16),
        scratch_shapes=[pltpu.VMEM((t_len, HGRN_WIDTH), BF16), pltpu.VMEM((t_len, HGRN_WIDTH), BF16),
                        pltpu.VMEM((2 * HGRN_HEADS, HGRN_K, HGRN_K), F32),
                        pltpu.VMEM((HGRN_CHUNK, HGRN_K), F32), pltpu.VMEM((HGRN_CHUNK, HGRN_K), F32),
                        pltpu.SMEM((1,), jnp.int32)],
        compiler_params=_cparams(("parallel",)),
        name="hgrn2",
    )(view(hq), view(hv), view(lff), view(lfb), view(hg), norm_g)
    return out.reshape(n_batch * t_len, HGRN_WIDTH)


def _mix_residual_norm(x, a_ref, b_ref, c_ref, w_ref, mod_ref, g_ref):
    y = (_dot(a_ref[...], w_ref[0:256, :]) + _dot(b_ref[...], w_ref[256:768, :])
         + _dot(c_ref[...], w_ref[768:1024, :]))
    x = x + mod_ref[0:1, :] * y
    ms = jnp.mean(x * x, axis=-1, keepdims=True)
    h = (x * lax.rsqrt(ms + NORM_EPS) * g_ref[...]) * (1.0 + mod_ref[2:3, :]) + mod_ref[1:2, :]
    return x, h


def _outproj_router_kernel(x_ref, a_ref, b_ref, c_ref, w_ref, mod_ref, g_ref, r_ref, xo_ref, h_ref, gate_ref):
    x, h = _mix_residual_norm(x_ref[...], a_ref, b_ref, c_ref, w_ref, mod_ref, g_ref)
    xo_ref[...] = x
    h_ref[...] = _pack_bf16_pairs(h)
    h_hi, h_lo = _split_bf16(h)
    r_hi, r_lo = _split_bf16(r_ref[...])
    logits = _dot(h_hi, r_hi) + _dot(h_lo, r_hi) + _dot(h_hi, r_lo)
    lane = lax.broadcasted_iota(jnp.int32, logits.shape, 1).astype(F32)
    logits = jnp.where(lane < N_EXPERTS, logits, NEG_BIG)
    v1 = jnp.max(logits, axis=-1, keepdims=True)
    i1 = jnp.min(jnp.where(logits == v1, lane, 128.0), axis=-1, keepdims=True)
    rest_l = jnp.where(lane == i1, NEG_BIG, logits)
    v2 = jnp.max(rest_l, axis=-1, keepdims=True)
    i2 = jnp.min(jnp.where(rest_l == v2, lane, 128.0), axis=-1, keepdims=True)
    e2 = jnp.exp(v2 - v1)
    w1 = 1.0 / (1.0 + e2)
    w2 = e2 * w1
    gate_ref[...] = (jnp.where(lane == i1, w1, 0.0) + jnp.where(lane == i2, w2, 0.0)
                     + jnp.where(lane == 8.0, i1, 0.0) + jnp.where(lane == 9.0, i2, 0.0)
                     + jnp.where(lane == 10.0, w1, 0.0) + jnp.where(lane == 11.0, w2, 0.0))


def _out_projection_router(xc, a, b, c, w, mod, g, router):
    n = xc.shape[0]
    nt = n // ROW_TILE
    row = lambda i: (i, 0)
    const = lambda i: (0, 0)
    return pl.pallas_call(
        _outproj_router_kernel,
        grid=(nt,),
        in_specs=[
            pl.BlockSpec((ROW_TILE, D_MODEL), row),
            pl.BlockSpec((ROW_TILE, DIFF_WIDTH), row),
            pl.BlockSpec((ROW_TILE, HGRN_WIDTH), row),
            pl.BlockSpec((ROW_TILE, SWA_WIDTH), row),
            pl.BlockSpec((D_MODEL, D_MODEL), const),
            pl.BlockSpec((None, 3, D_MODEL), lambda i: (i, 0, 0)),
            pl.BlockSpec((1, D_MODEL), const),
            pl.BlockSpec((D_MODEL, 128), const),
        ],
        out_specs=[pl.BlockSpec((ROW_TILE, D_MODEL), row), pl.BlockSpec((ROW_TILE, D_MODEL // 2), row),
                   pl.BlockSpec((ROW_TILE, 128), row)],
        out_shape=[jax.ShapeDtypeStruct((n, D_MODEL), F32), jax.ShapeDtypeStruct((n, D_MODEL // 2), jnp.int32),
                   jax.ShapeDtypeStruct((n, 128), F32)],
        compiler_params=_cparams(("parallel",)),
        name="out_projection",
    )(xc, a, b, c, w, mod, g, router)


def _outproj_ffn_kernel(*refs, n_x, tiles_per_seq):
    a_ref, b_ref, c_ref, wo_ref, mod_ref, g_ref, w1_ref, w3_ref, w2_ref, o_ref = refs[n_x:]
    x, h = _mix_residual_norm(_token_rows(refs[:n_x], tiles_per_seq), a_ref, b_ref, c_ref, wo_ref, mod_ref, g_ref)
    h = h.astype(BF16)
    u = _dot(h, w1_ref[...])
    act = (_silu(u) * _dot(h, w3_ref[...])).astype(BF16)
    o_ref[...] = x + mod_ref[3:4, :] * _dot(act, w2_ref[...])


def _outproj_dense_ffn(xs, a, b, c, wo, mod, g, w1, w3, w2, tiles_per_seq):
    n = a.shape[0]
    nt = n // ROW_TILE
    row = lambda i: (i, 0)
    const = lambda i: (0, 0)
    return pl.pallas_call(
        functools.partial(_outproj_ffn_kernel, n_x=len(xs), tiles_per_seq=tiles_per_seq),
        grid=(nt,),
        in_specs=_token_row_specs(xs, tiles_per_seq) + [
            pl.BlockSpec((ROW_TILE, DIFF_WIDTH), row),
            pl.BlockSpec((ROW_TILE, HGRN_WIDTH), row),
            pl.BlockSpec((ROW_TILE, SWA_WIDTH), row),
            pl.BlockSpec((D_MODEL, D_MODEL), const),
            pl.BlockSpec((None, 4, D_MODEL), lambda i: (i, 0, 0)),
            pl.BlockSpec((1, D_MODEL), const),
            pl.BlockSpec((D_MODEL, D_FF), const),
            pl.BlockSpec((D_MODEL, D_FF), const),
            pl.BlockSpec((D_FF, D_MODEL), const),
        ],
        out_specs=pl.BlockSpec((ROW_TILE, D_MODEL), row),
        out_shape=jax.ShapeDtypeStruct((n, D_MODEL), F32),
        compiler_params=_cparams(("parallel",)),
        name="outproj_dense_ffn",
    )(*xs, a, b, c, wo, mod, g, w1, w3, w2)


MOE_TM = 256
SC_CORES = 2
SC_SUBCORES = 16
SC_GATHER_ROWS = 128
SC_SPILL_ROWS = 4096


def _pack_bf16_pairs(h):
    half = h.shape[1] // 2
    bits = pltpu.bitcast(h.astype(BF16).astype(F32), jnp.uint32)
    packed = (bits[:, :half] >> 16) | (bits[:, half:] & jnp.uint32(0xFFFF0000))
    return pltpu.bitcast(packed, jnp.int32)


def _unpack_bf16_pairs(u):
    bits = pltpu.bitcast(u, jnp.uint32)
    lo = pltpu.bitcast(bits << 16, F32)
    hi = pltpu.bitcast(bits & jnp.uint32(0xFFFF0000), F32)
    return jnp.concatenate([lo, hi], axis=1)


def _sc_gather_rows(table, idx):
    n_workers = SC_CORES * SC_SUBCORES
    n_rows, width = idx.shape[0], table.shape[1]
    assert n_rows % (n_workers * SC_GATHER_ROWS) == 0
    per_worker = n_rows // n_workers
    mesh = plsc.VectorSubcoreMesh(core_axis_name="c", subcore_axis_name="s",
                                  num_cores=SC_CORES, num_subcores=SC_SUBCORES)

    @functools.partial(
        pl.kernel, mesh=mesh,
        out_type=jax.ShapeDtypeStruct((n_rows, width), table.dtype),
        scratch_types=[pltpu.VMEM((SC_GATHER_ROWS,), jnp.int32),
                       pltpu.VMEM((SC_GATHER_ROWS, width), table.dtype),
                       pltpu.SemaphoreType.DMA],
        name="sc_gather_rows",
    )
    def gather(table_hbm, idx_hbm, out_hbm, idx_v, rows_v, sem):
        worker = lax.axis_index("s") * SC_CORES + lax.axis_index("c")
        base = worker * per_worker

        @pl.loop(0, per_worker // SC_GATHER_ROWS)
        def _(step):
            off = pl.multiple_of(base + step * SC_GATHER_ROWS, SC_GATHER_ROWS)
            pltpu.sync_copy(idx_hbm.at[pl.ds(off, SC_GATHER_ROWS)], idx_v)
            pltpu.async_copy(table_hbm.at[idx_v], rows_v, sem).wait()
            pltpu.sync_copy(rows_v, out_hbm.at[pl.ds(off, SC_GATHER_ROWS)])

    return gather(table, idx)


def _sc_scatter_rows(rows, idx, n_out):
    n_workers = SC_CORES * SC_SUBCORES
    n_rows, width = rows.shape
    assert n_rows % (n_workers * SC_GATHER_ROWS) == 0
    chunks_per_worker = n_rows // (n_workers * SC_GATHER_ROWS)
    idx = idx.reshape(2, n_rows // SC_GATHER_ROWS, SC_GATHER_ROWS)
    mesh = plsc.VectorSubcoreMesh(core_axis_name="c", subcore_axis_name="s",
                                  num_cores=SC_CORES, num_subcores=SC_SUBCORES)

    @functools.partial(
        pl.kernel, mesh=mesh,
        out_type=jax.ShapeDtypeStruct((n_out, width), rows.dtype),
        scratch_types=[pltpu.VMEM((SC_GATHER_ROWS,), jnp.int32), pltpu.VMEM((SC_GATHER_ROWS,), jnp.int32),
                       pltpu.VMEM((SC_GATHER_ROWS, width), rows.dtype),
                       pltpu.SemaphoreType.DMA, pltpu.SemaphoreType.DMA],
        name="sc_scatter_rows",
    )
    def scatter(rows_hbm, idx_hbm, out_hbm, idx0_v, idx1_v, rows_v, sem0, sem1):
        worker = lax.axis_index("s") * SC_CORES + lax.axis_index("c")

        @pl.loop(0, chunks_per_worker)
        def _(step):
            chunk = worker * chunks_per_worker + step
            off = pl.multiple_of(chunk * SC_GATHER_ROWS, SC_GATHER_ROWS)
            pltpu.sync_copy(rows_hbm.at[pl.ds(off, SC_GATHER_ROWS)], rows_v)
            pltpu.sync_copy(idx_hbm.at[0, chunk], idx0_v)
            first = pltpu.async_copy(rows_v, out_hbm.at[idx0_v], sem0)
            pltpu.sync_copy(idx_hbm.at[1, chunk], idx1_v)
            second = pltpu.async_copy(rows_v, out_hbm.at[idx1_v], sem1)
            first.wait()
            second.wait()

    return scatter(rows, idx)


def _moe_expert_kernel(te_ref, tv_ref, xs_ref, w1_ref, w3_ref, w2_ref, ys_ref):
    r = pl.program_id(0)

    @pl.when(tv_ref[r] == 1)
    def _():
        h = _unpack_bf16_pairs(xs_ref[...]).astype(BF16)
        u = _dot(h, w1_ref[...])
        act = (_silu(u) * _dot(h, w3_ref[...])).astype(BF16)
        ys_ref[...] = _pack_bf16_pairs(_dot(act, w2_ref[...]))

    @pl.when(tv_ref[r] == 0)
    def _():
        ys_ref[...] = jnp.zeros_like(ys_ref)


def _moe_combine_kernel(x_ref, y0_ref, y1_ref, route_ref, mod_ref, o_ref):
    g0, g1 = route_ref[:, 10:11], route_ref[:, 11:12]
    y = (jnp.where(g0 != 0.0, g0 * _unpack_bf16_pairs(y0_ref[...]), 0.0)
         + jnp.where(g1 != 0.0, g1 * _unpack_bf16_pairs(y1_ref[...]), 0.0))
    o_ref[...] = x_ref[...] + mod_ref[...] * y


def _moe_plan(route, n_rt):
    n = route.shape[0]
    tm = MOE_TM
    gates_t = route[:, :N_EXPERTS].T
    sel = gates_t != 0.0
    si = sel.astype(jnp.int32)
    rank = jnp.cumsum(si, axis=1) - si
    ntile = (jnp.sum(si, axis=1) + tm - 1) // tm
    tile_end = jnp.cumsum(ntile)
    tile_off = tile_end - ntile
    used = tile_end[-1]
    n_rows = n_rt * tm
    dest = jnp.where(sel, tile_off[:, None] * tm + rank, n_rows)
    expert = jnp.arange(N_EXPERTS, dtype=jnp.int32)[:, None]

    def row_of(lane):
        pick = route[:, lane].astype(jnp.int32)[None, :]
        return jnp.sum(jnp.where(expert == pick, dest, 0), axis=0).astype(jnp.int32)

    pos = jnp.stack([row_of(8), row_of(9)])
    r = jnp.arange(n_rt, dtype=jnp.int32)
    rc = jnp.minimum(r, used - 1)
    tile_expert = jnp.sum((tile_end[None, :] <= rc[:, None]).astype(jnp.int32), axis=1)
    tile_valid = (r < used).astype(jnp.int32)
    return pos, tile_expert, tile_valid


def _moe_ffn(x, h_packed, route, w1, w3, w2, layer_idx, mod, latent_tiles_per_seq=None):
    n = x.shape[0]
    nt = n // ROW_TILE
    tm = MOE_TM
    half = D_MODEL // 2
    n_rt = 2 * n // tm + N_EXPERTS
    pos, tile_expert, tile_valid = _moe_plan(route, n_rt)
    routed = pos < n_rt * tm
    token = jnp.arange(n, dtype=jnp.int32)[None, :]

    xs = _sc_scatter_rows(h_packed, jnp.where(routed, pos, n_rt * tm + token % SC_SPILL_ROWS),
                          n_rt * tm + SC_SPILL_ROWS)
    expert_w = lambda r, te, tv: (layer_idx, te[r], 0, 0)
    ys = pl.pallas_call(
        _moe_expert_kernel,
        grid_spec=pltpu.PrefetchScalarGridSpec(
            num_scalar_prefetch=2,
            grid=(n_rt,),
            in_specs=[
                pl.BlockSpec((tm, half), lambda r, te, tv: (r, 0)),
                pl.BlockSpec((None, None, D_MODEL, D_FF), expert_w),
                pl.BlockSpec((None, None, D_MODEL, D_FF), expert_w),
                pl.BlockSpec((None, None, D_FF, D_MODEL), expert_w),
            ],
            out_specs=pl.BlockSpec((tm, half), lambda r, te, tv: (r, 0)),
        ),
        out_shape=jax.ShapeDtypeStruct((n_rt * tm, half), jnp.int32),
        compiler_params=_cparams(("arbitrary",)),
        name="moe_experts",
    )(tile_expert, tile_valid, xs, w1, w3, w2)

    y = _sc_gather_rows(ys, jnp.where(routed, pos, token).reshape(-1))
    row = lambda i: (i, 0)
    if latent_tiles_per_seq is None:
        out_rows, out_row = n, row
    else:
        lt = latent_tiles_per_seq
        out_rows = n // (lt + 1) * lt
        out_row = lambda i: ((i // (lt + 1)) * lt + jnp.maximum(i % (lt + 1) - 1, 0), 0)
    return pl.pallas_call(
        _moe_combine_kernel,
        grid=(nt,),
        in_specs=[
            pl.BlockSpec((ROW_TILE, D_MODEL), row),
            pl.BlockSpec((ROW_TILE, half), row),
            pl.BlockSpec((ROW_TILE, half), lambda i: (nt + i, 0)),
            pl.BlockSpec((ROW_TILE, 128), row),
            pl.BlockSpec((None, 1, D_MODEL), lambda i: (i, 0, 0)),
        ],
        out_specs=pl.BlockSpec((ROW_TILE, D_MODEL), out_row),
        out_shape=jax.ShapeDtypeStruct((out_rows, D_MODEL), F32),
        compiler_params=_cparams(("arbitrary",)),
        name="moe_combine",
    )(x, y, y, route, mod)


def _rope_tables(n_ctx, n_lat):
    pos_r = jnp.arange(n_lat, dtype=jnp.int32) // GRID_W
    pos_c = jnp.arange(n_lat, dtype=jnp.int32) % GRID_W

    def per_head(head_dim):
        nf = head_dim // 4
        inv = ROPE_BASE ** (-jnp.arange(nf, dtype=F32) / nf)
        ang_r = pos_r.astype(F32)[:, None] * inv[None, :]
        ang_c = pos_c.astype(F32)[:, None] * inv[None, :]
        z = jnp.zeros_like(ang_r)
        cos = jnp.concatenate([jnp.cos(ang_r)] * 2 + [jnp.cos(ang_c)] * 2, axis=-1)
        sa = jnp.concatenate([-jnp.sin(ang_r), z, -jnp.sin(ang_c), z], axis=-1)
        sb = jnp.concatenate([z, jnp.sin(ang_r), z, jnp.sin(ang_c)], axis=-1)
        reps = 256 // head_dim
        tabs = [jnp.tile(t, (1, reps)) for t in (cos, sa, sb)]
        ctx = [jnp.ones((n_ctx, 256), F32), jnp.zeros((n_ctx, 256), F32), jnp.zeros((n_ctx, 256), F32)]
        return [jnp.concatenate([c, t], axis=0) for c, t in zip(ctx, tabs)]

    return jnp.concatenate(per_head(DIFF_HEAD_DIM) + per_head(SWA_HEAD_DIM), axis=-1)


def _block_diag_mean(group):
    idx = jnp.arange(256) // group
    return jnp.where(idx[:, None] == idx[None, :], 1.0 / group, 0.0).astype(BF16)


def kernel(x, c, ctx, c_ctx, ada_w, ada_b, norm_mix_g, norm_ffn_g, w_in, w_out, diff_qk_norm_g, diff_lambda,
           diff_subln_g, hgrn_lb_logits, hgrn_norm_g, swa_qk_norm_g, swa_sink, ffn_w1, ffn_w3, ffn_w2,
           moe_router, moe_w1, moe_w3, moe_w2):
    n_batch, n_lat, _ = x.shape
    n_ctx = ctx.shape[1]
    depth = ada_w.shape[0]
    t_len = n_ctx + n_lat
    tps = t_len // ROW_TILE
    assert n_ctx == ROW_TILE and n_lat % ROW_TILE == 0 and n_lat >= ROW_TILE + 2 * WINDOW

    xs = (x.reshape(n_batch * n_lat, D_MODEL), ctx.reshape(n_batch * n_ctx, D_MODEL))

    n_rows = -(-(n_batch + 1) // 8) * 8
    cond = jnp.concatenate([c, c_ctx[None, :], jnp.zeros((n_rows - n_batch - 1, D_MODEL), F32)], axis=0)
    mods = _ada_modulation(cond, ada_w, ada_b).reshape(depth, n_rows, 6, D_MODEL)
    m_lat = jnp.broadcast_to(mods[:, :n_batch, None], (depth, n_batch, tps - 1, 6, D_MODEL))
    m_ctx = jnp.broadcast_to(mods[:, n_batch, None, None], (depth, n_batch, 1, 6, D_MODEL))
    mods = jnp.concatenate([m_ctx, m_lat], axis=2).reshape(depth, n_batch * tps, 6, D_MODEL)

    lb = jnp.cumsum(jax.nn.softmax(hgrn_lb_logits.astype(F32), axis=1), axis=1)
    lb = lb - lb[:, :1]
    rope = _rope_tables(n_ctx, n_lat)
    g32 = _block_diag_mean(DIFF_HEAD_DIM)
    g64 = _block_diag_mean(SWA_HEAD_DIM)

    perm_q = jnp.arange(SWA_WIDTH).reshape(SWA_KV_HEADS, SWA_GROUP, SWA_HEAD_DIM).transpose(1, 0, 2).reshape(-1)
    qc0 = 3 * DIFF_WIDTH + 5 * HGRN_WIDTH
    col_perm = jnp.concatenate([jnp.arange(qc0), qc0 + perm_q, jnp.arange(qc0 + SWA_WIDTH, D_IN)])
    oc0 = DIFF_WIDTH + HGRN_WIDTH
    row_perm = jnp.concatenate([jnp.arange(oc0), oc0 + perm_q])
    moe_w1_b, moe_w3_b, moe_w2_b = moe_w1.astype(BF16), moe_w3.astype(BF16), moe_w2.astype(BF16)

    for layer in range(depth):
        lam_init = 0.8 - 0.6 * math.exp(-0.3 * layer)
        mod = mods[layer]
        w_in_l = w_in[layer][:, col_perm].astype(BF16)
        w_out_l = w_out[layer][row_perm, :].astype(BF16)
        gains = jnp.stack([
            jnp.tile(diff_qk_norm_g[layer, 0], 8) * (DIFF_HEAD_DIM ** -0.5 * LOG2_E),
            jnp.tile(diff_qk_norm_g[layer, 1], 8),
            jnp.tile(swa_qk_norm_g[layer, 0], 4) * (SWA_HEAD_DIM ** -0.5 * LOG2_E),
            jnp.tile(swa_qk_norm_g[layer, 1], 4),
        ]).astype(F32)
        lbt = jnp.stack([
            jnp.log(lb[:, layer]).reshape(-1),
            jnp.log1p(-lb[:, layer]).reshape(-1),
        ]).astype(F32)
        lv = diff_lambda[layer].astype(F32)
        lam = jnp.exp(jnp.sum(lv[0] * lv[1])) - jnp.exp(jnp.sum(lv[2] * lv[3])) + lam_init
        lam_row = jnp.full((1, 128), lam, F32)
        post_row = (jnp.tile(diff_subln_g[layer], DIFF_HEADS) * (1.0 - lam_init)).reshape(1, DIFF_WIDTH).astype(F32)
        sink_tab = jnp.broadcast_to(swa_sink[layer].astype(F32)[:, None] * LOG2_E, (SWA_Q_HEADS, 128))

        (qa, ka, va, hq, hv, lff, lfb, hg, qc, kc, vc) = _in_projection(
            xs, mod[:, 0:2], norm_mix_g[layer].reshape(1, D_MODEL), w_in_l, rope, gains, lbt, g32, g64, tps)
        a = _diff_attention(qa, ka, va, lam_row, post_row, g64, n_batch, t_len, n_ctx)
        b = _hgrn(hq, hv, lff, lfb, hg, hgrn_norm_g[layer].reshape(1, HGRN_K), n_batch, t_len, n_ctx)
        cc = _swa_attention(qc, kc, vc, sink_tab, n_batch, t_len, n_ctx)

        jj = layer // 2
        g2 = norm_ffn_g[layer].reshape(1, D_MODEL)
        if layer % 2 == 1:
            router = jnp.pad(moe_router[jj].astype(F32), ((0, 0), (0, 128 - N_EXPERTS)))
            x_mid, h2, route = _out_projection_router(xs[0], a, b, cc, w_out_l, mod[:, 2:5], g2, router)
            if layer == depth - 1:
                is_lat = (jnp.arange(n_batch * t_len, dtype=jnp.int32) % t_len) >= n_ctx
                route = jnp.where(is_lat[:, None], route, 0.0)
            xc = _moe_ffn(x_mid, h2, route, moe_w1_b, moe_w3_b, moe_w2_b, jj, mod[:, 5:6],
                          latent_tiles_per_seq=tps - 1 if layer == depth - 1 else None)
        else:
            xc = _outproj_dense_ffn(xs, a, b, cc, w_out_l, mod[:, 2:6], g2, ffn_w1[jj].astype(BF16),
                                    ffn_w3[jj].astype(BF16), ffn_w2[jj].astype(BF16), tps)
        xs = (xc,)

    if depth % 2 == 0:
        return xc.reshape(n_batch, n_lat, D_MODEL)
    return xc.reshape(n_batch, t_len, D_MODEL)[:, n_ctx:, :]
```

```python
import functools
import math

import jax
import jax.numpy as jnp
from jax import lax
from jax.experimental import pallas as pl
from jax.experimental.pallas import tpu as pltpu
from jax.experimental.pallas import tpu_sc as plsc

D_MODEL = 1024
GRID_W = 64
DIFF_HEADS = 4
DIFF_HEAD_DIM = 32
DIFF_V_DIM = 64
DIFF_WIDTH = 256
HGRN_HEADS = 4
HGRN_K = 128
HGRN_WIDTH = 512
SWA_Q_HEADS = 4
SWA_KV_HEADS = 2
SWA_GROUP = 2
SWA_HEAD_DIM = 64
SWA_WIDTH = 256
SWA_KV_WIDTH = 128
WINDOW = 128
D_FF = 2816
N_EXPERTS = 8
ROPE_BASE = 10000.0
NORM_EPS = 1e-6
D_IN = 3840

F32 = jnp.float32
BF16 = jnp.bfloat16

ROW_TILE = 256
HGRN_CHUNK = 64
HGRN_UNROLL = 4
HGRN_SAFE_DECAY = 80.0
NEG_BIG = -1e30
LOG2_E = 1.4426950408889634
VMEM_LIMIT = 56 * 1024 * 1024


def _cparams(sem):
    return pltpu.CompilerParams(dimension_semantics=sem, vmem_limit_bytes=VMEM_LIMIT)


def _split_bf16(v):
    hi = v.astype(BF16)
    lo = (v - hi.astype(F32)).astype(BF16)
    return hi, lo


def _dot(a, b):
    return jnp.dot(a, b, preferred_element_type=F32)


def _dot_nt(a, b):
    return lax.dot_general(a, b, (((1,), (1,)), ((), ())), preferred_element_type=F32)


def _dot_tn(a, b):
    return lax.dot_general(a, b, (((0,), (0,)), ((), ())), preferred_element_type=F32)


def _group_mean_sq(y, gmat):
    hi, lo = _split_bf16(y * y)
    return _dot(hi, gmat) + _dot(lo, gmat)


def _silu(v):
    return v * (1.0 / (1.0 + jnp.exp(-v)))


def _ada_kernel(s_ref, w_ref, b_ref, o_ref):
    s = s_ref[...]
    s = _silu(s)
    s_hi, s_lo = _split_bf16(s)
    w_hi, w_lo = _split_bf16(w_ref[...])
    o_ref[...] = _dot(s_hi, w_hi) + _dot(s_lo, w_hi) + _dot(s_hi, w_lo) + b_ref[...]


def _ada_modulation(cond, ada_w, ada_b):
    depth = ada_w.shape[0]
    r = cond.shape[0]
    nblk = 6 * D_MODEL // 1024
    return pl.pallas_call(
        _ada_kernel,
        grid=(depth, nblk),
        in_specs=[
            pl.BlockSpec((r, D_MODEL), lambda l, n: (0, 0)),
            pl.BlockSpec((None, D_MODEL, 1024), lambda l, n: (l, 0, n)),
            pl.BlockSpec((None, 1, 1024), lambda l, n: (l, 0, n)),
        ],
        out_specs=pl.BlockSpec((None, r, 1024), lambda l, n: (l, 0, n)),
        out_shape=jax.ShapeDtypeStruct((depth, r, 6 * D_MODEL), F32),
        compiler_params=_cparams(("parallel", "parallel")),
        name="ada_modulation",
    )(cond, ada_w, ada_b.reshape(depth, 1, 6 * D_MODEL))


def _rope(y, cos, sa, sb, half):
    w = y.shape[-1]
    fwd = pltpu.roll(y, w - half, 1)
    bwd = pltpu.roll(y, half, 1)
    return y * cos + fwd * sa + bwd * sb


def _token_row_specs(xs, tiles_per_seq):
    if len(xs) == 1:
        return [pl.BlockSpec((ROW_TILE, D_MODEL), lambda i: (i, 0))]
    tps = tiles_per_seq
    return [pl.BlockSpec((ROW_TILE, D_MODEL), lambda i: ((i // tps) * (tps - 1) + jnp.maximum(i % tps - 1, 0), 0)),
            pl.BlockSpec((ROW_TILE, D_MODEL), lambda i: (i // tps, 0))]


def _token_rows(x_refs, tiles_per_seq):
    if len(x_refs) == 1:
        return x_refs[0][...]
    return jnp.where(pl.program_id(0) % tiles_per_seq == 0, x_refs[1][...], x_refs[0][...])


def _inproj_kernel(*refs, n_x, tiles_per_seq):
    (mod_ref, g_ref, w_ref, rope_ref, gains_ref, lbt_ref, ga_ref, gc_ref,
     qa_ref, ka_ref, va_ref, hq_ref, hv_ref, lff_ref, lfb_ref, hg_ref, qc_ref, kc_ref, vc_ref) = refs[n_x:]
    x = _token_rows(refs[:n_x], tiles_per_seq)
    shift = mod_ref[0:1, :]
    scale = mod_ref[1:2, :]
    ms = jnp.mean(x * x, axis=-1, keepdims=True)
    h = (x * lax.rsqrt(ms + NORM_EPS) * g_ref[...]) * (1.0 + scale) + shift
    hb = h.astype(BF16)

    def proj(a, b):
        return _dot(hb, w_ref[:, a:b])

    def qk_prep(y, msq, gain, cos, sa, sb, half):
        y = y * lax.rsqrt(msq + NORM_EPS) * gain
        return _rope(y, cos, sa, sb, half)

    ga = ga_ref[...]
    gc = gc_ref[...]
    ra = [rope_ref[:, 256 * i:256 * (i + 1)] for i in range(6)]
    y_qa, y_ka, y_qc, y_kc = proj(0, 256), proj(256, 512), proj(3328, 3584), proj(3584, 3712)
    va_ref[...] = proj(512, 768).astype(BF16)
    hq_ref[...] = _silu(proj(768, 1280)).astype(BF16)
    m_qa, m_ka = _group_mean_sq(y_qa, ga), _group_mean_sq(y_ka, ga)
    m_qc, m_kc = _group_mean_sq(y_qc, gc), _group_mean_sq(y_kc, gc[0:128, 0:128])
    hv_ref[...] = proj(1280, 1792).astype(BF16)
    for d, lf_ref in enumerate((lff_ref, lfb_ref)):
        z = proj(1792 + 512 * d, 2304 + 512 * d)
        log_lb = lbt_ref[0:1, 512 * d:512 * (d + 1)]
        log1m_lb = lbt_ref[1:2, 512 * d:512 * (d + 1)]
        sp = jnp.maximum(-z, 0.0) + jnp.log(1.0 + jnp.exp(-jnp.abs(z)))
        b2 = log1m_lb - sp
        mx = jnp.maximum(log_lb, b2)
        lf_ref[...] = mx + jnp.log(1.0 + jnp.exp(-jnp.abs(log_lb - b2)))
    hg_ref[...] = _silu(proj(2816, 3328)).astype(BF16)
    vc_ref[...] = proj(3712, 3840).astype(BF16)
    qa_ref[...] = qk_prep(y_qa, m_qa, gains_ref[0:1, :], ra[0], ra[1], ra[2], 8).astype(BF16)
    ka_ref[...] = qk_prep(y_ka, m_ka, gains_ref[1:2, :], ra[0], ra[1], ra[2], 8).astype(BF16)
    qc_ref[...] = qk_prep(y_qc, m_qc, gains_ref[2:3, :], ra[3], ra[4], ra[5], 16).astype(BF16)
    kc_ref[...] = qk_prep(y_kc, m_kc, gains_ref[3:4, 0:128],
                          ra[3][:, 0:128], ra[4][:, 0:128], ra[5][:, 0:128], 16).astype(BF16)


def _in_projection(xs, mod, g, w, rope, gains, lbt, ga, gc, tiles_per_seq):
    n = sum(a.shape[0] for a in xs)
    nt = n // ROW_TILE
    tps = tiles_per_seq

    def row(i):
        return (i, 0)

    def mod_idx(i):
        return (i, 0, 0)

    const = lambda i: (0, 0)
    widths = [(256, BF16), (256, BF16), (256, BF16), (512, BF16), (512, BF16),
              (512, F32), (512, F32), (512, BF16), (256, BF16), (128, BF16), (128, BF16)]
    return pl.pallas_call(
        functools.partial(_inproj_kernel, n_x=len(xs), tiles_per_seq=tps),
        grid=(nt,),
        in_specs=_token_row_specs(xs, tps) + [
            pl.BlockSpec((None, 2, D_MODEL), mod_idx),
            pl.BlockSpec((1, D_MODEL), const),
            pl.BlockSpec((D_MODEL, D_IN), const),
            pl.BlockSpec((ROW_TILE, 6 * 256), lambda i: (i % tps, 0)),
            pl.BlockSpec((4, 256), const),
            pl.BlockSpec((2, 1024), const),
            pl.BlockSpec((256, 256), const),
            pl.BlockSpec((256, 256), const),
        ],
        out_specs=[pl.BlockSpec((ROW_TILE, wd), row) for wd, _ in widths],
        out_shape=[jax.ShapeDtypeStruct((n, wd), dt) for wd, dt in widths],
        compiler_params=_cparams(("parallel",)),
        name="in_projection",
    )(*xs, mod, g, w, rope, gains, lbt, ga, gc)


def _diff_attend(q, k, v, lam):
    lane = lax.broadcasted_iota(jnp.int32, q.shape, 1)
    n_maps = 2 * DIFF_HEADS

    def scores(g):
        return _dot_nt(jnp.where(lane // DIFF_HEAD_DIM == g, q, jnp.zeros_like(q)), k)

    out = jnp.zeros(q.shape, F32)
    parts = []
    s_next = scores(0)
    for g in range(n_maps):
        s = s_next
        if g + 1 < n_maps:
            s_next = scores(g + 1)
        e = jnp.exp2(s - jnp.max(s, axis=-1, keepdims=True)).astype(BF16)
        o = _dot(e, v[g // 2])
        parts.append(o * (1.0 / pltpu.roll(o, DIFF_WIDTH - DIFF_V_DIM, 1)))
        if g % 2 == 1:
            oh = jnp.where(lane // DIFF_V_DIM == g // 2, parts[g - 1] - lam * parts[g], 0.0)
            msq = jnp.sum(oh * oh, axis=-1, keepdims=True) * (1.0 / DIFF_V_DIM)
            out = out + oh * lax.rsqrt(msq + NORM_EPS)
    return out


def _diff_kernel(q_ref, k_ref, v_ref, lam_ref, post_ref, g64_ref, o_ref, vaug_ref, *, n_ctx):
    j = pl.program_id(1)
    lam = lam_ref[0:1, 0:1]

    @pl.when(j == 0)
    def _():
        v = v_ref[...]
        lane = lax.broadcasted_iota(jnp.int32, v.shape, 1)
        for hd in range(DIFF_HEADS):
            vaug_ref[hd] = jnp.where(lane // DIFF_V_DIM == (hd + 1) % DIFF_HEADS, jnp.ones_like(v), v)

    def finish(o):
        o_ref[...] = (o * post_ref[...]).astype(BF16)

    @pl.when(j == 0)
    def _():
        finish(_diff_attend(q_ref[...], k_ref[0:n_ctx, :], [vaug_ref[hd, 0:n_ctx, :] for hd in range(DIFF_HEADS)],
                            lam))

    @pl.when(j > 0)
    def _():
        finish(_diff_attend(q_ref[...], k_ref[...], [vaug_ref[hd] for hd in range(DIFF_HEADS)], lam))


def _diff_attention(qa, ka, va, lam_row, post_row, g64, n_batch, t_len, n_ctx):
    tps = t_len // ROW_TILE
    q3 = qa.reshape(n_batch, t_len, DIFF_WIDTH)
    k3 = ka.reshape(n_batch, t_len, DIFF_WIDTH)
    v3 = va.reshape(n_batch, t_len, DIFF_WIDTH)
    const = lambda b, j: (0, 0)
    out = pl.pallas_call(
        functools.partial(_diff_kernel, n_ctx=n_ctx),
        grid=(n_batch, tps),
        in_specs=[
            pl.BlockSpec((None, ROW_TILE, DIFF_WIDTH), lambda b, j: (b, j, 0)),
            pl.BlockSpec((None, t_len, DIFF_WIDTH), lambda b, j: (b, 0, 0)),
            pl.BlockSpec((None, t_len, DIFF_WIDTH), lambda b, j: (b, 0, 0)),
            pl.BlockSpec((1, 128), const),
            pl.BlockSpec((1, DIFF_WIDTH), const),
            pl.BlockSpec((256, 256), const),
        ],
        out_specs=pl.BlockSpec((None, ROW_TILE, DIFF_WIDTH), lambda b, j: (b, j, 0)),
        out_shape=jax.ShapeDtypeStruct((n_batch, t_len, DIFF_WIDTH), BF16),
        scratch_shapes=[pltpu.VMEM((DIFF_HEADS, t_len, DIFF_WIDTH), BF16)],
        compiler_params=_cparams(("parallel", "arbitrary")),
        name="diff_attention",
    )(q3, k3, v3, lam_row, post_row, g64)
    return out.reshape(n_batch * t_len, DIFF_WIDTH)


def _swa_span_start(j, n_lat):
    q0 = (j - 1) * ROW_TILE
    return jnp.clip(q0 - WINDOW, 0, n_lat - (ROW_TILE + 2 * WINDOW)), q0


def _swa_kernel(q_ref, k_ref, v_ref, sink_ref, bias_ref, o_ref, *, n_ctx, n_lat):
    j = pl.program_id(1)
    q = q_ref[...]
    tq = q.shape[0]
    lane = lax.broadcasted_iota(jnp.int32, (tq, SWA_KV_WIDTH), 1)
    span = ROW_TILE + 2 * WINDOW
    heads = [(g, hd) for g in range(SWA_GROUP) for hd in range(SWA_KV_HEADS)]

    def run(k, v, bias):
        vlane = lax.broadcasted_iota(jnp.int32, v.shape, 1)
        vaug = [jnp.where(vlane // SWA_HEAD_DIM == hd, v, jnp.ones_like(v)) for hd in range(SWA_KV_HEADS)]
        sinks = [sink_ref[SWA_GROUP * hd + g:SWA_GROUP * hd + g + 1, 0:1] for g, hd in heads]
        scores = []
        for g, hd in heads:
            qg = q[:, SWA_KV_WIDTH * g:SWA_KV_WIDTH * (g + 1)]
            s = _dot_nt(jnp.where(lane // SWA_HEAD_DIM == hd, qg, jnp.zeros_like(qg)), k)
            scores.append(s if bias is None else s + bias)
        maxes = [jnp.maximum(jnp.max(s, axis=-1, keepdims=True), sink) for s, sink in zip(scores, sinks)]
        outs = [_dot(jnp.exp2(s - mx).astype(BF16), vaug[hd]) for s, mx, (g, hd) in zip(scores, maxes, heads)]
        for g in range(SWA_GROUP):
            out = jnp.zeros((tq, SWA_KV_WIDTH), F32)
            for i, (gi, hd) in enumerate(heads):
                if gi == g:
                    den = pltpu.roll(outs[i], SWA_HEAD_DIM, 1) + jnp.exp2(sinks[i] - maxes[i])
                    out = out + jnp.where(lane // SWA_HEAD_DIM == hd, outs[i] * (1.0 / den), 0.0)
            o_ref[:, SWA_KV_WIDTH * g:SWA_KV_WIDTH * (g + 1)] = out.astype(BF16)

    @pl.when(j == 0)
    def _():
        run(k_ref[0:n_ctx, :], v_ref[0:n_ctx, :], None)

    @pl.when(j > 0)
    def _():
        ks = pl.multiple_of(_swa_span_start(j, n_lat)[0], WINDOW)
        k = jnp.concatenate([k_ref[0:n_ctx, :], k_ref[pl.ds(n_ctx + ks, span), :]], axis=0)
        v = jnp.concatenate([v_ref[0:n_ctx, :], v_ref[pl.ds(n_ctx + ks, span), :]], axis=0)
        run(k, v, bias_ref[...])


def _swa_attention(qc, kc, vc, sink_tab, n_batch, t_len, n_ctx):
    tps = t_len // ROW_TILE
    n_lat = t_len - n_ctx
    n_keys = n_ctx + ROW_TILE + 2 * WINDOW
    q3 = qc.reshape(n_batch, t_len, SWA_WIDTH)
    k3 = kc.reshape(n_batch, t_len, SWA_KV_WIDTH)
    v3 = vc.reshape(n_batch, t_len, SWA_KV_WIDTH)
    row = jnp.arange(ROW_TILE)[None, :, None]
    key = jnp.arange(n_keys)[None, None, :] - n_ctx
    lead = WINDOW * jnp.arange(3)[:, None, None]
    bias = jnp.where((key < 0) | (jnp.abs(row + lead - key) <= WINDOW), 0.0, NEG_BIG).astype(F32)

    def bias_idx(b, j):
        ks, q0 = _swa_span_start(jnp.maximum(j, 1), n_lat)
        return ((q0 - ks) // WINDOW, 0, 0)

    out = pl.pallas_call(
        functools.partial(_swa_kernel, n_ctx=n_ctx, n_lat=n_lat),
        grid=(n_batch, tps),
        in_specs=[
            pl.BlockSpec((None, ROW_TILE, SWA_WIDTH), lambda b, j: (b, j, 0)),
            pl.BlockSpec((None, t_len, SWA_KV_WIDTH), lambda b, j: (b, 0, 0)),
            pl.BlockSpec((None, t_len, SWA_KV_WIDTH), lambda b, j: (b, 0, 0)),
            pl.BlockSpec((SWA_Q_HEADS, 128), lambda b, j: (0, 0)),
            pl.BlockSpec((None, ROW_TILE, n_keys), bias_idx),
        ],
        out_specs=pl.BlockSpec((None, ROW_TILE, SWA_WIDTH), lambda b, j: (b, j, 0)),
        out_shape=jax.ShapeDtypeStruct((n_batch, t_len, SWA_WIDTH), BF16),
        compiler_params=_cparams(("parallel", "arbitrary")),
        name="swa_attention",
    )(q3, k3, v3, sink_tab, bias)
    return out.reshape(n_batch * t_len, SWA_WIDTH)


def _hgrn_kernel(q_ref, v_ref, lff_ref, lfb_ref, g_ref, ng_ref, o_ref, of_ref, ob_ref, st_ref, cum_ref, qrow_ref,
                 strong_ref, *, n_ctx, t_len):
    c = HGRN_CHUNK
    n_chunks = t_len // c
    n_ctx_chunks = n_ctx // c
    mid = c // 2
    ri = lax.broadcasted_iota(jnp.int32, (c, c), 0)
    ci = lax.broadcasted_iota(jnp.int32, (c, c), 1)
    causal = ri >= ci
    anti = ri <= ci
    tri_f = jnp.where(causal, 1.0, 0.0).astype(BF16)
    tri_b = jnp.where(anti, 1.0, 0.0).astype(BF16)

    st_ref[...] = jnp.zeros_like(st_ref)
    n_iter = n_chunks // HGRN_UNROLL

    def chains_of(i):
        chains = []
        for u in range(HGRN_UNROLL):
            step = i * HGRN_UNROLL + u
            rf = pl.multiple_of(step * c, c)
            cb = jnp.where(step < n_ctx_chunks, n_ctx_chunks - 1 - step, n_chunks - 1 + n_ctx_chunks - step)
            rb = pl.multiple_of(cb * c, c)
            for hd in range(HGRN_HEADS):
                cols = slice(HGRN_K * hd, HGRN_K * (hd + 1))
                chains.append((2 * hd, pl.ds(rf, c), cols, lff_ref, of_ref, tri_f, causal, mid - 1, c - 1))
                chains.append((2 * hd + 1, pl.ds(rb, c), cols, lfb_ref, ob_ref, tri_b, anti, mid, 0))
        return chains

    def max_half_decay(i):
        worst = jnp.zeros((1, HGRN_K), F32)
        for _, rows, cols, lf_ref, _, _, _, _, _ in chains_of(i):
            lf = lf_ref[rows, cols]
            worst = jnp.maximum(worst, jnp.maximum(jnp.abs(jnp.sum(lf[0:mid, :], axis=0, keepdims=True)),
                                                   jnp.abs(jnp.sum(lf[mid:c, :], axis=0, keepdims=True))))
        return jnp.max(worst)

    def exact_step(i):
        for slot, rows, cols, lf_ref, o_ref_d, tri, mask, _, end_row in chains_of(i):
            lf = lf_ref[rows, cols]
            lf_hi, lf_lo = _split_bf16(lf)
            cum = _dot(tri, lf_hi) + _dot(tri, lf_lo)
            tot = cum[end_row:end_row + 1, :]
            v = v_ref[rows, cols]
            k = 1.0 - jnp.exp(lf)
            q = q_ref[rows, cols].astype(F32)
            cum_ref[...] = cum
            qrow_ref[...] = q

            def row(t, sct):
                w = jnp.exp(jnp.minimum(cum_ref[pl.ds(t, 1), :] - cum, 0.0))
                col = jnp.sum(qrow_ref[pl.ds(t, 1), :] * k * w, axis=-1, keepdims=True)
                return jnp.where(ci == t, col, sct)

            sct = lax.fori_loop(0, c, row, jnp.zeros((c, c), F32))
            valid_t = anti if mask is causal else causal
            st = st_ref[slot]
            o = (_dot_tn(jnp.where(valid_t, sct, 0.0).astype(BF16), v)
                 + _dot_nt((q * jnp.exp(cum)).astype(BF16), st.astype(BF16)))
            o_ref_d[rows, cols] = o.astype(BF16)
            st_ref[slot] = st * jnp.exp(tot) + _dot_tn(v, (k * jnp.exp(tot - cum)).astype(BF16))

    def fast_step(i):
        chains = chains_of(i)
        cums = []
        for _, rows, cols, lf_ref, _, tri, _, _, _ in chains:
            lf_hi, lf_lo = _split_bf16(lf_ref[rows, cols])
            cums.append(_dot(tri, lf_hi) + _dot(tri, lf_lo))
        prods = []
        for (slot, rows, cols, lf_ref, _, _, mask, ref_row, end_row), cum in zip(chains, cums):
            ref = cum[ref_row:ref_row + 1, :]
            tot = cum[end_row:end_row + 1, :]
            v = v_ref[rows, cols]
            k = 1.0 - jnp.exp(lf_ref[rows, cols])
            qt = q_ref[rows, cols].astype(F32) * jnp.exp(cum - ref)
            kt = k * jnp.exp(ref - cum)
            sc = _dot_nt(qt.astype(BF16), kt.astype(BF16))
            upd = _dot_tn(v, (kt * jnp.exp(tot - ref)).astype(BF16))
            prods.append((sc, (qt * jnp.exp(ref)).astype(BF16), jnp.exp(tot), upd, v))
        states = {}
        for (slot, rows, cols, _, o_ref_d, _, mask, _, _), (sc, q_in, decay, upd, v) in zip(chains, prods):
            st = states[slot] if slot in states else st_ref[slot]
            o = _dot(jnp.where(mask, sc, 0.0).astype(BF16), v) + _dot_nt(q_in, st.astype(BF16))
            o_ref_d[rows, cols] = o.astype(BF16)
            states[slot] = st * decay + upd
        for slot, st in states.items():
            st_ref[slot] = st

    def test_decay(i):
        strong_ref[0] = (max_half_decay(i) > HGRN_SAFE_DECAY).astype(jnp.int32)

    def body(i, _):
        strong = strong_ref[0]

        @pl.when(strong == 0)
        def _():
            test_decay(jnp.minimum(i + 1, n_iter - 1))
            fast_step(i)

        @pl.when(strong != 0)
        def _():
            test_decay(jnp.minimum(i + 1, n_iter - 1))
            exact_step(i)

        return 0

    test_decay(0)
    lax.fori_loop(0, n_iter, body, 0)

    def fin(t, _):
        r = pl.multiple_of(t * ROW_TILE, ROW_TILE)
        for hd in range(HGRN_HEADS):
            cols = slice(HGRN_K * hd, HGRN_K * (hd + 1))
            o = of_ref[pl.ds(r, ROW_TILE), cols].astype(F32) + ob_ref[pl.ds(r, ROW_TILE), cols].astype(F32)
            ms = jnp.mean(o * o, axis=-1, keepdims=True)
            o = o * lax.rsqrt(ms + NORM_EPS) * ng_ref[...]
            o_ref[pl.ds(r, ROW_TILE), cols] = (o * g_ref[pl.ds(r, ROW_TILE), cols].astype(F32)).astype(BF16)
        return 0

    lax.fori_loop(0, t_len // ROW_TILE, fin, 0)


def _hgrn(hq, hv, lff, lfb, hg, norm_g, n_batch, t_len, n_ctx):
    def view(a):
        return a.reshape(n_batch, t_len, HGRN_WIDTH)

    blk = pl.BlockSpec((None, t_len, HGRN_WIDTH), lambda b: (b, 0, 0))
    out = pl.pallas_call(
        functools.partial(_hgrn_kernel, n_ctx=n_ctx, t_len=t_len),
        grid=(n_batch,),
        in_specs=[blk] * 5 + [pl.BlockSpec((1, HGRN_K), lambda b: (0, 0))],
        out_specs=blk,
        out_shape=jax.ShapeDtypeStruct((n_batch, t_len, HGRN_WIDTH), BF16),
        scratch_shapes=[pltpu.VMEM((t_len, HGRN_WIDTH), BF16), pltpu.VMEM((t_len, HGRN_WIDTH), BF16),
                        pltpu.VMEM((2 * HGRN_HEADS, HGRN_K, HGRN_K), F32),
                        pltpu.VMEM((HGRN_CHUNK, HGRN_K), F32), pltpu.VMEM((HGRN_CHUNK, HGRN_K), F32),
                        pltpu.SMEM((1,), jnp.int32)],
        compiler_params=_cparams(("parallel",)),
        name="hgrn2",
    )(view(hq), view(hv), view(lff), view(lfb), view(hg), norm_g)
    return out.reshape(n_batch * t_len, HGRN_WIDTH)


def _mix_residual_norm(x, a_ref, b_ref, c_ref, w_ref, mod_ref, g_ref):
    y = (_dot(a_ref[...], w_ref[0:256, :]) + _dot(b_ref[...], w_ref[256:768, :])
         + _dot(c_ref[...], w_ref[768:1024, :]))
    x = x + mod_ref[0:1, :] * y
    ms = jnp.mean(x * x, axis=-1, keepdims=True)
    h = (x * lax.rsqrt(ms + NORM_EPS) * g_ref[...]) * (1.0 + mod_ref[2:3, :]) + mod_ref[1:2, :]
    return x, h


def _outproj_router_kernel(x_ref, a_ref, b_ref, c_ref, w_ref, mod_ref, g_ref, r_ref, xo_ref, h_ref, gate_ref):
    x, h = _mix_residual_norm(x_ref[...], a_ref, b_ref, c_ref, w_ref, mod_ref, g_ref)
    xo_ref[...] = x
    h_ref[...] = _pack_bf16_pairs(h)
    h_hi, h_lo = _split_bf16(h)
    r_hi, r_lo = _split_bf16(r_ref[...])
    logits = _dot(h_hi, r_hi) + _dot(h_lo, r_hi) + _dot(h_hi, r_lo)
    lane = lax.broadcasted_iota(jnp.int32, logits.shape, 1).astype(F32)
    logits = jnp.where(lane < N_EXPERTS, logits, NEG_BIG)
    v1 = jnp.max(logits, axis=-1, keepdims=True)
    i1 = jnp.min(jnp.where(logits == v1, lane, 128.0), axis=-1, keepdims=True)
    rest_l = jnp.where(lane == i1, NEG_BIG, logits)
    v2 = jnp.max(rest_l, axis=-1, keepdims=True)
    i2 = jnp.min(jnp.where(rest_l == v2, lane, 128.0), axis=-1, keepdims=True)
    e2 = jnp.exp(v2 - v1)
    w1 = 1.0 / (1.0 + e2)
    w2 = e2 * w1
    gate_ref[...] = (jnp.where(lane == i1, w1, 0.0) + jnp.where(lane == i2, w2, 0.0)
                     + jnp.where(lane == 8.0, i1, 0.0) + jnp.where(lane == 9.0, i2, 0.0)
                     + jnp.where(lane == 10.0, w1, 0.0) + jnp.where(lane == 11.0, w2, 0.0))


def _out_projection_router(xc, a, b, c, w, mod, g, router):
    n = xc.shape[0]
    nt = n // ROW_TILE
    row = lambda i: (i, 0)
    const = lambda i: (0, 0)
    return pl.pallas_call(
        _outproj_router_kernel,
        grid=(nt,),
        in_specs=[
            pl.BlockSpec((ROW_TILE, D_MODEL), row),
            pl.BlockSpec((ROW_TILE, DIFF_WIDTH), row),
            pl.BlockSpec((ROW_TILE, HGRN_WIDTH), row),
            pl.BlockSpec((ROW_TILE, SWA_WIDTH), row),
            pl.BlockSpec((D_MODEL, D_MODEL), const),
            pl.BlockSpec((None, 3, D_MODEL), lambda i: (i, 0, 0)),
            pl.BlockSpec((1, D_MODEL), const),
            pl.BlockSpec((D_MODEL, 128), const),
        ],
        out_specs=[pl.BlockSpec((ROW_TILE, D_MODEL), row), pl.BlockSpec((ROW_TILE, D_MODEL // 2), row),
                   pl.BlockSpec((ROW_TILE, 128), row)],
        out_shape=[jax.ShapeDtypeStruct((n, D_MODEL), F32), jax.ShapeDtypeStruct((n, D_MODEL // 2), jnp.int32),
                   jax.ShapeDtypeStruct((n, 128), F32)],
        compiler_params=_cparams(("parallel",)),
        name="out_projection",
    )(xc, a, b, c, w, mod, g, router)


def _outproj_ffn_kernel(*refs, n_x, tiles_per_seq):
    a_ref, b_ref, c_ref, wo_ref, mod_ref, g_ref, w1_ref, w3_ref, w2_ref, o_ref = refs[n_x:]
    x, h = _mix_residual_norm(_token_rows(refs[:n_x], tiles_per_seq), a_ref, b_ref, c_ref, wo_ref, mod_ref, g_ref)
    h = h.astype(BF16)
    u = _dot(h, w1_ref[...])
    act = (_silu(u) * _dot(h, w3_ref[...])).astype(BF16)
    o_ref[...] = x + mod_ref[3:4, :] * _dot(act, w2_ref[...])


def _outproj_dense_ffn(xs, a, b, c, wo, mod, g, w1, w3, w2, tiles_per_seq):
    n = a.shape[0]
    nt = n // ROW_TILE
    row = lambda i: (i, 0)
    const = lambda i: (0, 0)
    return pl.pallas_call(
        functools.partial(_outproj_ffn_kernel, n_x=len(xs), tiles_per_seq=tiles_per_seq),
        grid=(nt,),
        in_specs=_token_row_specs(xs, tiles_per_seq) + [
            pl.BlockSpec((ROW_TILE, DIFF_WIDTH), row),
            pl.BlockSpec((ROW_TILE, HGRN_WIDTH), row),
            pl.BlockSpec((ROW_TILE, SWA_WIDTH), row),
            pl.BlockSpec((D_MODEL, D_MODEL), const),
            pl.BlockSpec((None, 4, D_MODEL), lambda i: (i, 0, 0)),
            pl.BlockSpec((1, D_MODEL), const),
            pl.BlockSpec((D_MODEL, D_FF), const),
            pl.BlockSpec((D_MODEL, D_FF), const),
            pl.BlockSpec((D_FF, D_MODEL), const),
        ],
        out_specs=pl.BlockSpec((ROW_TILE, D_MODEL), row),
        out_shape=jax.ShapeDtypeStruct((n, D_MODEL), F32),
        compiler_params=_cparams(("parallel",)),
        name="outproj_dense_ffn",
    )(*xs, a, b, c, wo, mod, g, w1, w3, w2)


MOE_TM = 256
SC_CORES = 2
SC_SUBCORES = 16
SC_GATHER_ROWS = 128
SC_SPILL_ROWS = 4096


def _pack_bf16_pairs(h):
    half = h.shape[1] // 2
    bits = pltpu.bitcast(h.astype(BF16).astype(F32), jnp.uint32)
    packed = (bits[:, :half] >> 16) | (bits[:, half:] & jnp.uint32(0xFFFF0000))
    return pltpu.bitcast(packed, jnp.int32)


def _unpack_bf16_pairs(u):
    bits = pltpu.bitcast(u, jnp.uint32)
    lo = pltpu.bitcast(bits << 16, F32)
    hi = pltpu.bitcast(bits & jnp.uint32(0xFFFF0000), F32)
    return jnp.concatenate([lo, hi], axis=1)


def _sc_gather_rows(table, idx):
    n_workers = SC_CORES * SC_SUBCORES
    n_rows, width = idx.shape[0], table.shape[1]
    assert n_rows % (n_workers * SC_GATHER_ROWS) == 0
    per_worker = n_rows // n_workers
    mesh = plsc.VectorSubcoreMesh(core_axis_name="c", subcore_axis_name="s",
                                  num_cores=SC_CORES, num_subcores=SC_SUBCORES)

    @functools.partial(
        pl.kernel, mesh=mesh,
        out_type=jax.ShapeDtypeStruct((n_rows, width), table.dtype),
        scratch_types=[pltpu.VMEM((SC_GATHER_ROWS,), jnp.int32),
                       pltpu.VMEM((SC_GATHER_ROWS, width), table.dtype),
                       pltpu.SemaphoreType.DMA],
        name="sc_gather_rows",
    )
    def gather(table_hbm, idx_hbm, out_hbm, idx_v, rows_v, sem):
        worker = lax.axis_index("s") * SC_CORES + lax.axis_index("c")
        base = worker * per_worker

        @pl.loop(0, per_worker // SC_GATHER_ROWS)
        def _(step):
            off = pl.multiple_of(base + step * SC_GATHER_ROWS, SC_GATHER_ROWS)
            pltpu.sync_copy(idx_hbm.at[pl.ds(off, SC_GATHER_ROWS)], idx_v)
            pltpu.async_copy(table_hbm.at[idx_v], rows_v, sem).wait()
            pltpu.sync_copy(rows_v, out_hbm.at[pl.ds(off, SC_GATHER_ROWS)])

    return gather(table, idx)


def _sc_scatter_rows(rows, idx, n_out):
    n_workers = SC_CORES * SC_SUBCORES
    n_rows, width = rows.shape
    assert n_rows % (n_workers * SC_GATHER_ROWS) == 0
    chunks_per_worker = n_rows // (n_workers * SC_GATHER_ROWS)
    idx = idx.reshape(2, n_rows // SC_GATHER_ROWS, SC_GATHER_ROWS)
    mesh = plsc.VectorSubcoreMesh(core_axis_name="c", subcore_axis_name="s",
                                  num_cores=SC_CORES, num_subcores=SC_SUBCORES)

    @functools.partial(
        pl.kernel, mesh=mesh,
        out_type=jax.ShapeDtypeStruct((n_out, width), rows.dtype),
        scratch_types=[pltpu.VMEM((SC_GATHER_ROWS,), jnp.int32), pltpu.VMEM((SC_GATHER_ROWS,), jnp.int32),
                       pltpu.VMEM((SC_GATHER_ROWS, width), rows.dtype),
                       pltpu.SemaphoreType.DMA, pltpu.SemaphoreType.DMA],
        name="sc_scatter_rows",
    )
    def scatter(rows_hbm, idx_hbm, out_hbm, idx0_v, idx1_v, rows_v, sem0, sem1):
        worker = lax.axis_index("s") * SC_CORES + lax.axis_index("c")

        @pl.loop(0, chunks_per_worker)
        def _(step):
            chunk = worker * chunks_per_worker + step
            off = pl.multiple_of(chunk * SC_GATHER_ROWS, SC_GATHER_ROWS)
            pltpu.sync_copy(rows_hbm.at[pl.ds(off, SC_GATHER_ROWS)], rows_v)
            pltpu.sync_copy(idx_hbm.at[0, chunk], idx0_v)
            first = pltpu.async_copy(rows_v, out_hbm.at[idx0_v], sem0)
            pltpu.sync_copy(idx_hbm.at[1, chunk], idx1_v)
            second = pltpu.async_copy(rows_v, out_hbm.at[idx1_v], sem1)
            first.wait()
            second.wait()

    return scatter(rows, idx)


def _moe_expert_kernel(te_ref, tv_ref, xs_ref, w1_ref, w3_ref, w2_ref, ys_ref):
    r = pl.program_id(0)

    @pl.when(tv_ref[r] == 1)
    def _():
        h = _unpack_bf16_pairs(xs_ref[...]).astype(BF16)
        u = _dot(h, w1_ref[...])
        act = (_silu(u) * _dot(h, w3_ref[...])).astype(BF16)
        ys_ref[...] = _pack_bf16_pairs(_dot(act, w2_ref[...]))

    @pl.when(tv_ref[r] == 0)
    def _():
        ys_ref[...] = jnp.zeros_like(ys_ref)


def _moe_combine_kernel(x_ref, y0_ref, y1_ref, route_ref, mod_ref, o_ref):
    g0, g1 = route_ref[:, 10:11], route_ref[:, 11:12]
    y = (jnp.where(g0 != 0.0, g0 * _unpack_bf16_pairs(y0_ref[...]), 0.0)
         + jnp.where(g1 != 0.0, g1 * _unpack_bf16_pairs(y1_ref[...]), 0.0))
    o_ref[...] = x_ref[...] + mod_ref[...] * y


def _moe_plan(route, n_rt):
    n = route.shape[0]
    tm = MOE_TM
    gates_t = route[:, :N_EXPERTS].T
    sel = gates_t != 0.0
    si = sel.astype(jnp.int32)
    rank = jnp.cumsum(si, axis=1) - si
    ntile = (jnp.sum(si, axis=1) + tm - 1) // tm
    tile_end = jnp.cumsum(ntile)
    tile_off = tile_end - ntile
    used = tile_end[-1]
    n_rows = n_rt * tm
    dest = jnp.where(sel, tile_off[:, None] * tm + rank, n_rows)
    expert = jnp.arange(N_EXPERTS, dtype=jnp.int32)[:, None]

    def row_of(lane):
        pick = route[:, lane].astype(jnp.int32)[None, :]
        return jnp.sum(jnp.where(expert == pick, dest, 0), axis=0).astype(jnp.int32)

    pos = jnp.stack([row_of(8), row_of(9)])
    r = jnp.arange(n_rt, dtype=jnp.int32)
    rc = jnp.minimum(r, used - 1)
    tile_expert = jnp.sum((tile_end[None, :] <= rc[:, None]).astype(jnp.int32), axis=1)
    tile_valid = (r < used).astype(jnp.int32)
    return pos, tile_expert, tile_valid


def _moe_ffn(x, h_packed, route, w1, w3, w2, layer_idx, mod, latent_tiles_per_seq=None):
    n = x.shape[0]
    nt = n // ROW_TILE
    tm = MOE_TM
    half = D_MODEL // 2
    n_rt = 2 * n // tm + N_EXPERTS
    pos, tile_expert, tile_valid = _moe_plan(route, n_rt)
    routed = pos < n_rt * tm
    token = jnp.arange(n, dtype=jnp.int32)[None, :]

    xs = _sc_scatter_rows(h_packed, jnp.where(routed, pos, n_rt * tm + token % SC_SPILL_ROWS),
                          n_rt * tm + SC_SPILL_ROWS)
    expert_w = lambda r, te, tv: (layer_idx, te[r], 0, 0)
    ys = pl.pallas_call(
        _moe_expert_kernel,
        grid_spec=pltpu.PrefetchScalarGridSpec(
            num_scalar_prefetch=2,
            grid=(n_rt,),
            in_specs=[
                pl.BlockSpec((tm, half), lambda r, te, tv: (r, 0)),
                pl.BlockSpec((None, None, D_MODEL, D_FF), expert_w),
                pl.BlockSpec((None, None, D_MODEL, D_FF), expert_w),
                pl.BlockSpec((None, None, D_FF, D_MODEL), expert_w),
            ],
            out_specs=pl.BlockSpec((tm, half), lambda r, te, tv: (r, 0)),
        ),
        out_shape=jax.ShapeDtypeStruct((n_rt * tm, half), jnp.int32),
        compiler_params=_cparams(("arbitrary",)),
        name="moe_experts",
    )(tile_expert, tile_valid, xs, w1, w3, w2)

    y = _sc_gather_rows(ys, jnp.where(routed, pos, token).reshape(-1))
    row = lambda i: (i, 0)
    if latent_tiles_per_seq is None:
        out_rows, out_row = n, row
    else:
        lt = latent_tiles_per_seq
        out_rows = n // (lt + 1) * lt
        out_row = lambda i: ((i // (lt + 1)) * lt + jnp.maximum(i % (lt + 1) - 1, 0), 0)
    return pl.pallas_call(
        _moe_combine_kernel,
        grid=(nt,),
        in_specs=[
            pl.BlockSpec((ROW_TILE, D_MODEL), row),
            pl.BlockSpec((ROW_TILE, half), row),
            pl.BlockSpec((ROW_TILE, half), lambda i: (nt + i, 0)),
            pl.BlockSpec((ROW_TILE, 128), row),
            pl.BlockSpec((None, 1, D_MODEL), lambda i: (i, 0, 0)),
        ],
        out_specs=pl.BlockSpec((ROW_TILE, D_MODEL), out_row),
        out_shape=jax.ShapeDtypeStruct((out_rows, D_MODEL), F32),
        compiler_params=_cparams(("arbitrary",)),
        name="moe_combine",
    )(x, y, y, route, mod)


def _rope_tables(n_ctx, n_lat):
    pos_r = jnp.arange(n_lat, dtype=jnp.int32) // GRID_W
    pos_c = jnp.arange(n_lat, dtype=jnp.int32) % GRID_W

    def per_head(head_dim):
        nf = head_dim // 4
        inv = ROPE_BASE ** (-jnp.arange(nf, dtype=F32) / nf)
        ang_r = pos_r.astype(F32)[:, None] * inv[None, :]
        ang_c = pos_c.astype(F32)[:, None] * inv[None, :]
        z = jnp.zeros_like(ang_r)
        cos = jnp.concatenate([jnp.cos(ang_r)] * 2 + [jnp.cos(ang_c)] * 2, axis=-1)
        sa = jnp.concatenate([-jnp.sin(ang_r), z, -jnp.sin(ang_c), z], axis=-1)
        sb = jnp.concatenate([z, jnp.sin(ang_r), z, jnp.sin(ang_c)], axis=-1)
        reps = 256 // head_dim
        tabs = [jnp.tile(t, (1, reps)) for t in (cos, sa, sb)]
        ctx = [jnp.ones((n_ctx, 256), F32), jnp.zeros((n_ctx, 256), F32), jnp.zeros((n_ctx, 256), F32)]
        return [jnp.concatenate([c, t], axis=0) for c, t in zip(ctx, tabs)]

    return jnp.concatenate(per_head(DIFF_HEAD_DIM) + per_head(SWA_HEAD_DIM), axis=-1)


def _block_diag_mean(group):
    idx = jnp.arange(256) // group
    return jnp.where(idx[:, None] == idx[None, :], 1.0 / group, 0.0).astype(BF16)


def kernel(x, c, ctx, c_ctx, ada_w, ada_b, norm_mix_g, norm_ffn_g, w_in, w_out, diff_qk_norm_g, diff_lambda,
           diff_subln_g, hgrn_lb_logits, hgrn_norm_g, swa_qk_norm_g, swa_sink, ffn_w1, ffn_w3, ffn_w2,
           moe_router, moe_w1, moe_w3, moe_w2):
    n_batch, n_lat, _ = x.shape
    n_ctx = ctx.shape[1]
    depth = ada_w.shape[0]
    t_len = n_ctx + n_lat
    tps = t_len // ROW_TILE
    assert n_ctx == ROW_TILE and n_lat % ROW_TILE == 0 and n_lat >= ROW_TILE + 2 * WINDOW

    xs = (x.reshape(n_batch * n_lat, D_MODEL), ctx.reshape(n_batch * n_ctx, D_MODEL))

    n_rows = -(-(n_batch + 1) // 8) * 8
    cond = jnp.concatenate([c, c_ctx[None, :], jnp.zeros((n_rows - n_batch - 1, D_MODEL), F32)], axis=0)
    mods = _ada_modulation(cond, ada_w, ada_b).reshape(depth, n_rows, 6, D_MODEL)
    m_lat = jnp.broadcast_to(mods[:, :n_batch, None], (depth, n_batch, tps - 1, 6, D_MODEL))
    m_ctx = jnp.broadcast_to(mods[:, n_batch, None, None], (depth, n_batch, 1, 6, D_MODEL))
    mods = jnp.concatenate([m_ctx, m_lat], axis=2).reshape(depth, n_batch * tps, 6, D_MODEL)

    lb = jnp.cumsum(jax.nn.softmax(hgrn_lb_logits.astype(F32), axis=1), axis=1)
    lb = lb - lb[:, :1]
    rope = _rope_tables(n_ctx, n_lat)
    g32 = _block_diag_mean(DIFF_HEAD_DIM)
    g64 = _block_diag_mean(SWA_HEAD_DIM)

    perm_q = jnp.arange(SWA_WIDTH).reshape(SWA_KV_HEADS, SWA_GROUP, SWA_HEAD_DIM).transpose(1, 0, 2).reshape(-1)
    qc0 = 3 * DIFF_WIDTH + 5 * HGRN_WIDTH
    col_perm = jnp.concatenate([jnp.arange(qc0), qc0 + perm_q, jnp.arange(qc0 + SWA_WIDTH, D_IN)])
    oc0 = DIFF_WIDTH + HGRN_WIDTH
    row_perm = jnp.concatenate([jnp.arange(oc0), oc0 + perm_q])
    moe_w1_b, moe_w3_b, moe_w2_b = moe_w1.astype(BF16), moe_w3.astype(BF16), moe_w2.astype(BF16)

    for layer in range(depth):
        lam_init = 0.8 - 0.6 * math.exp(-0.3 * layer)
        mod = mods[layer]
        w_in_l = w_in[layer][:, col_perm].astype(BF16)
        w_out_l = w_out[layer][row_perm, :].astype(BF16)
        gains = jnp.stack([
            jnp.tile(diff_qk_norm_g[layer, 0], 8) * (DIFF_HEAD_DIM ** -0.5 * LOG2_E),
            jnp.tile(diff_qk_norm_g[layer, 1], 8),
            jnp.tile(swa_qk_norm_g[layer, 0], 4) * (SWA_HEAD_DIM ** -0.5 * LOG2_E),
            jnp.tile(swa_qk_norm_g[layer, 1], 4),
        ]).astype(F32)
        lbt = jnp.stack([
            jnp.log(lb[:, layer]).reshape(-1),
            jnp.log1p(-lb[:, layer]).reshape(-1),
        ]).astype(F32)
        lv = diff_lambda[layer].astype(F32)
        lam = jnp.exp(jnp.sum(lv[0] * lv[1])) - jnp.exp(jnp.sum(lv[2] * lv[3])) + lam_init
        lam_row = jnp.full((1, 128), lam, F32)
        post_row = (jnp.tile(diff_subln_g[layer], DIFF_HEADS) * (1.0 - lam_init)).reshape(1, DIFF_WIDTH).astype(F32)
        sink_tab = jnp.broadcast_to(swa_sink[layer].astype(F32)[:, None] * LOG2_E, (SWA_Q_HEADS, 128))

        (qa, ka, va, hq, hv, lff, lfb, hg, qc, kc, vc) = _in_projection(
            xs, mod[:, 0:2], norm_mix_g[layer].reshape(1, D_MODEL), w_in_l, rope, gains, lbt, g32, g64, tps)
        a = _diff_attention(qa, ka, va, lam_row, post_row, g64, n_batch, t_len, n_ctx)
        b = _hgrn(hq, hv, lff, lfb, hg, hgrn_norm_g[layer].reshape(1, HGRN_K), n_batch, t_len, n_ctx)
        cc = _swa_attention(qc, kc, vc, sink_tab, n_batch, t_len, n_ctx)

        jj = layer // 2
        g2 = norm_ffn_g[layer].reshape(1, D_MODEL)
        if layer % 2 == 1:
            router = jnp.pad(moe_router[jj].astype(F32), ((0, 0), (0, 128 - N_EXPERTS)))
            x_mid, h2, route = _out_projection_router(xs[0], a, b, cc, w_out_l, mod[:, 2:5], g2, router)
            if layer == depth - 1:
                is_lat = (jnp.arange(n_batch * t_len, dtype=jnp.int32) % t_len) >= n_ctx
                route = jnp.where(is_lat[:, None], route, 0.0)
            xc = _moe_ffn(x_mid, h2, route, moe_w1_b, moe_w3_b, moe_w2_b, jj, mod[:, 5:6],
                          latent_tiles_per_seq=tps - 1 if layer == depth - 1 else None)
        else:
            xc = _outproj_dense_ffn(xs, a, b, cc, w_out_l, mod[:, 2:6], g2, ffn_w1[jj].astype(BF16),
                                    ffn_w3[jj].astype(BF16), ffn_w2[jj].astype(BF16), tps)
        xs = (xc,)

    if depth % 2 == 0:
        return xc.reshape(n_batch, n_lat, D_MODEL)
    return xc.reshape(n_batch, t_len, D_MODEL)[:, n_ctx:, :]
```
